```python
import math, functools
import jax, jax.numpy as jnp
from jax import lax
import numpy as np

D_MODEL = 2048
BATCH = 16
SEQ = 2048
DEPTH = 2

MIX_WIDTH = D_MODEL
GDN_HEAD_DIM = 128
GDN_HEADS = MIX_WIDTH // (2 * GDN_HEAD_DIM)
GDN_W = GDN_HEADS * GDN_HEAD_DIM
RET_HEAD_DIM = 256
RET_HEADS = MIX_WIDTH // (2 * RET_HEAD_DIM)
RET_W = RET_HEADS * RET_HEAD_DIM
EV_IN_SIZES = (3 * GDN_W, GDN_HEADS, GDN_HEADS, GDN_W, RET_W, RET_W, RET_W, RET_W)
EV_IN_WIDTH = 4 * GDN_W + 2 * GDN_HEADS + 4 * RET_W
CHUNK = 64
CONV_WIDTH = 4
ROPE_BASE = 10000.0
LRU_WIDTH = D_MODEL
LRU_BLOCK = 256
LRU_BLOCKS = LRU_WIDTH // LRU_BLOCK
LRU_C = 8.0
D_FF = (11 * D_MODEL) // 4
N_EVEN = (DEPTH + 1) // 2
N_ODD = DEPTH // 2
N_SUB = 3
EPS = 1e-6

kernel_name = 'hybrid_gdn_retention_rglru_macaron_block'


def rms_norm(x, gain, eps=EPS):
    xf = x.astype(jnp.float32)
    y = xf * lax.rsqrt(jnp.mean(xf * xf, axis=-1, keepdims=True) + eps)
    return (y * gain.astype(jnp.float32)).astype(x.dtype)


def l2_normalize(x, eps=EPS):
    return x * lax.rsqrt(jnp.sum(x * x, axis=-1, keepdims=True) + eps)


def causal_depthwise_conv(x, w):
    width, t_len = w.shape[0], x.shape[1]
    xp = jnp.pad(x, ((0, 0), (width - 1, 0), (0, 0)))
    out = xp[:, 0:t_len] * w[0]
    for tap in range(1, width):
        out = out + xp[:, tap:tap + t_len] * w[tap]
    return out


def rotary(x, pos):
    half = x.shape[-1] // 2
    inv_freq = ROPE_BASE ** (-jnp.arange(half, dtype=jnp.float32) / half)
    ang = pos[:, None] * inv_freq[None, :]
    cos, sin = jnp.cos(ang)[None, :, None, :], jnp.sin(ang)[None, :, None, :]
    x1, x2 = x[..., :half], x[..., half:]
    return jnp.concatenate([x1 * cos - x2 * sin, x2 * cos + x1 * sin], axis=-1)


def heads_first(t, n_heads):
    b, t_len, _ = t.shape
    return t.reshape(b, t_len, n_heads, -1).transpose(0, 2, 1, 3)


def to_chunks(t, n_chunks):
    return t.reshape(t.shape[0], t.shape[1], n_chunks, CHUNK, t.shape[-1])


def chunk_gated_delta_rule(q, k, v, log_decay, beta):
    b, h, t_len, dk = q.shape
    dv = v.shape[-1]
    n = t_len // CHUNK
    q, k, v = to_chunks(q * dk ** -0.5, n), to_chunks(k, n), to_chunks(v, n)
    g = jnp.cumsum(log_decay.reshape(b, h, n, CHUNK), axis=-1)
    beta = beta.reshape(b, h, n, CHUNK)[..., None]
    idx = jnp.arange(CHUNK)
    causal = idx[:, None] >= idx[None, :]
    strict = idx[:, None] > idx[None, :]
    decay = jnp.exp(jnp.where(causal, g[..., :, None] - g[..., None, :], -jnp.inf))
    k_beta = k * beta
    a_mat = jnp.where(strict, jnp.einsum('bhnid,bhnjd->bhnij', k_beta, k) * decay, 0.0)
    rhs = jnp.concatenate([v * beta, k_beta * jnp.exp(g)[..., None]], axis=-1)
    sol = lax.linalg.triangular_solve(a_mat, rhs, left_side=True, lower=True, unit_diagonal=True)
    u, w = sol[..., :dv], sol[..., dv:]
    qk = jnp.where(causal, jnp.einsum('bhnid,bhnjd->bhnij', q, k) * decay, 0.0)
    g_last = g[..., -1:]
    q_st = q * jnp.exp(g)[..., None]
    k_st = k * jnp.exp(g_last - g)[..., None]
    chunk_decay = jnp.exp(g_last)[..., None]

    def step(state, inp):
        q_c, k_c, u_c, w_c, qk_c, dec_c = inp
        v_new = u_c - jnp.einsum('bhik,bhkv->bhiv', w_c, state)
        o = jnp.einsum('bhik,bhkv->bhiv', q_c, state) + jnp.einsum('bhij,bhjv->bhiv', qk_c, v_new)
        state = state * dec_c + jnp.einsum('bhjk,bhjv->bhkv', k_c, v_new)
        return state, o

    xs = tuple(jnp.moveaxis(t, 2, 0) for t in (q_st, k_st, u, w, qk, chunk_decay))
    _, o = lax.scan(step, jnp.zeros((b, h, dk, dv), q.dtype), xs)
    return jnp.moveaxis(o, 0, 2).reshape(b, h, t_len, dv)


def chunk_retention(q, k, v):
    b, h, t_len, dk = q.shape
    dv = v.shape[-1]
    n = t_len // CHUNK
    log_gamma = jnp.log1p(-jnp.exp2(-5.0 - jnp.arange(h, dtype=jnp.float32)))
    q, k, v = to_chunks(q, n), to_chunks(k * dk ** -0.5, n), to_chunks(v, n)
    idx = jnp.arange(CHUNK, dtype=jnp.float32)
    causal = idx[:, None] >= idx[None, :]
    intra_decay = jnp.exp(jnp.where(causal, (idx[:, None] - idx[None, :]) * log_gamma[:, None, None], -jnp.inf))
    scores = jnp.einsum('bhnid,bhnjd->bhnij', q, k) * intra_decay[:, None]
    o_intra = jnp.einsum('bhnij,bhnjd->bhnid', scores, v)
    lg4 = log_gamma[:, None, None, None]
    q_in = q * jnp.exp((idx[:, None] + 1.0) * lg4)
    k_st = k * jnp.exp((CHUNK - 1.0 - idx[:, None]) * lg4)
    chunk_decay = jnp.exp(CHUNK * log_gamma)[:, None, None]

    def step(state, inp):
        q_c, k_c, v_c = inp
        o = jnp.einsum('bhid,bhde->bhie', q_c, state)
        state = state * chunk_decay + jnp.einsum('bhjd,bhje->bhde', k_c, v_c)
        return state, o

    xs = tuple(jnp.moveaxis(t, 2, 0) for t in (q_in, k_st, v))
    _, o_inter = lax.scan(step, jnp.zeros((b, h, dk, dv), q.dtype), xs)
    return (o_intra + jnp.moveaxis(o_inter, 0, 2)).reshape(b, h, t_len, dv)


def gdn_retention_mixer(h, w_in, conv_w, a_log, dt_bias, o_norm, ret_norm, w_out):
    b, t_len, _ = h.shape
    f32 = jnp.float32
    cuts = [int(o) for o in np.cumsum(EV_IN_SIZES)[:-1]]
    qkv_a, beta_in, alpha_in, z_a, q_b, k_b, v_b, g_b = jnp.split(h @ w_in, cuts, axis=-1)
    qkv = jax.nn.silu(causal_depthwise_conv(qkv_a, conv_w)).astype(f32)
    q_a, k_a, v_a = [heads_first(t, GDN_HEADS) for t in jnp.split(qkv, 3, axis=-1)]
    q_a, k_a = l2_normalize(q_a), l2_normalize(k_a)
    beta = jax.nn.sigmoid(beta_in.astype(f32)).transpose(0, 2, 1)
    log_decay = (-jnp.exp(a_log.astype(f32)) * jax.nn.softplus(alpha_in.astype(f32) + dt_bias.astype(f32))).transpose(0, 2, 1)
    o_a = chunk_gated_delta_rule(q_a, k_a, v_a, log_decay, beta).transpose(0, 2, 1, 3)
    z = z_a.astype(f32).reshape(b, t_len, GDN_HEADS, GDN_HEAD_DIM)
    o_a = (rms_norm(o_a, o_norm) * jax.nn.silu(z)).reshape(b, t_len, GDN_W)
    pos = jnp.arange(t_len, dtype=f32)
    q_r = rotary(q_b.astype(f32).reshape(b, t_len, RET_HEADS, RET_HEAD_DIM), pos).transpose(0, 2, 1, 3)
    k_r = rotary(k_b.astype(f32).reshape(b, t_len, RET_HEADS, RET_HEAD_DIM), pos).transpose(0, 2, 1, 3)
    v_r = heads_first(v_b.astype(f32), RET_HEADS)
    o_b = chunk_retention(q_r, k_r, v_r).transpose(0, 2, 1, 3)
    gate_b = jax.nn.silu(g_b.astype(f32).reshape(b, t_len, RET_HEADS, RET_HEAD_DIM))
    o_b = (rms_norm(o_b, ret_norm.reshape(RET_HEADS, RET_HEAD_DIM)) * gate_b).reshape(b, t_len, RET_W)
    o = jnp.concatenate([o_a, o_b], axis=-1).astype(h.dtype)
    return o @ w_out


def rglru_mixer(h, w_in, conv_w, conv_b, gate_a_w, gate_a_b, gate_x_w, gate_x_b, lam, w_out):
    b, t_len, _ = h.shape
    f32 = jnp.float32
    y_branch, x_branch = jnp.split(h @ w_in, 2, axis=-1)
    y = jax.nn.gelu(y_branch, approximate=True)
    xc = (causal_depthwise_conv(x_branch, conv_w) + conv_b).astype(f32)
    xb = xc.reshape(b, t_len, LRU_BLOCKS, LRU_BLOCK)
    r = jax.nn.sigmoid(jnp.einsum('btni,nij->btnj', xb, gate_a_w.astype(f32)).reshape(b, t_len, LRU_WIDTH) + gate_a_b.astype(f32))
    i = jax.nn.sigmoid(jnp.einsum('btni,nij->btnj', xb, gate_x_w.astype(f32)).reshape(b, t_len, LRU_WIDTH) + gate_x_b.astype(f32))
    log_a = -LRU_C * r * jax.nn.softplus(-lam.astype(f32))
    a = jnp.exp(log_a)
    inp = jnp.sqrt(-jnp.expm1(2.0 * log_a)) * (i * xc)

    def combine(left, right):
        a_l, b_l = left
        a_r, b_r = right
        return a_l * a_r, a_r * b_l + b_r

    _, hs = lax.associative_scan(combine, (a, inp), axis=1)
    return (hs.astype(h.dtype) * y) @ w_out


def swiglu_ffn(h, w13, w2):
    gate, up = jnp.split(h @ w13, 2, axis=-1)
    return (jax.nn.silu(gate) * up) @ w2


def sandwich(x, fn, mod_s, g_pre, g_post, res_w):
    shift, scale, gate = mod_s[:, 0, None], mod_s[:, 1, None], mod_s[:, 2, None]
    h = rms_norm(x, g_pre) * (1.0 + scale) + shift
    return x + res_w * (1.0 + gate) * rms_norm(fn(h), g_post)


def setup_inputs(seed: int = 0) -> dict:
    key = jax.random.key(seed)
    ks = jax.random.split(key, 24)
    f32 = jnp.float32

    def nrm(k, shape, scale):
        return scale * jax.random.normal(k, shape, f32)

    dt = jnp.exp(jax.random.uniform(ks[11], (N_EVEN, GDN_HEADS), f32, math.log(1e-3), math.log(1e-1)))
    a_pow = jax.random.uniform(ks[22], (N_ODD, LRU_WIDTH), f32, 0.9, 0.999)
    s = a_pow ** (1.0 / LRU_C)
    return {
        'x': nrm(ks[0], (BATCH, SEQ, D_MODEL), 1.0),
        'c': nrm(ks[1], (BATCH, D_MODEL), 1.0),
        'ada_w': nrm(ks[2], (DEPTH, D_MODEL, N_SUB * 3 * D_MODEL), 0.5 * D_MODEL ** -0.5),
        'ada_b': nrm(ks[3], (DEPTH, N_SUB * 3 * D_MODEL), 0.01),
        'norm_pre': 1.0 + nrm(ks[4], (DEPTH, N_SUB, D_MODEL), 0.05),
        'norm_post': 1.0 + nrm(ks[5], (DEPTH, N_SUB, D_MODEL), 0.05),
        'ffn_w13': nrm(ks[6], (DEPTH, 2, D_MODEL, 2 * D_FF), D_MODEL ** -0.5),
        'ffn_w2': nrm(ks[7], (DEPTH, 2, D_FF, D_MODEL), D_FF ** -0.5),
        'ev_w_in': nrm(ks[8], (N_EVEN, D_MODEL, EV_IN_WIDTH), D_MODEL ** -0.5),
        'ev_conv_w': nrm(ks[9], (N_EVEN, CONV_WIDTH, 3 * GDN_W), CONV_WIDTH ** -0.5),
        'ev_a_log': jnp.log(jax.random.uniform(ks[10], (N_EVEN, GDN_HEADS), f32, 1.0, 16.0)),
        'ev_dt_bias': dt + jnp.log(-jnp.expm1(-dt)),
        'ev_o_norm': 1.0 + nrm(ks[12], (N_EVEN, GDN_HEAD_DIM), 0.05),
        'ev_ret_norm': 1.0 + nrm(ks[13], (N_EVEN, RET_W), 0.05),
        'ev_w_out': nrm(ks[14], (N_EVEN, MIX_WIDTH, D_MODEL), MIX_WIDTH ** -0.5),
        'od_w_in': nrm(ks[15], (N_ODD, D_MODEL, 2 * LRU_WIDTH), D_MODEL ** -0.5),
        'od_conv_w': nrm(ks[16], (N_ODD, CONV_WIDTH, LRU_WIDTH), CONV_WIDTH ** -0.5),
        'od_conv_b': nrm(ks[17], (N_ODD, LRU_WIDTH), 0.01),
        'od_gate_a_w': nrm(ks[18], (N_ODD, LRU_BLOCKS, LRU_BLOCK, LRU_BLOCK), LRU_BLOCK ** -0.5),
        'od_gate_a_b': nrm(ks[19], (N_ODD, LRU_WIDTH), 0.01),
        'od_gate_x_w': nrm(ks[20], (N_ODD, LRU_BLOCKS, LRU_BLOCK, LRU_BLOCK), LRU_BLOCK ** -0.5),
        'od_gate_x_b': nrm(ks[21], (N_ODD, LRU_WIDTH), 0.01),
        'od_lambda': jnp.log(s) - jnp.log1p(-s),
        'od_w_out': nrm(ks[23], (N_ODD, LRU_WIDTH, D_MODEL), LRU_WIDTH ** -0.5),
    }


def reference(x, c, ada_w, ada_b, norm_pre, norm_post, ffn_w13, ffn_w2,
              ev_w_in, ev_conv_w, ev_a_log, ev_dt_bias, ev_o_norm, ev_ret_norm, ev_w_out,
              od_w_in, od_conv_w, od_conv_b, od_gate_a_w, od_gate_a_b, od_gate_x_w, od_gate_x_b,
              od_lambda, od_w_out):
    b = x.shape[0]
    for layer in range(DEPTH):
        mod = (jax.nn.silu(c) @ ada_w[layer] + ada_b[layer]).reshape(b, N_SUB, 3, D_MODEL)
        ffn_pre = functools.partial(swiglu_ffn, w13=ffn_w13[layer, 0], w2=ffn_w2[layer, 0])
        ffn_post = functools.partial(swiglu_ffn, w13=ffn_w13[layer, 1], w2=ffn_w2[layer, 1])
        if layer % 2 == 0:
            e = layer // 2
            mixer = functools.partial(gdn_retention_mixer, w_in=ev_w_in[e], conv_w=ev_conv_w[e],
                                      a_log=ev_a_log[e], dt_bias=ev_dt_bias[e], o_norm=ev_o_norm[e],
                                      ret_norm=ev_ret_norm[e], w_out=ev_w_out[e])
        else:
            o = layer // 2
            mixer = functools.partial(rglru_mixer, w_in=od_w_in[o], conv_w=od_conv_w[o], conv_b=od_conv_b[o],
                                      gate_a_w=od_gate_a_w[o], gate_a_b=od_gate_a_b[o],
                                      gate_x_w=od_gate_x_w[o], gate_x_b=od_gate_x_b[o],
                                      lam=od_lambda[o], w_out=od_w_out[o])
        x = sandwich(x, ffn_pre, mod[:, 0], norm_pre[layer, 0], norm_post[layer, 0], 0.5)
        x = sandwich(x, mixer, mod[:, 1], norm_pre[layer, 1], norm_post[layer, 1], 1.0)
        x = sandwich(x, ffn_post, mod[:, 2], norm_pre[layer, 2], norm_post[layer, 2], 0.5)
    return x
```

```python
import functools
import math

import jax
import jax.numpy as jnp
from jax import lax
from jax.experimental import pallas as pl
from jax.experimental.pallas import tpu as pltpu

F32 = jnp.float32
BF16 = jnp.bfloat16

EPS = 1e-6
GDN_HEAD_DIM = 128
RET_HEAD_DIM = 256
CHUNK = 64
CONV_WIDTH = 4
ROPE_BASE = 10000.0
LRU_C = 8.0
N_SUB = 3

SUBLANES = 8
LANES = 128
V7X_VMEM_LIMIT_BYTES = 56 * 1024 * 1024


def _cparams(*semantics):
    return pltpu.CompilerParams(dimension_semantics=semantics,
                                vmem_limit_bytes=V7X_VMEM_LIMIT_BYTES)


def _dot(a, b):
    return jnp.dot(a, b, preferred_element_type=F32)


def _dot_nt(a, b):
    return lax.dot_general(a, b, (((1,), (1,)), ((), ())), preferred_element_type=F32)


def _dot_tn(a, b):
    return lax.dot_general(a, b, (((0,), (0,)), ((), ())), preferred_element_type=F32)


def _dot_f32(a, b):
    return jnp.dot(a, b, preferred_element_type=F32, precision=lax.Precision.HIGHEST)


def _sigmoid(x):
    return jax.nn.sigmoid(x)


def _silu(x):
    return x * jax.nn.sigmoid(x)


def _softplus(x):
    return jnp.maximum(x, 0.0) + jnp.log1p(jnp.exp(-jnp.abs(x)))


def _gelu_tanh(x):
    c = math.sqrt(2.0 / math.pi)
    return 0.5 * x * (1.0 + jnp.tanh(c * (x + 0.044715 * (x * x * x))))


def _rms(x, gain):
    return x * lax.rsqrt(jnp.mean(x * x, axis=-1, keepdims=True) + EPS) * gain


def _norm_mod(x, g_pre, mod_ref):
    return _rms(x, g_pre) * (1.0 + mod_ref[0, 1:2, :]) + mod_ref[0, 0:1, :]


def _post_residual(x, f, g_post, mod_ref, res_w):
    return x + (res_w * (1.0 + mod_ref[0, 2:3, :])) * _rms(f, g_post)


def _pick(n, pref):
    if n <= pref:
        return n
    for t in range(pref - pref % LANES, 0, -LANES):
        if n % t == 0:
            return t
    raise ValueError(f"no lane-aligned tile of {n} at or below {pref}")


def _ada_kernel(c_ref, w_ref, b_ref, o_ref):
    a = _silu(c_ref[...]).astype(BF16)
    o_ref[0] = _dot(a, w_ref[0].astype(BF16)) + b_ref[0]


def _ada(c, ada_w, ada_b):
    n_layers, d, n = ada_w.shape
    b = c.shape[0]
    tn = _pick(n, 1024)
    return pl.pallas_call(
        _ada_kernel,
        grid=(n_layers, n // tn),
        in_specs=[
            pl.BlockSpec((b, d), lambda l, j: (0, 0)),
            pl.BlockSpec((1, d, tn), lambda l, j: (l, 0, j)),
            pl.BlockSpec((1, 1, tn), lambda l, j: (l, 0, j)),
        ],
        out_specs=pl.BlockSpec((1, b, tn), lambda l, j: (l, 0, j)),
        out_shape=jax.ShapeDtypeStruct((n_layers, b, n), F32),
        compiler_params=_cparams("arbitrary", "arbitrary"),
        name="ada_mod",
    )(c, ada_w, ada_b.reshape(n_layers, 1, n))


def _ffn_kernel(x_ref, mod_ref, gpre_ref, gpost_ref, w1_ref, w3_ref, w2_ref, o_ref,
                h_scr, acc_scr, *, res_w, n_f):
    j = pl.program_id(1)

    @pl.when(j == 0)
    def _():
        h_scr[...] = _norm_mod(x_ref[...], gpre_ref[...], mod_ref).astype(BF16)
        acc_scr[...] = jnp.zeros_like(acc_scr)

    h = h_scr[...]
    g = _dot(h, w1_ref[...])
    u = _dot(h, w3_ref[...])
    a = (_silu(g) * u).astype(BF16)
    acc_scr[...] += _dot(a, w2_ref[...])

    @pl.when(j == n_f - 1)
    def _():
        o_ref[...] = _post_residual(x_ref[...], acc_scr[...], gpost_ref[...], mod_ref, res_w)


def _ffn(x2, mod_s, g_pre, g_post, w13, w2, seq, res_w):
    m, d = x2.shape
    f = w2.shape[0]
    tm = _pick(seq, 512)
    tf = _pick(f, 512)
    n_f = f // tf
    per_b = seq // tm
    return pl.pallas_call(
        functools.partial(_ffn_kernel, res_w=res_w, n_f=n_f),
        grid=(m // tm, n_f),
        in_specs=[
            pl.BlockSpec((tm, d), lambda i, j: (i, 0)),
            pl.BlockSpec((1, 3, d), lambda i, j: (i // per_b, 0, 0)),
            pl.BlockSpec((1, d), lambda i, j: (0, 0)),
            pl.BlockSpec((1, d), lambda i, j: (0, 0)),
            pl.BlockSpec((d, tf), lambda i, j: (0, j)),
            pl.BlockSpec((d, tf), lambda i, j: (0, j + n_f)),
            pl.BlockSpec((tf, d), lambda i, j: (j, 0)),
        ],
        out_specs=pl.BlockSpec((tm, d), lambda i, j: (i, 0)),
        out_shape=jax.ShapeDtypeStruct((m, d), F32),
        scratch_shapes=[pltpu.VMEM((tm, d), BF16), pltpu.VMEM((tm, d), F32)],
        compiler_params=_cparams("arbitrary", "arbitrary"),
        name="ffn",
    )(x2, mod_s, g_pre.reshape(1, d), g_post.reshape(1, d), w13, w13, w2)


def _inproj_kernel(x_ref, mod_ref, gpre_ref, w_ref, *rest, has_small):
    if has_small:
        ws_ref, o_ref, os_ref, h_scr = rest
    else:
        o_ref, h_scr = rest
    j = pl.program_id(1)

    @pl.when(j == 0)
    def _():
        h = _norm_mod(x_ref[...], gpre_ref[...], mod_ref).astype(BF16)
        h_scr[...] = h
        if has_small:
            os_ref[...] = _dot(h, ws_ref[...])

    o_ref[...] = _dot(h_scr[...], w_ref[...])


def _inproj(x2, mod_s, g_pre, w, seq, w_small=None):
    m, d = x2.shape
    n = w.shape[1]
    tm = _pick(seq, 1024)
    tn = _pick(n, 1024)
    per_b = seq // tm
    has_small = w_small is not None
    in_specs = [
        pl.BlockSpec((tm, d), lambda i, j: (i, 0)),
        pl.BlockSpec((1, 3, d), lambda i, j: (i // per_b, 0, 0)),
        pl.BlockSpec((1, d), lambda i, j: (0, 0)),
        pl.BlockSpec((d, tn), lambda i, j: (0, j)),
    ]
    out_specs = [pl.BlockSpec((tm, tn), lambda i, j: (i, j))]
    out_shape = [jax.ShapeDtypeStruct((m, n), F32)]
    args = [x2, mod_s, g_pre.reshape(1, d), w]
    if has_small:
        ns = w_small.shape[1]
        in_specs.append(pl.BlockSpec((d, ns), lambda i, j: (0, 0)))
        out_specs.append(pl.BlockSpec((tm, ns), lambda i, j: (i, 0)))
        out_shape.append(jax.ShapeDtypeStruct((m, ns), F32))
        args.append(w_small)
    out = pl.pallas_call(
        functools.partial(_inproj_kernel, has_small=has_small),
        grid=(m // tm, n // tn),
        in_specs=in_specs,
        out_specs=out_specs,
        out_shape=out_shape,
        scratch_shapes=[pltpu.VMEM((tm, d), BF16)],
        compiler_params=_cparams("arbitrary", "arbitrary"),
        name="mixer_inproj",
    )(*args)
    return out if has_small else out[0]


def _outproj_kernel(*refs, n_in, res_w):
    a_refs = refs[:n_in]
    w_refs = refs[n_in:2 * n_in]
    x_ref, mod_ref, gpost_ref, o_ref = refs[2 * n_in:]
    f = _dot(a_refs[0][...], w_refs[0][...])
    for a_ref, w_ref in zip(a_refs[1:], w_refs[1:]):
        f = f + _dot(a_ref[...], w_ref[...])
    o_ref[...] = _post_residual(x_ref[...], f, gpost_ref[...], mod_ref, res_w)


def _outproj(acts, w_out, x2, mod_s, g_post, seq, res_w):
    m, d = x2.shape
    tm = _pick(seq, 512)
    per_b = seq // tm
    n_in = len(acts)
    in_specs, args = [], []
    for a in acts:
        in_specs.append(pl.BlockSpec((tm, a.shape[1]), lambda i: (i, 0)))
        args.append(a)
    row = 0
    for a in acts:
        wi = a.shape[1]
        assert row % wi == 0
        in_specs.append(pl.BlockSpec((wi, d), lambda i, r=row // wi: (r, 0)))
        args.append(w_out)
        row += wi
    in_specs += [
        pl.BlockSpec((tm, d), lambda i: (i, 0)),
        pl.BlockSpec((1, 3, d), lambda i: (i // per_b, 0, 0)),
        pl.BlockSpec((1, d), lambda i: (0, 0)),
    ]
    args += [x2, mod_s, g_post.reshape(1, d)]
    return pl.pallas_call(
        functools.partial(_outproj_kernel, n_in=n_in, res_w=res_w),
        grid=(m // tm,),
        in_specs=in_specs,
        out_specs=pl.BlockSpec((tm, d), lambda i: (i, 0)),
        out_shape=jax.ShapeDtypeStruct((m, d), F32),
        compiler_params=_cparams("arbitrary"),
        name="mixer_outproj",
    )(*args)


def _conv_from_scratch(cs_ref, w_ref, rows, lo, hi):
    acc = cs_ref[SUBLANES:SUBLANES + rows, lo:hi] * w_ref[CONV_WIDTH - 1:CONV_WIDTH, lo:hi]
    for back in range(1, CONV_WIDTH):
        tap = CONV_WIDTH - 1 - back
        acc = acc + cs_ref[SUBLANES - back:SUBLANES - back + rows, lo:hi] * w_ref[tap:tap + 1, lo:hi]
    return acc


def _unit_lower_inverse(a_mat, row, col):
    eye = jnp.where(row == col, 1.0, 0.0)
    pair = (row - col == 1) & ((row & 1) == 1)
    x = eye - jnp.where(pair, a_mat, 0.0)
    s = 2
    while s < CHUNK:
        rb = row // s
        off = ((rb & 1) == 1) & ((col // s) == rb - 1)
        a_off = jnp.where(off, a_mat, 0.0)
        x = x - _dot_f32(_dot_f32(x, a_off), x)
        s *= 2
    return x


def _gdn_kernel(q_ref, k_ref, v_ref, z_ref, sm_ref, cw_ref, alog_ref, dtb_ref, onorm_ref,
                o_ref, s_scr, cs_scr, *, n_heads):
    t = pl.program_id(1)
    gw = n_heads * GDN_HEAD_DIM
    dk = GDN_HEAD_DIM

    @pl.when(t == 0)
    def _():
        s_scr[...] = jnp.zeros_like(s_scr)
        cs_scr[0:SUBLANES, :] = jnp.zeros((SUBLANES, 3 * gw), F32)

    cs_scr[SUBLANES:SUBLANES + CHUNK, 0:gw] = q_ref[0]
    cs_scr[SUBLANES:SUBLANES + CHUNK, gw:2 * gw] = k_ref[0]
    cs_scr[SUBLANES:SUBLANES + CHUNK, 2 * gw:3 * gw] = v_ref[0]

    sm = sm_ref[0]
    beta_all = _sigmoid(sm)
    ld_all = -jnp.exp(alog_ref[...]) * _softplus(sm + dtb_ref[...])
    rows8 = lax.broadcasted_iota(jnp.int32, (CHUNK, LANES), 0)
    g_all = ld_all
    d = 1
    while d < CHUNK:
        g_all = g_all + jnp.where(rows8 >= d, pltpu.roll(g_all, d, 0), 0.0)
        d *= 2
    eg_all = jnp.exp(g_all)
    g_t = jnp.concatenate([g_all, jnp.zeros((LANES - CHUNK, LANES), F32)], axis=0).T

    row = lax.broadcasted_iota(jnp.int32, (CHUNK, CHUNK), 0)
    col = lax.broadcasted_iota(jnp.int32, (CHUNK, CHUNK), 1)
    causal = row >= col
    strict = row > col

    for h in range(n_heads):
        lo, hi = h * dk, (h + 1) * dk
        q = _silu(_conv_from_scratch(cs_scr, cw_ref, CHUNK, lo, hi))
        k = _silu(_conv_from_scratch(cs_scr, cw_ref, CHUNK, gw + lo, gw + hi))
        v = _silu(_conv_from_scratch(cs_scr, cw_ref, CHUNK, 2 * gw + lo, 2 * gw + hi))
        qn = (q * lax.rsqrt(jnp.sum(q * q, axis=-1, keepdims=True) + EPS)) * dk ** -0.5
        kn = k * lax.rsqrt(jnp.sum(k * k, axis=-1, keepdims=True) + EPS)

        beta_c = beta_all[:, h:h + 1]
        g_c = g_all[:, n_heads + h:n_heads + h + 1]
        eg_c = eg_all[:, n_heads + h:n_heads + h + 1]
        g_r = g_t[n_heads + h:n_heads + h + 1, 0:CHUNK]
        g_last = g_all[CHUNK - 1:CHUNK, n_heads + h:n_heads + h + 1]

        decay = jnp.exp(jnp.where(causal, g_c - g_r, -jnp.inf))
        kb = kn * beta_c
        kn_b = kn.astype(BF16)
        a_mat = jnp.where(strict, _dot_nt(kb.astype(BF16), kn_b) * decay, 0.0)
        t_inv = _unit_lower_inverse(a_mat, row, col)
        rhs = jnp.concatenate([v * beta_c, kb * eg_c], axis=-1)
        sol = _dot_f32(t_inv, rhs)
        u, w = sol[:, :dk], sol[:, dk:]
        qk = jnp.where(causal, _dot_nt(qn.astype(BF16), kn_b) * decay, 0.0)
        q_st = qn * eg_c
        k_st = kn * jnp.exp(g_last - g_c)

        state = s_scr[h]
        state_b = state.astype(BF16)
        v_new = u - _dot(w.astype(BF16), state_b)
        v_new_b = v_new.astype(BF16)
        o = _dot(q_st.astype(BF16), state_b) + _dot(qk.astype(BF16), v_new_b)
        s_scr[h] = state * jnp.exp(g_last) + _dot_tn(k_st.astype(BF16), v_new_b)

        out = _rms(o, onorm_ref[...]) * _silu(z_ref[0, :, lo:hi])
        o_ref[0, :, lo:hi] = out.astype(o_ref.dtype)

    cs_scr[0:SUBLANES, :] = cs_scr[CHUNK:CHUNK + SUBLANES, :]


def _gdn(proj3, small3, conv_w, a_log, dt_bias, o_norm):
    b, seq, _ = proj3.shape
    n_heads = a_log.shape[0]
    gw = n_heads * GDN_HEAD_DIM
    pad = LANES - 2 * n_heads
    alog_row = jnp.pad(a_log, (n_heads, pad)).reshape(1, LANES)
    dtb_row = jnp.pad(dt_bias, (n_heads, pad)).reshape(1, LANES)
    col = lambda c: pl.BlockSpec((1, CHUNK, gw), lambda i, t, c=c: (i, t, c))
    return pl.pallas_call(
        functools.partial(_gdn_kernel, n_heads=n_heads),
        grid=(b, seq // CHUNK),
        in_specs=[
            col(0), col(1), col(2), col(3),
            pl.BlockSpec((1, CHUNK, LANES), lambda i, t: (i, t, 0)),
            pl.BlockSpec((CONV_WIDTH, 3 * gw), lambda i, t: (0, 0)),
            pl.BlockSpec((1, LANES), lambda i, t: (0, 0)),
            pl.BlockSpec((1, LANES), lambda i, t: (0, 0)),
            pl.BlockSpec((1, GDN_HEAD_DIM), lambda i, t: (0, 0)),
        ],
        out_specs=pl.BlockSpec((1, CHUNK, gw), lambda i, t: (i, t, 0)),
        out_shape=jax.ShapeDtypeStruct((b, seq, gw), BF16),
        scratch_shapes=[
            pltpu.VMEM((n_heads, GDN_HEAD_DIM, GDN_HEAD_DIM), F32),
            pltpu.VMEM((CHUNK + SUBLANES, 3 * gw), F32),
        ],
        compiler_params=_cparams("arbitrary", "arbitrary"),
        name="gdn",
    )(proj3, proj3, proj3, proj3, small3, conv_w, alog_row, dtb_row,
      o_norm.reshape(1, GDN_HEAD_DIM))


def _rotary(x, cos, sin):
    half = x.shape[-1] // 2
    x1, x2 = x[:, :half], x[:, half:]
    return jnp.concatenate([x1 * cos - x2 * sin, x2 * cos + x1 * sin], axis=-1)


def _ret_kernel(q_ref, k_ref, v_ref, g_ref, cos_ref, sin_ref, norm_ref, o_ref, s_scr,
                *, n_heads, rows):
    t = pl.program_id(1)
    dk = RET_HEAD_DIM

    @pl.when(t == 0)
    def _():
        s_scr[...] = jnp.zeros_like(s_scr)

    cos, sin = cos_ref[...], sin_ref[...]
    ri = lax.broadcasted_iota(jnp.int32, (rows, rows), 0)
    ci = lax.broadcasted_iota(jnp.int32, (rows, rows), 1)
    delta = (ri - ci).astype(F32)
    causal = ri >= ci
    pos = lax.broadcasted_iota(jnp.int32, (rows, 1), 0).astype(F32)

    for h in range(n_heads):
        lo, hi = h * dk, (h + 1) * dk
        log_gamma = math.log1p(-(2.0 ** (-5.0 - h)))
        q = _rotary(q_ref[0, :, lo:hi], cos, sin)
        k = _rotary(k_ref[0, :, lo:hi], cos, sin) * dk ** -0.5
        v_b = v_ref[0, :, lo:hi].astype(BF16)
        intra = jnp.exp(jnp.where(causal, delta * log_gamma, -jnp.inf))
        scores = _dot_nt(q.astype(BF16), k.astype(BF16)) * intra
        q_in = q * jnp.exp((pos + 1.0) * log_gamma)
        k_st = k * jnp.exp((rows - 1.0 - pos) * log_gamma)
        state = s_scr[h]
        o = _dot(scores.astype(BF16), v_b) + _dot(q_in.astype(BF16), state.astype(BF16))
        s_scr[h] = state * math.exp(rows * log_gamma) + _dot_tn(k_st.astype(BF16), v_b)
        out = _rms(o, norm_ref[:, lo:hi]) * _silu(g_ref[0, :, lo:hi])
        o_ref[0, :, lo:hi] = out.astype(o_ref.dtype)


def _retention(proj3, ret_norm, col0):
    b, seq, _ = proj3.shape
    rw = ret_norm.shape[0]
    n_heads = rw // RET_HEAD_DIM
    rows = _pick(seq, 256)
    half = RET_HEAD_DIM // 2
    inv_freq = ROPE_BASE ** (-jnp.arange(half, dtype=F32) / half)
    ang = jnp.arange(seq, dtype=F32)[:, None] * inv_freq[None, :]
    cos, sin = jnp.cos(ang), jnp.sin(ang)
    assert col0 % rw == 0
    c0 = col0 // rw
    col = lambda c: pl.BlockSpec((1, rows, rw), lambda i, t, c=c: (i, t, c0 + c))
    return pl.pallas_call(
        functools.partial(_ret_kernel, n_heads=n_heads, rows=rows),
        grid=(b, seq // rows),
        in_specs=[
            col(0), col(1), col(2), col(3),
            pl.BlockSpec((rows, half), lambda i, t: (t, 0)),
            pl.BlockSpec((rows, half), lambda i, t: (t, 0)),
            pl.BlockSpec((1, rw), lambda i, t: (0, 0)),
        ],
        out_specs=pl.BlockSpec((1, rows, rw), lambda i, t: (i, t, 0)),
        out_shape=jax.ShapeDtypeStruct((b, seq, rw), BF16),
        scratch_shapes=[pltpu.VMEM((n_heads, RET_HEAD_DIM, RET_HEAD_DIM), F32)],
        compiler_params=_cparams("arbitrary", "arbitrary"),
        name="retention",
    )(proj3, proj3, proj3, proj3, cos, sin, ret_norm.reshape(1, rw))


def _lru_kernel(y_ref, x_ref, cw_ref, cb_ref, wa_ref, ba_ref, wx_ref, bx_ref, lam_ref,
                o_ref, cs_scr, h_scr, *, n_blocks, rows, bw):
    t = pl.program_id(1)
    width = n_blocks * bw

    @pl.when(t == 0)
    def _():
        cs_scr[0:SUBLANES, :] = jnp.zeros((SUBLANES, width), F32)
        h_scr[...] = jnp.zeros_like(h_scr)

    cs_scr[SUBLANES:SUBLANES + rows, :] = x_ref[0]
    row = lax.broadcasted_iota(jnp.int32, (rows, bw), 0)

    for n in range(n_blocks):
        lo, hi = n * bw, (n + 1) * bw
        xc = _conv_from_scratch(cs_scr, cw_ref, rows, lo, hi) + cb_ref[:, lo:hi]
        xc_b = xc.astype(BF16)
        r = _sigmoid(_dot(xc_b, wa_ref[n]) + ba_ref[:, lo:hi])
        i = _sigmoid(_dot(xc_b, wx_ref[n]) + bx_ref[:, lo:hi])
        log_a = (-LRU_C * r) * _softplus(-lam_ref[:, lo:hi])
        a = jnp.exp(log_a)
        mult = jnp.sqrt(-jnp.tanh(log_a) * (a * a + 1.0))
        bv = mult * (i * xc)
        d = 1
        while d < rows:
            keep = row >= d
            a_sh = jnp.where(keep, pltpu.roll(a, d, 0), 1.0)
            b_sh = jnp.where(keep, pltpu.roll(bv, d, 0), 0.0)
            bv = a * b_sh + bv
            a = a * a_sh
            d *= 2
        hs = bv + a * h_scr[0:1, lo:hi]
        h_scr[0:1, lo:hi] = hs[rows - 1:rows, :]
        o_ref[0, :, lo:hi] = (hs * _gelu_tanh(y_ref[0, :, lo:hi])).astype(o_ref.dtype)

    cs_scr[0:SUBLANES, :] = cs_scr[rows:rows + SUBLANES, :]


def _lru(proj3, conv_w, conv_b, gate_a_w, gate_a_b, gate_x_w, gate_x_b, lam):
    b, seq, two_w = proj3.shape
    width = two_w // 2
    n_blocks, bw, _ = gate_a_w.shape
    rows = _pick(seq, 256)
    vec = lambda: pl.BlockSpec((1, width), lambda i, t: (0, 0))
    gate = lambda: pl.BlockSpec((n_blocks, bw, bw), lambda i, t: (0, 0, 0))
    return pl.pallas_call(
        functools.partial(_lru_kernel, n_blocks=n_blocks, rows=rows, bw=bw),
        grid=(b, seq // rows),
        in_specs=[
            pl.BlockSpec((1, rows, width), lambda i, t: (i, t, 0)),
            pl.BlockSpec((1, rows, width), lambda i, t: (i, t, 1)),
            pl.BlockSpec((CONV_WIDTH, width), lambda i, t: (0, 0)),
            vec(), gate(), vec(), gate(), vec(), vec(),
        ],
        out_specs=pl.BlockSpec((1, rows, width), lambda i, t: (i, t, 0)),
        out_shape=jax.ShapeDtypeStruct((b, seq, width), BF16),
        scratch_shapes=[
            pltpu.VMEM((rows + SUBLANES, width), F32),
            pltpu.VMEM((SUBLANES, width), F32),
        ],
        compiler_params=_cparams("arbitrary", "arbitrary"),
        name="rglru",
    )(proj3, proj3, conv_w, conv_b.reshape(1, width), gate_a_w.astype(BF16),
      gate_a_b.reshape(1, width), gate_x_w.astype(BF16), gate_x_b.reshape(1, width),
      lam.reshape(1, width))


def kernel(x, c, ada_w, ada_b, norm_pre, norm_post, ffn_w13, ffn_w2, ev_w_in, ev_conv_w, ev_a_log,
           ev_dt_bias, ev_o_norm, ev_ret_norm, ev_w_out, od_w_in, od_conv_w, od_conv_b,
           od_gate_a_w, od_gate_a_b, od_gate_x_w, od_gate_x_b, od_lambda, od_w_out):
    b, seq, d = x.shape
    depth = ada_w.shape[0]
    m = b * seq
    mod = _ada(c, ada_w, ada_b).reshape(depth, b, N_SUB, 3, d)
    x2 = x.reshape(m, d)

    for layer in range(depth):
        mod_l = mod[layer]
        x2 = _ffn(x2, mod_l[:, 0], norm_pre[layer, 0], norm_post[layer, 0],
                  ffn_w13[layer, 0].astype(BF16), ffn_w2[layer, 0].astype(BF16), seq, 0.5)
        if layer % 2 == 0:
            e = layer // 2
            n_heads = ev_a_log.shape[1]
            gw = n_heads * GDN_HEAD_DIM
            w_in = ev_w_in[e]
            w_main = jnp.concatenate([w_in[:, :3 * gw], w_in[:, 3 * gw + 2 * n_heads:]], axis=1)
            w_small = jnp.pad(w_in[:, 3 * gw:3 * gw + 2 * n_heads],
                              ((0, 0), (0, LANES - 2 * n_heads)))
            proj, small = _inproj(x2, mod_l[:, 1], norm_pre[layer, 1], w_main.astype(BF16), seq,
                                  w_small.astype(BF16))
            proj3 = proj.reshape(b, seq, -1)
            o_a = _gdn(proj3, small.reshape(b, seq, LANES), ev_conv_w[e], ev_a_log[e],
                       ev_dt_bias[e], ev_o_norm[e])
            o_b = _retention(proj3, ev_ret_norm[e], 4 * gw)
            acts = [o_a.reshape(m, -1), o_b.reshape(m, -1)]
            w_out = ev_w_out[e]
        else:
            o = layer // 2
            proj = _inproj(x2, mod_l[:, 1], norm_pre[layer, 1], od_w_in[o].astype(BF16), seq)
            hs = _lru(proj.reshape(b, seq, -1), od_conv_w[o], od_conv_b[o], od_gate_a_w[o],
                      od_gate_a_b[o], od_gate_x_w[o], od_gate_x_b[o], od_lambda[o])
            acts = [hs.reshape(m, -1)]
            w_out = od_w_out[o]
        x2 = _outproj(acts, w_out.astype(BF16), x2, mod_l[:, 1], norm_post[layer, 1], seq, 1.0)
        x2 = _ffn(x2, mod_l[:, 2], norm_pre[layer, 2], norm_post[layer, 2],
                  ffn_w13[layer, 1].astype(BF16), ffn_w2[layer, 1].astype(BF16), seq, 0.5)
    return x2.reshape(b, seq, d)
```

```python
import functools
import math

import jax
import jax.numpy as jnp
from jax import lax
from jax.experimental import pallas as pl
from jax.experimental.pallas import tpu as pltpu

F32 = jnp.float32
BF16 = jnp.bfloat16

EPS = 1e-6
GDN_HEAD_DIM = 128
RET_HEAD_DIM = 256
CHUNK = 64
CONV_WIDTH = 4
ROPE_BASE = 10000.0
LRU_C = 8.0
N_SUB = 3

SUBLANES = 8
LANES = 128
V7X_VMEM_LIMIT_BYTES = 56 * 1024 * 1024


def _cparams(*semantics):
    return pltpu.CompilerParams(dimension_semantics=semantics,
                                vmem_limit_bytes=V7X_VMEM_LIMIT_BYTES)


def _dot(a, b):
    return jnp.dot(a, b, preferred_element_type=F32)


def _dot_nt(a, b):
    return lax.dot_general(a, b, (((1,), (1,)), ((), ())), preferred_element_type=F32)


def _dot_tn(a, b):
    return lax.dot_general(a, b, (((0,), (0,)), ((), ())), preferred_element_type=F32)


def _sigmoid(x):
    return jax.nn.sigmoid(x)


def _silu(x):
    return x * jax.nn.sigmoid(x)


def _softplus(x):
    return jnp.maximum(x, 0.0) + jnp.log1p(jnp.exp(-jnp.abs(x)))


def _gelu_tanh(x):
    c = math.sqrt(2.0 / math.pi)
    return 0.5 * x * (1.0 + jnp.tanh(c * (x + 0.044715 * (x * x * x))))


def _rms(x, gain):
    return x * lax.rsqrt(jnp.mean(x * x, axis=-1, keepdims=True) + EPS) * gain


def _norm_mod(x, g_pre, mod_ref):
    return _rms(x, g_pre) * (1.0 + mod_ref[0, 1:2, :]) + mod_ref[0, 0:1, :]


def _post_residual(x, f, g_post, mod_ref, res_w):
    return x + (res_w * (1.0 + mod_ref[0, 2:3, :])) * _rms(f, g_post)


def _pick(n, pref):
    if n <= pref:
        return n
    for t in range(pref - pref % LANES, 0, -LANES):
        if n % t == 0:
            return t
    raise ValueError(f"no lane-aligned tile of {n} at or below {pref}")


def _ada_kernel(c_ref, w_ref, b_ref, o_ref):
    a = _silu(c_ref[...]).astype(BF16)
    o_ref[0] = _dot(a, w_ref[0].astype(BF16)) + b_ref[0]


def _ada(c, ada_w, ada_b):
    n_layers, d, n = ada_w.shape
    b = c.shape[0]
    tn = _pick(n, 1024)
    return pl.pallas_call(
        _ada_kernel,
        grid=(n_layers, n // tn),
        in_specs=[
            pl.BlockSpec((b, d), lambda l, j: (0, 0)),
            pl.BlockSpec((1, d, tn), lambda l, j: (l, 0, j)),
            pl.BlockSpec((1, 1, tn), lambda l, j: (l, 0, j)),
        ],
        out_specs=pl.BlockSpec((1, b, tn), lambda l, j: (l, 0, j)),
        out_shape=jax.ShapeDtypeStruct((n_layers, b, n), F32),
        compiler_params=_cparams("arbitrary", "arbitrary"),
        name="ada_mod",
    )(c, ada_w, ada_b.reshape(n_layers, 1, n))


def _ffn_kernel(x_ref, mod_ref, gpre_ref, gpost_ref, w1_ref, w3_ref, w2_ref, o_ref,
                h_scr, acc_scr, *, res_w, n_f):
    j = pl.program_id(1)

    @pl.when(j == 0)
    def _():
        h_scr[...] = _norm_mod(x_ref[...], gpre_ref[...], mod_ref).astype(BF16)
        acc_scr[...] = jnp.zeros_like(acc_scr)

    h = h_scr[...]
    g = _dot(h, w1_ref[...])
    u = _dot(h, w3_ref[...])
    a = (_silu(g) * u).astype(BF16)
    acc_scr[...] += _dot(a, w2_ref[...])

    @pl.when(j == n_f - 1)
    def _():
        o_ref[...] = _post_residual(x_ref[...], acc_scr[...], gpost_ref[...], mod_ref, res_w)


def _ffn(x2, mod_s, g_pre, g_post, w13, w2, seq, res_w):
    m, d = x2.shape
    f = w2.shape[0]
    tm = _pick(seq, 512)
    tf = _pick(f, 512)
    n_f = f // tf
    per_b = seq // tm
    return pl.pallas_call(
        functools.partial(_ffn_kernel, res_w=res_w, n_f=n_f),
        grid=(m // tm, n_f),
        in_specs=[
            pl.BlockSpec((tm, d), lambda i, j: (i, 0)),
            pl.BlockSpec((1, 3, d), lambda i, j: (i // per_b, 0, 0)),
            pl.BlockSpec((1, d), lambda i, j: (0, 0)),
            pl.BlockSpec((1, d), lambda i, j: (0, 0)),
            pl.BlockSpec((d, tf), lambda i, j: (0, j)),
            pl.BlockSpec((d, tf), lambda i, j: (0, j + n_f)),
            pl.BlockSpec((tf, d), lambda i, j: (j, 0)),
        ],
        out_specs=pl.BlockSpec((tm, d), lambda i, j: (i, 0)),
        out_shape=jax.ShapeDtypeStruct((m, d), F32),
        scratch_shapes=[pltpu.VMEM((tm, d), BF16), pltpu.VMEM((tm, d), F32)],
        compiler_params=_cparams("arbitrary", "arbitrary"),
        name="ffn",
    )(x2, mod_s, g_pre.reshape(1, d), g_post.reshape(1, d), w13, w13, w2)


def _inproj_kernel(x_ref, mod_ref, gpre_ref, w_ref, *rest, has_small):
    if has_small:
        ws_ref, o_ref, os_ref, h_scr = rest
    else:
        o_ref, h_scr = rest
    j = pl.program_id(1)

    @pl.when(j == 0)
    def _():
        h = _norm_mod(x_ref[...], gpre_ref[...], mod_ref).astype(BF16)
        h_scr[...] = h
        if has_small:
            os_ref[...] = _dot(h, ws_ref[...])

    o_ref[...] = _dot(h_scr[...], w_ref[...])


def _inproj(x2, mod_s, g_pre, w, seq, w_small=None):
    m, d = x2.shape
    n = w.shape[1]
    tm = _pick(seq, 1024)
    tn = _pick(n, 1024)
    per_b = seq // tm
    has_small = w_small is not None
    in_specs = [
        pl.BlockSpec((tm, d), lambda i, j: (i, 0)),
        pl.BlockSpec((1, 3, d), lambda i, j: (i // per_b, 0, 0)),
        pl.BlockSpec((1, d), lambda i, j: (0, 0)),
        pl.BlockSpec((d, tn), lambda i, j: (0, j)),
    ]
    out_specs = [pl.BlockSpec((tm, tn), lambda i, j: (i, j))]
    out_shape = [jax.ShapeDtypeStruct((m, n), F32)]
    args = [x2, mod_s, g_pre.reshape(1, d), w]
    if has_small:
        ns = w_small.shape[1]
        in_specs.append(pl.BlockSpec((d, ns), lambda i, j: (0, 0)))
        out_specs.append(pl.BlockSpec((tm, ns), lambda i, j: (i, 0)))
        out_shape.append(jax.ShapeDtypeStruct((m, ns), F32))
        args.append(w_small)
    out = pl.pallas_call(
        functools.partial(_inproj_kernel, has_small=has_small),
        grid=(m // tm, n // tn),
        in_specs=in_specs,
        out_specs=out_specs,
        out_shape=out_shape,
        scratch_shapes=[pltpu.VMEM((tm, d), BF16)],
        compiler_params=_cparams("arbitrary", "arbitrary"),
        name="mixer_inproj",
    )(*args)
    return out if has_small else out[0]


def _outproj_kernel(*refs, n_in, res_w):
    a_refs = refs[:n_in]
    w_refs = refs[n_in:2 * n_in]
    x_ref, mod_ref, gpost_ref, o_ref = refs[2 * n_in:]
    f = _dot(a_refs[0][...], w_refs[0][...])
    for a_ref, w_ref in zip(a_refs[1:], w_refs[1:]):
        f = f + _dot(a_ref[...], w_ref[...])
    o_ref[...] = _post_residual(x_ref[...], f, gpost_ref[...], mod_ref, res_w)


def _outproj(acts, w_out, x2, mod_s, g_post, seq, res_w):
    m, d = x2.shape
    tm = _pick(seq, 512)
    per_b = seq // tm
    n_in = len(acts)
    in_specs, args = [], []
    for a in acts:
        in_specs.append(pl.BlockSpec((tm, a.shape[1]), lambda i: (i, 0)))
        args.append(a)
    row = 0
    for a in acts:
        wi = a.shape[1]
        assert row % wi == 0
        in_specs.append(pl.BlockSpec((wi, d), lambda i, r=row // wi: (r, 0)))
        args.append(w_out)
        row += wi
    in_specs += [
        pl.BlockSpec((tm, d), lambda i: (i, 0)),
        pl.BlockSpec((1, 3, d), lambda i: (i // per_b, 0, 0)),
        pl.BlockSpec((1, d), lambda i: (0, 0)),
    ]
    args += [x2, mod_s, g_post.reshape(1, d)]
    return pl.pallas_call(
        functools.partial(_outproj_kernel, n_in=n_in, res_w=res_w),
        grid=(m // tm,),
        in_specs=in_specs,
        out_specs=pl.BlockSpec((tm, d), lambda i: (i, 0)),
        out_shape=jax.ShapeDtypeStruct((m, d), F32),
        compiler_params=_cparams("arbitrary"),
        name="mixer_outproj",
    )(*args)


def _conv_from_scratch(cs_ref, w_ref, rows, lo, hi):
    acc = cs_ref[SUBLANES:SUBLANES + rows, lo:hi] * w_ref[CONV_WIDTH - 1:CONV_WIDTH, lo:hi]
    for back in range(1, CONV_WIDTH):
        tap = CONV_WIDTH - 1 - back
        acc = acc + cs_ref[SUBLANES - back:SUBLANES - back + rows, lo:hi] * w_ref[tap:tap + 1, lo:hi]
    return acc


GDN_BLOCK = 256
GDN_PACK = 4


def _tile_rows(x, n):
    return jnp.concatenate([x] * n, axis=0)


def _gdn_kernel(q_ref, k_ref, v_ref, z_ref, sm_ref, cw_ref, alog_ref, dtb_ref, onorm_ref,
                o_ref, s_scr, cs_scr, qn_scr, kn_scr, kb_scr, vb_scr, kbe_scr, kst_scr, egq_scr,
                a_scr, qk_scr, xo_scr, l_scr, ku_scr, oin_scr, *, n_heads, rows):
    t = pl.program_id(1)
    dk = GDN_HEAD_DIM
    gw = n_heads * dk
    n_chunks = rows // CHUNK
    n_groups = n_heads // GDN_PACK
    pw = GDN_PACK * CHUNK
    gk = GDN_PACK * dk

    @pl.when(t == 0)
    def _():
        s_scr[...] = jnp.zeros_like(s_scr)
        cs_scr[0:SUBLANES, :] = jnp.zeros((SUBLANES, 3 * gw), F32)

    cs_scr[SUBLANES:SUBLANES + rows, 0:gw] = q_ref[0]
    cs_scr[SUBLANES:SUBLANES + rows, gw:2 * gw] = k_ref[0]
    cs_scr[SUBLANES:SUBLANES + rows, 2 * gw:3 * gw] = v_ref[0]

    sm = sm_ref[0]
    beta_all = _sigmoid(sm)
    g_all = -jnp.exp(alog_ref[...]) * _softplus(sm + dtb_ref[...])
    in_chunk = lax.broadcasted_iota(jnp.int32, (rows, LANES), 0) & (CHUNK - 1)
    d = 1
    while d < CHUNK:
        g_all = g_all + jnp.where(in_chunk >= d, pltpu.roll(g_all, d, 0), 0.0)
        d *= 2
    eg_all = jnp.exp(g_all)
    glast_all = jnp.concatenate(
        [jnp.broadcast_to(g_all[(c + 1) * CHUNK - 1:(c + 1) * CHUNK, :], (CHUNK, LANES))
         for c in range(n_chunks)], axis=0)
    ekl_all = jnp.exp(glast_all - g_all)

    for h in range(n_heads):
        lo, hi = h * dk, (h + 1) * dk
        bcast = lambda arr, idx: jnp.broadcast_to(arr[:, idx:idx + 1], (rows, dk))
        beta_b = bcast(beta_all, h)
        eg_b = bcast(eg_all, n_heads + h)
        q = _silu(_conv_from_scratch(cs_scr, cw_ref, rows, lo, hi))
        qn = (q * lax.rsqrt(jnp.sum(q * q, axis=-1, keepdims=True) + EPS)) * dk ** -0.5
        qn_scr[:, lo:hi] = qn.astype(BF16)
        egq_scr[:, lo:hi] = qn * eg_b
        k = _silu(_conv_from_scratch(cs_scr, cw_ref, rows, gw + lo, gw + hi))
        kn = k * lax.rsqrt(jnp.sum(k * k, axis=-1, keepdims=True) + EPS)
        kb = kn * beta_b
        kn_scr[:, lo:hi] = kn.astype(BF16)
        kb_scr[:, lo:hi] = kb.astype(BF16)
        kbe_scr[:, lo:hi] = kb * eg_b
        kst_scr[:, lo:hi] = (kn * bcast(ekl_all, n_heads + h)).astype(BF16)
        v = _silu(_conv_from_scratch(cs_scr, cw_ref, rows, 2 * gw + lo, 2 * gw + hi))
        vb_scr[:, lo:hi] = v * beta_b
    cs_scr[0:SUBLANES, :] = cs_scr[rows:rows + SUBLANES, :]

    ri = lax.broadcasted_iota(jnp.int32, (CHUNK, pw), 0)
    li = lax.broadcasted_iota(jnp.int32, (CHUNK, pw), 1)
    cj = li & (CHUNK - 1)
    lh = li // CHUNK
    causal = ri >= cj
    strict = ri > cj
    eye = ri == cj
    bd_r = lax.broadcasted_iota(jnp.int32, (pw, pw), 0) // CHUNK
    bd_c = lax.broadcasted_iota(jnp.int32, (pw, pw), 1) // CHUNK
    bd_sq = bd_r == bd_c
    bk_r = lax.broadcasted_iota(jnp.int32, (pw, gk), 0) // CHUNK
    bk_c = lax.broadcasted_iota(jnp.int32, (pw, gk), 1) // dk
    bd_k = bk_r == bk_c
    zero_sq = jnp.zeros((pw, pw), BF16)
    zero_k = jnp.zeros((pw, gk), BF16)

    def block_diag(x_rp):
        return jnp.where(bd_sq, _tile_rows(x_rp.astype(BF16), GDN_PACK), zero_sq)

    def block_diag_k(x_cat):
        return jnp.where(bd_k, _tile_rows(x_cat, GDN_PACK), zero_k)

    probs = [(c, gi) for c in range(n_chunks) for gi in range(n_groups)]
    for p, (c, gi) in enumerate(probs):
        r0, r1 = c * CHUNK, (c + 1) * CHUNK
        c0 = gi * gk
        gcs = [jnp.broadcast_to(g_all[r0:r1, n_heads + gi * GDN_PACK + s:n_heads + gi * GDN_PACK + s + 1],
                                (CHUNK, pw)) for s in range(GDN_PACK)]
        gc = gcs[GDN_PACK - 1]
        for s in range(GDN_PACK - 2, -1, -1):
            gc = jnp.where(lh == s, gcs[s], gc)
        gr = jnp.sum(jnp.where(eye, gc, 0.0), axis=0, keepdims=True)
        decay = jnp.exp(jnp.where(causal, gc - gr, -jnp.inf))
        lhs = jnp.concatenate([kb_scr[r0:r1, c0:c0 + gk], qn_scr[r0:r1, c0:c0 + gk]], axis=0)
        kq = _dot_nt(lhs, block_diag_k(kn_scr[r0:r1, c0:c0 + gk]))
        a = jnp.where(strict, kq[0:CHUNK] * decay, 0.0)
        a_scr[p] = a
        qk_scr[p] = jnp.where(causal, kq[CHUNK:2 * CHUNK] * decay, 0.0)
        xo_scr[p] = -jnp.where((ri - cj == 1) & ((ri & 1) == 1), a, 0.0)

    s = 2
    while s < CHUNK:
        rb = ri // s
        off = ((rb & 1) == 1) & ((cj // s) == rb - 1)
        for p in range(len(probs)):
            a_off = jnp.where(off, a_scr[p], 0.0)
            xo = xo_scr[p]
            y = a_off + _dot(xo.astype(BF16), block_diag(a_off))
            xo_scr[p] = xo - (y + _dot(y.astype(BF16), block_diag(xo)))
        s *= 2

    for p, (c, gi) in enumerate(probs):
        r0, r1 = c * CHUNK, (c + 1) * CHUNK
        rhs = jnp.concatenate(
            [jnp.concatenate([vb_scr[r0:r1, h * dk:(h + 1) * dk], kbe_scr[r0:r1, h * dk:(h + 1) * dk]], axis=1)
             for h in range(gi * GDN_PACK, (gi + 1) * GDN_PACK)], axis=0)
        sol = rhs + _dot(block_diag(xo_scr[p]), rhs.astype(BF16))
        sol_b = sol.astype(BF16)
        qks = _dot(block_diag(qk_scr[p]), sol_b)
        ks = _dot_tn(block_diag_k(kst_scr[r0:r1, gi * gk:(gi + 1) * gk]), sol_b)
        for s_ in range(GDN_PACK):
            h = gi * GDN_PACK + s_
            half = (h % 2) * dk
            rr = slice(s_ * CHUNK, (s_ + 1) * CHUNK)
            kr = slice(s_ * dk, (s_ + 1) * dk)
            q_eff = egq_scr[r0:r1, h * dk:(h + 1) * dk] - qks[rr, dk:2 * dk]
            l_scr[c, h // 2, 0:dk, half:half + dk] = ks[kr, dk:2 * dk].astype(BF16)
            l_scr[c, h // 2, dk:dk + CHUNK, half:half + dk] = q_eff.astype(BF16)
            ku_scr[c, h] = ks[kr, 0:dk]
            oin_scr[r0:r1, h * dk:(h + 1) * dk] = qks[rr, 0:dk]

    zero = jnp.zeros((dk, dk), BF16)
    for c in range(n_chunks):
        r0, r1 = c * CHUNK, (c + 1) * CHUNK
        for pr in range(n_heads // 2):
            h0, h1 = 2 * pr, 2 * pr + 1
            s0 = s_scr[h0]
            s1 = s_scr[h1]
            s_bd = jnp.concatenate(
                [jnp.concatenate([s0.astype(BF16), zero], axis=1),
                 jnp.concatenate([zero, s1.astype(BF16)], axis=1)], axis=0)
            r = _dot(l_scr[c, pr], s_bd)
            for h, st, off_ in ((h0, s0, 0), (h1, s1, dk)):
                cd = eg_all[r1 - 1:r1, n_heads + h:n_heads + h + 1]
                s_scr[h] = st * cd - r[0:dk, off_:off_ + dk] + ku_scr[c, h]
                o = r[dk:dk + CHUNK, off_:off_ + dk] + oin_scr[r0:r1, h * dk:(h + 1) * dk]
                out = _rms(o, onorm_ref[...]) * _silu(z_ref[0, r0:r1, h * dk:(h + 1) * dk])
                o_ref[0, r0:r1, h * dk:(h + 1) * dk] = out.astype(o_ref.dtype)


def _gdn(proj3, small3, conv_w, a_log, dt_bias, o_norm):
    b, seq, _ = proj3.shape
    n_heads = a_log.shape[0]
    assert n_heads % GDN_PACK == 0 and GDN_HEAD_DIM == LANES
    gw = n_heads * GDN_HEAD_DIM
    rows = _pick(seq, GDN_BLOCK)
    n_chunks = rows // CHUNK
    n_prob = n_chunks * (n_heads // GDN_PACK)
    pad = LANES - 2 * n_heads
    alog_row = jnp.pad(a_log, (n_heads, pad)).reshape(1, LANES)
    dtb_row = jnp.pad(dt_bias, (n_heads, pad)).reshape(1, LANES)
    col = lambda c: pl.BlockSpec((1, rows, gw), lambda i, t, c=c: (i, t, c))
    return pl.pallas_call(
        functools.partial(_gdn_kernel, n_heads=n_heads, rows=rows),
        grid=(b, seq // rows),
        in_specs=[
            col(0), col(1), col(2), col(3),
            pl.BlockSpec((1, rows, LANES), lambda i, t: (i, t, 0)),
            pl.BlockSpec((CONV_WIDTH, 3 * gw), lambda i, t: (0, 0)),
            pl.BlockSpec((1, LANES), lambda i, t: (0, 0)),
            pl.BlockSpec((1, LANES), lambda i, t: (0, 0)),
            pl.BlockSpec((1, GDN_HEAD_DIM), lambda i, t: (0, 0)),
        ],
        out_specs=pl.BlockSpec((1, rows, gw), lambda i, t: (i, t, 0)),
        out_shape=jax.ShapeDtypeStruct((b, seq, gw), BF16),
        scratch_shapes=[
            pltpu.VMEM((n_heads, GDN_HEAD_DIM, GDN_HEAD_DIM), F32),
            pltpu.VMEM((rows + SUBLANES, 3 * gw), F32),
            pltpu.VMEM((rows, gw), BF16),
            pltpu.VMEM((rows, gw), BF16),
            pltpu.VMEM((rows, gw), BF16),
            pltpu.VMEM((rows, gw), F32),
            pltpu.VMEM((rows, gw), F32),
            pltpu.VMEM((rows, gw), BF16),
            pltpu.VMEM((rows, gw), F32),
            pltpu.VMEM((n_prob, CHUNK, GDN_PACK * CHUNK), F32),
            pltpu.VMEM((n_prob, CHUNK, GDN_PACK * CHUNK), F32),
            pltpu.VMEM((n_prob, CHUNK, GDN_PACK * CHUNK), F32),
            pltpu.VMEM((n_chunks, n_heads // 2, GDN_HEAD_DIM + CHUNK, 2 * GDN_HEAD_DIM), BF16),
            pltpu.VMEM((n_chunks, n_heads, GDN_HEAD_DIM, GDN_HEAD_DIM), F32),
            pltpu.VMEM((rows, gw), F32),
        ],
        compiler_params=_cparams("arbitrary", "arbitrary"),
        name="gdn",
    )(proj3, proj3, proj3, proj3, small3, conv_w, alog_row, dtb_row,
      o_norm.reshape(1, GDN_HEAD_DIM))


def _rotary(x, cos, sin):
    half = x.shape[-1] // 2
    x1, x2 = x[:, :half], x[:, half:]
    return jnp.concatenate([x1 * cos - x2 * sin, x2 * cos + x1 * sin], axis=-1)


def _ret_kernel(q_ref, k_ref, v_ref, g_ref, cos_ref, sin_ref, norm_ref, o_ref, s_scr,
                *, n_heads, rows):
    t = pl.program_id(1)
    dk = RET_HEAD_DIM

    @pl.when(t == 0)
    def _():
        s_scr[...] = jnp.zeros_like(s_scr)

    cos, sin = cos_ref[...], sin_ref[...]
    ri = lax.broadcasted_iota(jnp.int32, (rows, rows), 0)
    ci = lax.broadcasted_iota(jnp.int32, (rows, rows), 1)
    delta = (ri - ci).astype(F32)
    causal = ri >= ci
    pos = lax.broadcasted_iota(jnp.int32, (rows, 1), 0).astype(F32)

    for h in range(n_heads):
        lo, hi = h * dk, (h + 1) * dk
        log_gamma = math.log1p(-(2.0 ** (-5.0 - h)))
        q = _rotary(q_ref[0, :, lo:hi], cos, sin)
        k = _rotary(k_ref[0, :, lo:hi], cos, sin) * dk ** -0.5
        v_b = v_ref[0, :, lo:hi].astype(BF16)
        intra = jnp.exp(jnp.where(causal, delta * log_gamma, -jnp.inf))
        scores = _dot_nt(q.astype(BF16), k.astype(BF16)) * intra
        q_in = q * jnp.exp((pos + 1.0) * log_gamma)
        k_st = k * jnp.exp((rows - 1.0 - pos) * log_gamma)
        state = s_scr[h]
        o = _dot(scores.astype(BF16), v_b) + _dot(q_in.astype(BF16), state.astype(BF16))
        s_scr[h] = state * math.exp(rows * log_gamma) + _dot_tn(k_st.astype(BF16), v_b)
        out = _rms(o, norm_ref[:, lo:hi]) * _silu(g_ref[0, :, lo:hi])
        o_ref[0, :, lo:hi] = out.astype(o_ref.dtype)


def _retention(proj3, ret_norm, col0):
    b, seq, _ = proj3.shape
    rw = ret_norm.shape[0]
    n_heads = rw // RET_HEAD_DIM
    rows = _pick(seq, 256)
    half = RET_HEAD_DIM // 2
    inv_freq = ROPE_BASE ** (-jnp.arange(half, dtype=F32) / half)
    ang = jnp.arange(seq, dtype=F32)[:, None] * inv_freq[None, :]
    cos, sin = jnp.cos(ang), jnp.sin(ang)
    assert col0 % rw == 0
    c0 = col0 // rw
    col = lambda c: pl.BlockSpec((1, rows, rw), lambda i, t, c=c: (i, t, c0 + c))
    return pl.pallas_call(
        functools.partial(_ret_kernel, n_heads=n_heads, rows=rows),
        grid=(b, seq // rows),
        in_specs=[
            col(0), col(1), col(2), col(3),
            pl.BlockSpec((rows, half), lambda i, t: (t, 0)),
            pl.BlockSpec((rows, half), lambda i, t: (t, 0)),
            pl.BlockSpec((1, rw), lambda i, t: (0, 0)),
        ],
        out_specs=pl.BlockSpec((1, rows, rw), lambda i, t: (i, t, 0)),
        out_shape=jax.ShapeDtypeStruct((b, seq, rw), BF16),
        scratch_shapes=[pltpu.VMEM((n_heads, RET_HEAD_DIM, RET_HEAD_DIM), F32)],
        compiler_params=_cparams("arbitrary", "arbitrary"),
        name="retention",
    )(proj3, proj3, proj3, proj3, cos, sin, ret_norm.reshape(1, rw))


def _lru_kernel(y_ref, x_ref, cw_ref, cb_ref, wa_ref, ba_ref, wx_ref, bx_ref, lam_ref,
                o_ref, cs_scr, h_scr, *, n_blocks, rows, bw):
    t = pl.program_id(1)
    width = n_blocks * bw

    @pl.when(t == 0)
    def _():
        cs_scr[0:SUBLANES, :] = jnp.zeros((SUBLANES, width), F32)
        h_scr[...] = jnp.zeros_like(h_scr)

    cs_scr[SUBLANES:SUBLANES + rows, :] = x_ref[0]
    row = lax.broadcasted_iota(jnp.int32, (rows, bw), 0)

    for n in range(n_blocks):
        lo, hi = n * bw, (n + 1) * bw
        xc = _conv_from_scratch(cs_scr, cw_ref, rows, lo, hi) + cb_ref[:, lo:hi]
        xc_b = xc.astype(BF16)
        r = _sigmoid(_dot(xc_b, wa_ref[n]) + ba_ref[:, lo:hi])
        i = _sigmoid(_dot(xc_b, wx_ref[n]) + bx_ref[:, lo:hi])
        log_a = (-LRU_C * r) * _softplus(-lam_ref[:, lo:hi])
        a = jnp.exp(log_a)
        mult = jnp.sqrt(-jnp.tanh(log_a) * (a * a + 1.0))
        bv = mult * (i * xc)
        d = 1
        while d < rows:
            keep = row >= d
            a_sh = jnp.where(keep, pltpu.roll(a, d, 0), 1.0)
            b_sh = jnp.where(keep, pltpu.roll(bv, d, 0), 0.0)
            bv = a * b_sh + bv
            a = a * a_sh
            d *= 2
        hs = bv + a * h_scr[0:1, lo:hi]
        h_scr[0:1, lo:hi] = hs[rows - 1:rows, :]
        o_ref[0, :, lo:hi] = (hs * _gelu_tanh(y_ref[0, :, lo:hi])).astype(o_ref.dtype)

    cs_scr[0:SUBLANES, :] = cs_scr[rows:rows + SUBLANES, :]


def _lru(proj3, conv_w, conv_b, gate_a_w, gate_a_b, gate_x_w, gate_x_b, lam):
    b, seq, two_w = proj3.shape
    width = two_w // 2
    n_blocks, bw, _ = gate_a_w.shape
    rows = _pick(seq, 256)
    vec = lambda: pl.BlockSpec((1, width), lambda i, t: (0, 0))
    gate = lambda: pl.BlockSpec((n_blocks, bw, bw), lambda i, t: (0, 0, 0))
    return pl.pallas_call(
        functools.partial(_lru_kernel, n_blocks=n_blocks, rows=rows, bw=bw),
        grid=(b, seq // rows),
        in_specs=[
            pl.BlockSpec((1, rows, width), lambda i, t: (i, t, 0)),
            pl.BlockSpec((1, rows, width), lambda i, t: (i, t, 1)),
            pl.BlockSpec((CONV_WIDTH, width), lambda i, t: (0, 0)),
            vec(), gate(), vec(), gate(), vec(), vec(),
        ],
        out_specs=pl.BlockSpec((1, rows, width), lambda i, t: (i, t, 0)),
        out_shape=jax.ShapeDtypeStruct((b, seq, width), BF16),
        scratch_shapes=[
            pltpu.VMEM((rows + SUBLANES, width), F32),
            pltpu.VMEM((SUBLANES, width), F32),
        ],
        compiler_params=_cparams("arbitrary", "arbitrary"),
        name="rglru",
    )(proj3, proj3, conv_w, conv_b.reshape(1, width), gate_a_w.astype(BF16),
      gate_a_b.reshape(1, width), gate_x_w.astype(BF16), gate_x_b.reshape(1, width),
      lam.reshape(1, width))


def kernel(x, c, ada_w, ada_b, norm_pre, norm_post, ffn_w13, ffn_w2, ev_w_in, ev_conv_w, ev_a_log,
           ev_dt_bias, ev_o_norm, ev_ret_norm, ev_w_out, od_w_in, od_conv_w, od_conv_b,
           od_gate_a_w, od_gate_a_b, od_gate_x_w, od_gate_x_b, od_lambda, od_w_out):
    b, seq, d = x.shape
    depth = ada_w.shape[0]
    m = b * seq
    mod = _ada(c, ada_w, ada_b).reshape(depth, b, N_SUB, 3, d)
    x2 = x.reshape(m, d)

    for layer in range(depth):
        mod_l = mod[layer]
        x2 = _ffn(x2, mod_l[:, 0], norm_pre[layer, 0], norm_post[layer, 0],
                  ffn_w13[layer, 0].astype(BF16), ffn_w2[layer, 0].astype(BF16), seq, 0.5)
        if layer % 2 == 0:
            e = layer // 2
            n_heads = ev_a_log.shape[1]
            gw = n_heads * GDN_HEAD_DIM
            w_in = ev_w_in[e]
            w_main = jnp.concatenate([w_in[:, :3 * gw], w_in[:, 3 * gw + 2 * n_heads:]], axis=1)
            w_small = jnp.pad(w_in[:, 3 * gw:3 * gw + 2 * n_heads],
                              ((0, 0), (0, LANES - 2 * n_heads)))
            proj, small = _inproj(x2, mod_l[:, 1], norm_pre[layer, 1], w_main.astype(BF16), seq,
                                  w_small.astype(BF16))
            proj3 = proj.reshape(b, seq, -1)
            o_a = _gdn(proj3, small.reshape(b, seq, LANES), ev_conv_w[e], ev_a_log[e],
                       ev_dt_bias[e], ev_o_norm[e])
            o_b = _retention(proj3, ev_ret_norm[e], 4 * gw)
            acts = [o_a.reshape(m, -1), o_b.reshape(m, -1)]
            w_out = ev_w_out[e]
        else:
            o = layer // 2
            proj = _inproj(x2, mod_l[:, 1], norm_pre[layer, 1], od_w_in[o].astype(BF16), seq)
            hs = _lru(proj.reshape(b, seq, -1), od_conv_w[o], od_conv_b[o], od_gate_a_w[o],
                      od_gate_a_b[o], od_gate_x_w[o], od_gate_x_b[o], od_lambda[o])
            acts = [hs.reshape(m, -1)]
            w_out = od_w_out[o]
        x2 = _outproj(acts, w_out.astype(BF16), x2, mod_l[:, 1], norm_post[layer, 1], seq, 1.0)
        x2 = _ffn(x2, mod_l[:, 2], norm_pre[layer, 2], norm_post[layer, 2],
                  ffn_w13[layer, 1].astype(BF16), ffn_w2[layer, 1].astype(BF16), seq, 0.5)
    return x2.reshape(b, seq, d)
```

```python
import functools
import math

import jax
import jax.numpy as jnp
from jax import lax
from jax.experimental import pallas as pl
from jax.experimental.pallas import tpu as pltpu

F32 = jnp.float32
BF16 = jnp.bfloat16

EPS = 1e-6
GDN_HEAD_DIM = 128
RET_HEAD_DIM = 256
CHUNK = 64
CONV_WIDTH = 4
ROPE_BASE = 10000.0
LRU_C = 8.0
N_SUB = 3

SUBLANES = 8
LANES = 128
V7X_VMEM_LIMIT_BYTES = 56 * 1024 * 1024


def _cparams(*semantics):
    return pltpu.CompilerParams(dimension_semantics=semantics,
                                vmem_limit_bytes=V7X_VMEM_LIMIT_BYTES)


def _dot(a, b):
    return jnp.dot(a, b, preferred_element_type=F32)


def _dot_nt(a, b):
    return lax.dot_general(a, b, (((1,), (1,)), ((), ())), preferred_element_type=F32)


def _dot_tn(a, b):
    return lax.dot_general(a, b, (((0,), (0,)), ((), ())), preferred_element_type=F32)


def _sigmoid(x):
    return jax.nn.sigmoid(x)


def _silu(x):
    return x * jax.nn.sigmoid(x)


def _softplus(x):
    return jnp.maximum(x, 0.0) + jnp.log1p(jnp.exp(-jnp.abs(x)))


def _gelu_tanh(x):
    c = math.sqrt(2.0 / math.pi)
    return 0.5 * x * (1.0 + jnp.tanh(c * (x + 0.044715 * (x * x * x))))


def _rms(x, gain):
    return x * lax.rsqrt(jnp.mean(x * x, axis=-1, keepdims=True) + EPS) * gain


def _norm_mod(x, g_pre, mod_ref):
    return _rms(x, g_pre) * (1.0 + mod_ref[0, 1:2, :]) + mod_ref[0, 0:1, :]


def _post_residual(x, f, g_post, mod_ref, res_w):
    return x + (res_w * (1.0 + mod_ref[0, 2:3, :])) * _rms(f, g_post)


def _pick(n, pref):
    if n <= pref:
        return n
    for t in range(pref - pref % LANES, 0, -LANES):
        if n % t == 0:
            return t
    raise ValueError(f"no lane-aligned tile of {n} at or below {pref}")


def _ada_kernel(c_ref, w_ref, b_ref, o_ref):
    a = _silu(c_ref[...]).astype(BF16)
    o_ref[0] = _dot(a, w_ref[0].astype(BF16)) + b_ref[0]


def _ada(c, ada_w, ada_b):
    n_layers, d, n = ada_w.shape
    b = c.shape[0]
    tn = _pick(n, 1024)
    return pl.pallas_call(
        _ada_kernel,
        grid=(n_layers, n // tn),
        in_specs=[
            pl.BlockSpec((b, d), lambda l, j: (0, 0)),
            pl.BlockSpec((1, d, tn), lambda l, j: (l, 0, j)),
            pl.BlockSpec((1, 1, tn), lambda l, j: (l, 0, j)),
        ],
        out_specs=pl.BlockSpec((1, b, tn), lambda l, j: (l, 0, j)),
        out_shape=jax.ShapeDtypeStruct((n_layers, b, n), F32),
        compiler_params=_cparams("arbitrary", "arbitrary"),
        name="ada_mod",
    )(c, ada_w, ada_b.reshape(n_layers, 1, n))


def _ffn_kernel(x_ref, mod_ref, gpre_ref, gpost_ref, w1_ref, w3_ref, w2_ref, o_ref,
                h_scr, acc_scr, *, res_w, n_f):
    j = pl.program_id(1)

    @pl.when(j == 0)
    def _():
        h_scr[...] = _norm_mod(x_ref[...], gpre_ref[...], mod_ref).astype(BF16)
        acc_scr[...] = jnp.zeros_like(acc_scr)

    h = h_scr[...]
    g = _dot(h, w1_ref[...])
    u = _dot(h, w3_ref[...])
    a = (_silu(g) * u).astype(BF16)
    acc_scr[...] += _dot(a, w2_ref[...])

    @pl.when(j == n_f - 1)
    def _():
        o_ref[...] = _post_residual(x_ref[...], acc_scr[...], gpost_ref[...], mod_ref, res_w)


def _ffn(x2, mod_s, g_pre, g_post, w13_all, w2_all, layer, which, seq, res_w):
    m, d = x2.shape
    f = w2_all.shape[2]
    tm = _pick(seq, 512)
    tf = _pick(f, 512)
    n_f = f // tf
    per_b = seq // tm
    return pl.pallas_call(
        functools.partial(_ffn_kernel, res_w=res_w, n_f=n_f),
        grid=(m // tm, n_f),
        in_specs=[
            pl.BlockSpec((tm, d), lambda i, j: (i, 0)),
            pl.BlockSpec((1, 3, d), lambda i, j: (i // per_b, 0, 0)),
            pl.BlockSpec((1, d), lambda i, j: (0, 0)),
            pl.BlockSpec((1, d), lambda i, j: (0, 0)),
            pl.BlockSpec((None, None, d, tf), lambda i, j: (layer, which, 0, j)),
            pl.BlockSpec((None, None, d, tf), lambda i, j: (layer, which, 0, j + n_f)),
            pl.BlockSpec((None, None, tf, d), lambda i, j: (layer, which, j, 0)),
        ],
        out_specs=pl.BlockSpec((tm, d), lambda i, j: (i, 0)),
        out_shape=jax.ShapeDtypeStruct((m, d), F32),
        scratch_shapes=[pltpu.VMEM((tm, d), BF16), pltpu.VMEM((tm, d), F32)],
        compiler_params=_cparams("arbitrary", "arbitrary"),
        name="ffn",
    )(x2, mod_s, g_pre.reshape(1, d), g_post.reshape(1, d), w13_all, w13_all, w2_all)


def _inproj_kernel(x_ref, mod_ref, gpre_ref, w_ref, *rest, has_small):
    if has_small:
        ws_ref, o_ref, os_ref, h_scr = rest
    else:
        o_ref, h_scr = rest
    j = pl.program_id(1)

    @pl.when(j == 0)
    def _():
        h = _norm_mod(x_ref[...], gpre_ref[...], mod_ref).astype(BF16)
        h_scr[...] = h
        if has_small:
            os_ref[...] = _dot(h, ws_ref[...])

    o_ref[...] = _dot(h_scr[...], w_ref[...])


def _inproj(x2, mod_s, g_pre, w, seq, w_small=None):
    m, d = x2.shape
    n = w.shape[1]
    tm = _pick(seq, 1024)
    tn = _pick(n, 1024)
    per_b = seq // tm
    has_small = w_small is not None
    in_specs = [
        pl.BlockSpec((tm, d), lambda i, j: (i, 0)),
        pl.BlockSpec((1, 3, d), lambda i, j: (i // per_b, 0, 0)),
        pl.BlockSpec((1, d), lambda i, j: (0, 0)),
        pl.BlockSpec((d, tn), lambda i, j: (0, j)),
    ]
    out_specs = [pl.BlockSpec((tm, tn), lambda i, j: (i, j))]
    out_shape = [jax.ShapeDtypeStruct((m, n), F32)]
    args = [x2, mod_s, g_pre.reshape(1, d), w]
    if has_small:
        ns = w_small.shape[1]
        in_specs.append(pl.BlockSpec((d, ns), lambda i, j: (0, 0)))
        out_specs.append(pl.BlockSpec((tm, ns), lambda i, j: (i, 0)))
        out_shape.append(jax.ShapeDtypeStruct((m, ns), F32))
        args.append(w_small)
    out = pl.pallas_call(
        functools.partial(_inproj_kernel, has_small=has_small),
        grid=(m // tm, n // tn),
        in_specs=in_specs,
        out_specs=out_specs,
        out_shape=out_shape,
        scratch_shapes=[pltpu.VMEM((tm, d), BF16)],
        compiler_params=_cparams("arbitrary", "arbitrary"),
        name="mixer_inproj",
    )(*args)
    return out if has_small else out[0]


def _outproj_kernel(*refs, n_in, res_w):
    a_refs = refs[:n_in]
    w_refs = refs[n_in:2 * n_in]
    x_ref, mod_ref, gpost_ref, o_ref = refs[2 * n_in:]
    f = _dot(a_refs[0][...], w_refs[0][...])
    for a_ref, w_ref in zip(a_refs[1:], w_refs[1:]):
        f = f + _dot(a_ref[...], w_ref[...])
    o_ref[...] = _post_residual(x_ref[...], f, gpost_ref[...], mod_ref, res_w)


def _outproj(acts, w_out, x2, mod_s, g_post, seq, res_w):
    m, d = x2.shape
    tm = _pick(seq, 512)
    per_b = seq // tm
    n_in = len(acts)
    in_specs, args = [], []
    for a in acts:
        in_specs.append(pl.BlockSpec((tm, a.shape[1]), lambda i: (i, 0)))
        args.append(a)
    row = 0
    for a in acts:
        wi = a.shape[1]
        assert row % wi == 0
        in_specs.append(pl.BlockSpec((wi, d), lambda i, r=row // wi: (r, 0)))
        args.append(w_out)
        row += wi
    in_specs += [
        pl.BlockSpec((tm, d), lambda i: (i, 0)),
        pl.BlockSpec((1, 3, d), lambda i: (i // per_b, 0, 0)),
        pl.BlockSpec((1, d), lambda i: (0, 0)),
    ]
    args += [x2, mod_s, g_post.reshape(1, d)]
    return pl.pallas_call(
        functools.partial(_outproj_kernel, n_in=n_in, res_w=res_w),
        grid=(m // tm,),
        in_specs=in_specs,
        out_specs=pl.BlockSpec((tm, d), lambda i: (i, 0)),
        out_shape=jax.ShapeDtypeStruct((m, d), F32),
        compiler_params=_cparams("arbitrary"),
        name="mixer_outproj",
    )(*args)


def _conv_from_scratch(cs_ref, w_ref, rows, lo, hi):
    acc = cs_ref[SUBLANES:SUBLANES + rows, lo:hi] * w_ref[CONV_WIDTH - 1:CONV_WIDTH, lo:hi]
    for back in range(1, CONV_WIDTH):
        tap = CONV_WIDTH - 1 - back
        acc = acc + cs_ref[SUBLANES - back:SUBLANES - back + rows, lo:hi] * w_ref[tap:tap + 1, lo:hi]
    return acc


GDN_BLOCK = 256
GDN_PACK = 4


def _tile_rows(x, n):
    return jnp.concatenate([x] * n, axis=0)


def _gdn_kernel(q_ref, k_ref, v_ref, z_ref, sm_ref, cw_ref, alog_ref, dtb_ref, onorm_ref,
                o_ref, s_scr, cs_scr, qn_scr, kn_scr, kb_scr, vb_scr, kbe_scr, kst_scr, egq_scr,
                a_scr, qk_scr, xo_scr, l_scr, ku_scr, oin_scr, *, n_heads, rows):
    t = pl.program_id(1)
    dk = GDN_HEAD_DIM
    gw = n_heads * dk
    n_chunks = rows // CHUNK
    n_groups = n_heads // GDN_PACK
    pw = GDN_PACK * CHUNK
    gk = GDN_PACK * dk

    @pl.when(t == 0)
    def _():
        s_scr[...] = jnp.zeros_like(s_scr)
        cs_scr[0:SUBLANES, :] = jnp.zeros((SUBLANES, 3 * gw), F32)

    cs_scr[SUBLANES:SUBLANES + rows, 0:gw] = q_ref[0]
    cs_scr[SUBLANES:SUBLANES + rows, gw:2 * gw] = k_ref[0]
    cs_scr[SUBLANES:SUBLANES + rows, 2 * gw:3 * gw] = v_ref[0]

    sm = sm_ref[0]
    beta_all = _sigmoid(sm)
    g_all = -jnp.exp(alog_ref[...]) * _softplus(sm + dtb_ref[...])
    in_chunk = lax.broadcasted_iota(jnp.int32, (rows, LANES), 0) & (CHUNK - 1)
    d = 1
    while d < CHUNK:
        g_all = g_all + jnp.where(in_chunk >= d, pltpu.roll(g_all, d, 0), 0.0)
        d *= 2
    eg_all = jnp.exp(g_all)
    glast_all = jnp.concatenate(
        [jnp.broadcast_to(g_all[(c + 1) * CHUNK - 1:(c + 1) * CHUNK, :], (CHUNK, LANES))
         for c in range(n_chunks)], axis=0)
    ekl_all = jnp.exp(glast_all - g_all)

    for h in range(n_heads):
        lo, hi = h * dk, (h + 1) * dk
        bcast = lambda arr, idx: jnp.broadcast_to(arr[:, idx:idx + 1], (rows, dk))
        beta_b = bcast(beta_all, h)
        eg_b = bcast(eg_all, n_heads + h)
        q = _silu(_conv_from_scratch(cs_scr, cw_ref, rows, lo, hi))
        qn = (q * lax.rsqrt(jnp.sum(q * q, axis=-1, keepdims=True) + EPS)) * dk ** -0.5
        qn_scr[:, lo:hi] = qn.astype(BF16)
        egq_scr[:, lo:hi] = qn * eg_b
        k = _silu(_conv_from_scratch(cs_scr, cw_ref, rows, gw + lo, gw + hi))
        kn = k * lax.rsqrt(jnp.sum(k * k, axis=-1, keepdims=True) + EPS)
        kb = kn * beta_b
        kn_scr[:, lo:hi] = kn.astype(BF16)
        kb_scr[:, lo:hi] = kb.astype(BF16)
        kbe_scr[:, lo:hi] = kb * eg_b
        kst_scr[:, lo:hi] = (kn * bcast(ekl_all, n_heads + h)).astype(BF16)
        v = _silu(_conv_from_scratch(cs_scr, cw_ref, rows, 2 * gw + lo, 2 * gw + hi))
        vb_scr[:, lo:hi] = v * beta_b
    cs_scr[0:SUBLANES, :] = cs_scr[rows:rows + SUBLANES, :]

    ri = lax.broadcasted_iota(jnp.int32, (CHUNK, pw), 0)
    li = lax.broadcasted_iota(jnp.int32, (CHUNK, pw), 1)
    cj = li & (CHUNK - 1)
    lh = li // CHUNK
    causal = ri >= cj
    strict = ri > cj
    eye = ri == cj
    bd_r = lax.broadcasted_iota(jnp.int32, (pw, pw), 0) // CHUNK
    bd_c = lax.broadcasted_iota(jnp.int32, (pw, pw), 1) // CHUNK
    bd_sq = bd_r == bd_c
    bk_r = lax.broadcasted_iota(jnp.int32, (pw, gk), 0) // CHUNK
    bk_c = lax.broadcasted_iota(jnp.int32, (pw, gk), 1) // dk
    bd_k = bk_r == bk_c
    zero_sq = jnp.zeros((pw, pw), BF16)
    zero_k = jnp.zeros((pw, gk), BF16)

    def block_diag(x_rp):
        return jnp.where(bd_sq, _tile_rows(x_rp.astype(BF16), GDN_PACK), zero_sq)

    def block_diag_k(x_cat):
        return jnp.where(bd_k, _tile_rows(x_cat, GDN_PACK), zero_k)

    probs = [(c, gi) for c in range(n_chunks) for gi in range(n_groups)]
    for p, (c, gi) in enumerate(probs):
        r0, r1 = c * CHUNK, (c + 1) * CHUNK
        c0 = gi * gk
        gcs = [jnp.broadcast_to(g_all[r0:r1, n_heads + gi * GDN_PACK + s:n_heads + gi * GDN_PACK + s + 1],
                                (CHUNK, pw)) for s in range(GDN_PACK)]
        gc = gcs[GDN_PACK - 1]
        for s in range(GDN_PACK - 2, -1, -1):
            gc = jnp.where(lh == s, gcs[s], gc)
        gr = jnp.sum(jnp.where(eye, gc, 0.0), axis=0, keepdims=True)
        decay = jnp.exp(jnp.where(causal, gc - gr, -jnp.inf))
        lhs = jnp.concatenate([kb_scr[r0:r1, c0:c0 + gk], qn_scr[r0:r1, c0:c0 + gk]], axis=0)
        kq = _dot_nt(lhs, block_diag_k(kn_scr[r0:r1, c0:c0 + gk]))
        a = jnp.where(strict, kq[0:CHUNK] * decay, 0.0)
        a_scr[p] = a
        qk_scr[p] = jnp.where(causal, kq[CHUNK:2 * CHUNK] * decay, 0.0)
        xo_scr[p] = -jnp.where((ri - cj == 1) & ((ri & 1) == 1), a, 0.0)

    s = 2
    while s < CHUNK:
        rb = ri // s
        off = ((rb & 1) == 1) & ((cj // s) == rb - 1)
        for p in range(len(probs)):
            a_off = jnp.where(off, a_scr[p], 0.0)
            xo = xo_scr[p]
            y = a_off + _dot(xo.astype(BF16), block_diag(a_off))
            xo_scr[p] = xo - (y + _dot(y.astype(BF16), block_diag(xo)))
        s *= 2

    for p, (c, gi) in enumerate(probs):
        r0, r1 = c * CHUNK, (c + 1) * CHUNK
        rhs = jnp.concatenate(
            [jnp.concatenate([vb_scr[r0:r1, h * dk:(h + 1) * dk], kbe_scr[r0:r1, h * dk:(h + 1) * dk]], axis=1)
             for h in range(gi * GDN_PACK, (gi + 1) * GDN_PACK)], axis=0)
        sol = rhs + _dot(block_diag(xo_scr[p]), rhs.astype(BF16))
        sol_b = sol.astype(BF16)
        qks = _dot(block_diag(qk_scr[p]), sol_b)
        ks = _dot_tn(block_diag_k(kst_scr[r0:r1, gi * gk:(gi + 1) * gk]), sol_b)
        for s_ in range(GDN_PACK):
            h = gi * GDN_PACK + s_
            half = (h % 2) * dk
            rr = slice(s_ * CHUNK, (s_ + 1) * CHUNK)
            kr = slice(s_ * dk, (s_ + 1) * dk)
            q_eff = egq_scr[r0:r1, h * dk:(h + 1) * dk] - qks[rr, dk:2 * dk]
            l_scr[c, h // 2, 0:dk, half:half + dk] = ks[kr, dk:2 * dk].astype(BF16)
            l_scr[c, h // 2, dk:dk + CHUNK, half:half + dk] = q_eff.astype(BF16)
            ku_scr[c, h] = ks[kr, 0:dk]
            oin_scr[r0:r1, h * dk:(h + 1) * dk] = qks[rr, 0:dk]

    zero = jnp.zeros((dk, dk), BF16)
    for c in range(n_chunks):
        r0, r1 = c * CHUNK, (c + 1) * CHUNK
        for pr in range(n_heads // 2):
            h0, h1 = 2 * pr, 2 * pr + 1
            s0 = s_scr[h0]
            s1 = s_scr[h1]
            s_bd = jnp.concatenate(
                [jnp.concatenate([s0.astype(BF16), zero], axis=1),
                 jnp.concatenate([zero, s1.astype(BF16)], axis=1)], axis=0)
            r = _dot(l_scr[c, pr], s_bd)
            for h, st, off_ in ((h0, s0, 0), (h1, s1, dk)):
                cd = eg_all[r1 - 1:r1, n_heads + h:n_heads + h + 1]
                s_scr[h] = st * cd - r[0:dk, off_:off_ + dk] + ku_scr[c, h]
                o = r[dk:dk + CHUNK, off_:off_ + dk] + oin_scr[r0:r1, h * dk:(h + 1) * dk]
                out = _rms(o, onorm_ref[...]) * _silu(z_ref[0, r0:r1, h * dk:(h + 1) * dk])
                o_ref[0, r0:r1, h * dk:(h + 1) * dk] = out.astype(o_ref.dtype)


def _gdn(proj3, small3, conv_w, a_log, dt_bias, o_norm):
    b, seq, _ = proj3.shape
    n_heads = a_log.shape[0]
    assert n_heads % GDN_PACK == 0 and GDN_HEAD_DIM == LANES
    gw = n_heads * GDN_HEAD_DIM
    rows = _pick(seq, GDN_BLOCK)
    n_chunks = rows // CHUNK
    n_prob = n_chunks * (n_heads // GDN_PACK)
    pad = LANES - 2 * n_heads
    alog_row = jnp.pad(a_log, (n_heads, pad)).reshape(1, LANES)
    dtb_row = jnp.pad(dt_bias, (n_heads, pad)).reshape(1, LANES)
    col = lambda c: pl.BlockSpec((1, rows, gw), lambda i, t, c=c: (i, t, c))
    return pl.pallas_call(
        functools.partial(_gdn_kernel, n_heads=n_heads, rows=rows),
        grid=(b, seq // rows),
        in_specs=[
            col(0), col(1), col(2), col(3),
            pl.BlockSpec((1, rows, LANES), lambda i, t: (i, t, 0)),
            pl.BlockSpec((CONV_WIDTH, 3 * gw), lambda i, t: (0, 0)),
            pl.BlockSpec((1, LANES), lambda i, t: (0, 0)),
            pl.BlockSpec((1, LANES), lambda i, t: (0, 0)),
            pl.BlockSpec((1, GDN_HEAD_DIM), lambda i, t: (0, 0)),
        ],
        out_specs=pl.BlockSpec((1, rows, gw), lambda i, t: (i, t, 0)),
        out_shape=jax.ShapeDtypeStruct((b, seq, gw), BF16),
        scratch_shapes=[
            pltpu.VMEM((n_heads, GDN_HEAD_DIM, GDN_HEAD_DIM), F32),
            pltpu.VMEM((rows + SUBLANES, 3 * gw), F32),
            pltpu.VMEM((rows, gw), BF16),
            pltpu.VMEM((rows, gw), BF16),
            pltpu.VMEM((rows, gw), BF16),
            pltpu.VMEM((rows, gw), F32),
            pltpu.VMEM((rows, gw), F32),
            pltpu.VMEM((rows, gw), BF16),
            pltpu.VMEM((rows, gw), F32),
            pltpu.VMEM((n_prob, CHUNK, GDN_PACK * CHUNK), F32),
            pltpu.VMEM((n_prob, CHUNK, GDN_PACK * CHUNK), F32),
            pltpu.VMEM((n_prob, CHUNK, GDN_PACK * CHUNK), F32),
            pltpu.VMEM((n_chunks, n_heads // 2, GDN_HEAD_DIM + CHUNK, 2 * GDN_HEAD_DIM), BF16),
            pltpu.VMEM((n_chunks, n_heads, GDN_HEAD_DIM, GDN_HEAD_DIM), F32),
            pltpu.VMEM((rows, gw), F32),
        ],
        compiler_params=_cparams("arbitrary", "arbitrary"),
        name="gdn",
    )(proj3, proj3, proj3, proj3, small3, conv_w, alog_row, dtb_row,
      o_norm.reshape(1, GDN_HEAD_DIM))


def _rotary(x, cos, sin):
    half = x.shape[-1] // 2
    x1, x2 = x[:, :half], x[:, half:]
    return jnp.concatenate([x1 * cos - x2 * sin, x2 * cos + x1 * sin], axis=-1)


def _ret_kernel(q_ref, k_ref, v_ref, g_ref, cos_ref, sin_ref, norm_ref, o_ref, s_scr,
                *, n_heads, rows):
    t = pl.program_id(1)
    dk = RET_HEAD_DIM

    @pl.when(t == 0)
    def _():
        s_scr[...] = jnp.zeros_like(s_scr)

    cos, sin = cos_ref[...], sin_ref[...]
    ri = lax.broadcasted_iota(jnp.int32, (rows, rows), 0)
    ci = lax.broadcasted_iota(jnp.int32, (rows, rows), 1)
    delta = (ri - ci).astype(F32)
    causal = ri >= ci
    pos = lax.broadcasted_iota(jnp.int32, (rows, 1), 0).astype(F32)

    for h in range(n_heads):
        lo, hi = h * dk, (h + 1) * dk
        log_gamma = math.log1p(-(2.0 ** (-5.0 - h)))
        q = _rotary(q_ref[0, :, lo:hi], cos, sin)
        k = _rotary(k_ref[0, :, lo:hi], cos, sin) * dk ** -0.5
        v_b = v_ref[0, :, lo:hi].astype(BF16)
        intra = jnp.exp(jnp.where(causal, delta * log_gamma, -jnp.inf))
        scores = _dot_nt(q.astype(BF16), k.astype(BF16)) * intra
        q_in = q * jnp.exp((pos + 1.0) * log_gamma)
        k_st = k * jnp.exp((rows - 1.0 - pos) * log_gamma)
        state = s_scr[h]
        o = _dot(scores.astype(BF16), v_b) + _dot(q_in.astype(BF16), state.astype(BF16))
        s_scr[h] = state * math.exp(rows * log_gamma) + _dot_tn(k_st.astype(BF16), v_b)
        out = _rms(o, norm_ref[:, lo:hi]) * _silu(g_ref[0, :, lo:hi])
        o_ref[0, :, lo:hi] = out.astype(o_ref.dtype)


def _retention(proj3, ret_norm, col0):
    b, seq, _ = proj3.shape
    rw = ret_norm.shape[0]
    n_heads = rw // RET_HEAD_DIM
    rows = _pick(seq, 256)
    half = RET_HEAD_DIM // 2
    inv_freq = ROPE_BASE ** (-jnp.arange(half, dtype=F32) / half)
    ang = jnp.arange(seq, dtype=F32)[:, None] * inv_freq[None, :]
    cos, sin = jnp.cos(ang), jnp.sin(ang)
    assert col0 % rw == 0
    c0 = col0 // rw
    col = lambda c: pl.BlockSpec((1, rows, rw), lambda i, t, c=c: (i, t, c0 + c))
    return pl.pallas_call(
        functools.partial(_ret_kernel, n_heads=n_heads, rows=rows),
        grid=(b, seq // rows),
        in_specs=[
            col(0), col(1), col(2), col(3),
            pl.BlockSpec((rows, half), lambda i, t: (t, 0)),
            pl.BlockSpec((rows, half), lambda i, t: (t, 0)),
            pl.BlockSpec((1, rw), lambda i, t: (0, 0)),
        ],
        out_specs=pl.BlockSpec((1, rows, rw), lambda i, t: (i, t, 0)),
        out_shape=jax.ShapeDtypeStruct((b, seq, rw), BF16),
        scratch_shapes=[pltpu.VMEM((n_heads, RET_HEAD_DIM, RET_HEAD_DIM), F32)],
        compiler_params=_cparams("arbitrary", "arbitrary"),
        name="retention",
    )(proj3, proj3, proj3, proj3, cos, sin, ret_norm.reshape(1, rw))


def _lru_kernel(y_ref, x_ref, cw_ref, cb_ref, wa_ref, ba_ref, wx_ref, bx_ref, lam_ref,
                o_ref, cs_scr, h_scr, *, n_blocks, rows, bw):
    t = pl.program_id(1)
    width = n_blocks * bw

    @pl.when(t == 0)
    def _():
        cs_scr[0:SUBLANES, :] = jnp.zeros((SUBLANES, width), F32)
        h_scr[...] = jnp.zeros_like(h_scr)

    cs_scr[SUBLANES:SUBLANES + rows, :] = x_ref[0]
    groups = rows // SUBLANES
    sub = lax.broadcasted_iota(jnp.int32, (groups, SUBLANES, bw), 1)

    for n in range(n_blocks):
        lo, hi = n * bw, (n + 1) * bw
        xc = _conv_from_scratch(cs_scr, cw_ref, rows, lo, hi) + cb_ref[:, lo:hi]
        xc_b = xc.astype(BF16)
        r = _sigmoid(_dot(xc_b, wa_ref[n]) + ba_ref[:, lo:hi])
        i = _sigmoid(_dot(xc_b, wx_ref[n]) + bx_ref[:, lo:hi])
        log_a = (-LRU_C * r) * _softplus(-lam_ref[:, lo:hi])
        a = jnp.exp(log_a)
        mult = jnp.sqrt(-jnp.tanh(log_a) * (a * a + 1.0))
        bv = mult * (i * xc)
        a3 = a.reshape(groups, SUBLANES, bw)
        b3 = bv.reshape(groups, SUBLANES, bw)
        d = 1
        while d < SUBLANES:
            keep = sub >= d
            a_sh = jnp.where(keep, pltpu.roll(a3, d, 1), 1.0)
            b_sh = jnp.where(keep, pltpu.roll(b3, d, 1), 0.0)
            b3 = a3 * b_sh + b3
            a3 = a3 * a_sh
            d *= 2
        carry = h_scr[0:1, lo:hi]
        outs = []
        for gi in range(groups):
            hg = b3[gi] + a3[gi] * carry
            outs.append(hg)
            carry = hg[SUBLANES - 1:SUBLANES, :]
        h_scr[0:1, lo:hi] = carry
        hs = jnp.concatenate(outs, axis=0)
        o_ref[0, :, lo:hi] = (hs * _gelu_tanh(y_ref[0, :, lo:hi])).astype(o_ref.dtype)

    cs_scr[0:SUBLANES, :] = cs_scr[rows:rows + SUBLANES, :]


def _lru(proj3, conv_w, conv_b, gate_a_w, gate_a_b, gate_x_w, gate_x_b, lam):
    b, seq, two_w = proj3.shape
    width = two_w // 2
    n_blocks, bw, _ = gate_a_w.shape
    rows = _pick(seq, 256)
    vec = lambda: pl.BlockSpec((1, width), lambda i, t: (0, 0))
    gate = lambda: pl.BlockSpec((n_blocks, bw, bw), lambda i, t: (0, 0, 0))
    return pl.pallas_call(
        functools.partial(_lru_kernel, n_blocks=n_blocks, rows=rows, bw=bw),
        grid=(b, seq // rows),
        in_specs=[
            pl.BlockSpec((1, rows, width), lambda i, t: (i, t, 0)),
            pl.BlockSpec((1, rows, width), lambda i, t: (i, t, 1)),
            pl.BlockSpec((CONV_WIDTH, width), lambda i, t: (0, 0)),
            vec(), gate(), vec(), gate(), vec(), vec(),
        ],
        out_specs=pl.BlockSpec((1, rows, width), lambda i, t: (i, t, 0)),
        out_shape=jax.ShapeDtypeStruct((b, seq, width), BF16),
        scratch_shapes=[
            pltpu.VMEM((rows + SUBLANES, width), F32),
            pltpu.VMEM((SUBLANES, width), F32),
        ],
        compiler_params=_cparams("arbitrary", "arbitrary"),
        name="rglru",
    )(proj3, proj3, conv_w, conv_b.reshape(1, width), gate_a_w.astype(BF16),
      gate_a_b.reshape(1, width), gate_x_w.astype(BF16), gate_x_b.reshape(1, width),
      lam.reshape(1, width))


def kernel(x, c, ada_w, ada_b, norm_pre, norm_post, ffn_w13, ffn_w2, ev_w_in, ev_conv_w, ev_a_log,
           ev_dt_bias, ev_o_norm, ev_ret_norm, ev_w_out, od_w_in, od_conv_w, od_conv_b,
           od_gate_a_w, od_gate_a_b, od_gate_x_w, od_gate_x_b, od_lambda, od_w_out):
    b, seq, d = x.shape
    depth = ada_w.shape[0]
    m = b * seq
    mod = _ada(c, ada_w, ada_b).reshape(depth, b, N_SUB, 3, d)
    x2 = x.reshape(m, d)
    w13_b = ffn_w13.astype(BF16)
    w2_b = ffn_w2.astype(BF16)

    for layer in range(depth):
        mod_l = mod[layer]
        x2 = _ffn(x2, mod_l[:, 0], norm_pre[layer, 0], norm_post[layer, 0], w13_b, w2_b, layer, 0, seq, 0.5)
        if layer % 2 == 0:
            e = layer // 2
            n_heads = ev_a_log.shape[1]
            gw = n_heads * GDN_HEAD_DIM
            w_in = ev_w_in[e]
            w_main = jnp.concatenate([w_in[:, :3 * gw], w_in[:, 3 * gw + 2 * n_heads:]], axis=1)
            w_small = jnp.pad(w_in[:, 3 * gw:3 * gw + 2 * n_heads],
                              ((0, 0), (0, LANES - 2 * n_heads)))
            proj, small = _inproj(x2, mod_l[:, 1], norm_pre[layer, 1], w_main.astype(BF16), seq,
                                  w_small.astype(BF16))
            proj3 = proj.reshape(b, seq, -1)
            o_a = _gdn(proj3, small.reshape(b, seq, LANES), ev_conv_w[e], ev_a_log[e],
                       ev_dt_bias[e], ev_o_norm[e])
            o_b = _retention(proj3, ev_ret_norm[e], 4 * gw)
            acts = [o_a.reshape(m, -1), o_b.reshape(m, -1)]
            w_out = ev_w_out[e]
        else:
            o = layer // 2
            proj = _inproj(x2, mod_l[:, 1], norm_pre[layer, 1], od_w_in[o].astype(BF16), seq)
            hs = _lru(proj.reshape(b, seq, -1), od_conv_w[o], od_conv_b[o], od_gate_a_w[o],
                      od_gate_a_b[o], od_gate_x_w[o], od_gate_x_b[o], od_lambda[o])
            acts = [hs.reshape(m, -1)]
            w_out = od_w_out[o]
        x2 = _outproj(acts, w_out.astype(BF16), x2, mod_l[:, 1], norm_post[layer, 1], seq, 1.0)
        x2 = _ffn(x2, mod_l[:, 2], norm_pre[layer, 2], norm_post[layer, 2], w13_b, w2_b, layer, 1, seq, 0.5)
    return x2.reshape(b, seq, d)
```

```python
import functools
import math

import jax
import jax.numpy as jnp
from jax import lax
from jax.experimental import pallas as pl
from jax.experimental.pallas import tpu as pltpu

F32 = jnp.float32
BF16 = jnp.bfloat16

EPS = 1e-6
GDN_HEAD_DIM = 128
RET_HEAD_DIM = 256
CHUNK = 64
CONV_WIDTH = 4
ROPE_BASE = 10000.0
LRU_C = 8.0
N_SUB = 3

SUBLANES = 8
LANES = 128
V7X_VMEM_LIMIT_BYTES = 56 * 1024 * 1024


def _cparams(*semantics):
    return pltpu.CompilerParams(dimension_semantics=semantics,
                                vmem_limit_bytes=V7X_VMEM_LIMIT_BYTES)


def _dot(a, b):
    return jnp.dot(a, b, preferred_element_type=F32)


def _dot_nt(a, b):
    return lax.dot_general(a, b, (((1,), (1,)), ((), ())), preferred_element_type=F32)


def _dot_tn(a, b):
    return lax.dot_general(a, b, (((0,), (0,)), ((), ())), preferred_element_type=F32)


def _sigmoid(x):
    return jax.nn.sigmoid(x)


def _silu(x):
    return x * jax.nn.sigmoid(x)


def _softplus(x):
    return jnp.maximum(x, 0.0) + jnp.log1p(jnp.exp(-jnp.abs(x)))


def _gelu_tanh(x):
    c = math.sqrt(2.0 / math.pi)
    return 0.5 * x * (1.0 + jnp.tanh(c * (x + 0.044715 * (x * x * x))))


def _rms(x, gain):
    return x * lax.rsqrt(jnp.mean(x * x, axis=-1, keepdims=True) + EPS) * gain


def _norm_mod(x, g_pre, mod_ref):
    return _rms(x, g_pre) * (1.0 + mod_ref[0, 1:2, :]) + mod_ref[0, 0:1, :]


def _post_residual(x, f, g_post, mod_ref, res_w):
    return x + (res_w * (1.0 + mod_ref[0, 2:3, :])) * _rms(f, g_post)


def _pick(n, pref):
    if n <= pref:
        return n
    for t in range(pref - pref % LANES, 0, -LANES):
        if n % t == 0:
            return t
    raise ValueError(f"no lane-aligned tile of {n} at or below {pref}")


def _ada_kernel(c_ref, w_ref, b_ref, o_ref):
    a = _silu(c_ref[...]).astype(BF16)
    o_ref[0] = _dot(a, w_ref[0].astype(BF16)) + b_ref[0]


def _ada(c, ada_w, ada_b):
    n_layers, d, n = ada_w.shape
    b = c.shape[0]
    tn = _pick(n, 1024)
    return pl.pallas_call(
        _ada_kernel,
        grid=(n_layers, n // tn),
        in_specs=[
            pl.BlockSpec((b, d), lambda l, j: (0, 0)),
            pl.BlockSpec((1, d, tn), lambda l, j: (l, 0, j)),
            pl.BlockSpec((1, 1, tn), lambda l, j: (l, 0, j)),
        ],
        out_specs=pl.BlockSpec((1, b, tn), lambda l, j: (l, 0, j)),
        out_shape=jax.ShapeDtypeStruct((n_layers, b, n), F32),
        compiler_params=_cparams("arbitrary", "arbitrary"),
        name="ada_mod",
    )(c, ada_w, ada_b.reshape(n_layers, 1, n))


def _ffn_kernel(x_ref, mod_ref, gpre_ref, gpost_ref, w1_ref, w3_ref, w2_ref, o_ref,
                h_scr, acc_scr, *, res_w, n_f):
    j = pl.program_id(1)

    @pl.when(j == 0)
    def _():
        h_scr[...] = _norm_mod(x_ref[...], gpre_ref[...], mod_ref).astype(BF16)
        acc_scr[...] = jnp.zeros_like(acc_scr)

    h = h_scr[...]
    g = _dot(h, w1_ref[...])
    u = _dot(h, w3_ref[...])
    a = (_silu(g) * u).astype(BF16)
    acc_scr[...] += _dot(a, w2_ref[...])

    @pl.when(j == n_f - 1)
    def _():
        o_ref[...] = _post_residual(x_ref[...], acc_scr[...], gpost_ref[...], mod_ref, res_w)


def _ffn(x2, mod_s, g_pre, g_post, w13_all, w2_all, layer, which, seq, res_w):
    m, d = x2.shape
    f = w2_all.shape[2]
    tm = _pick(seq, 512)
    tf = _pick(f, 512)
    n_f = f // tf
    per_b = seq // tm
    return pl.pallas_call(
        functools.partial(_ffn_kernel, res_w=res_w, n_f=n_f),
        grid=(m // tm, n_f),
        in_specs=[
            pl.BlockSpec((tm, d), lambda i, j: (i, 0)),
            pl.BlockSpec((1, 3, d), lambda i, j: (i // per_b, 0, 0)),
            pl.BlockSpec((1, d), lambda i, j: (0, 0)),
            pl.BlockSpec((1, d), lambda i, j: (0, 0)),
            pl.BlockSpec((None, None, d, tf), lambda i, j: (layer, which, 0, j)),
            pl.BlockSpec((None, None, d, tf), lambda i, j: (layer, which, 0, j + n_f)),
            pl.BlockSpec((None, None, tf, d), lambda i, j: (layer, which, j, 0)),
        ],
        out_specs=pl.BlockSpec((tm, d), lambda i, j: (i, 0)),
        out_shape=jax.ShapeDtypeStruct((m, d), F32),
        scratch_shapes=[pltpu.VMEM((tm, d), BF16), pltpu.VMEM((tm, d), F32)],
        compiler_params=_cparams("arbitrary", "arbitrary"),
        name="ffn",
    )(x2, mod_s, g_pre.reshape(1, d), g_post.reshape(1, d), w13_all, w13_all, w2_all)


def _inproj_kernel(x_ref, mod_ref, gpre_ref, w_ref, *rest, has_small):
    if has_small:
        ws_ref, o_ref, os_ref, h_scr = rest
    else:
        o_ref, h_scr = rest
    j = pl.program_id(1)

    @pl.when(j == 0)
    def _():
        h = _norm_mod(x_ref[...], gpre_ref[...], mod_ref).astype(BF16)
        h_scr[...] = h
        if has_small:
            os_ref[...] = _dot(h, ws_ref[...])

    o_ref[...] = _dot(h_scr[...], w_ref[...])


def _inproj(x2, mod_s, g_pre, w, seq, w_small=None):
    m, d = x2.shape
    n = w.shape[1]
    tm = _pick(seq, 1024)
    tn = _pick(n, 1024)
    per_b = seq // tm
    has_small = w_small is not None
    in_specs = [
        pl.BlockSpec((tm, d), lambda i, j: (i, 0)),
        pl.BlockSpec((1, 3, d), lambda i, j: (i // per_b, 0, 0)),
        pl.BlockSpec((1, d), lambda i, j: (0, 0)),
        pl.BlockSpec((d, tn), lambda i, j: (0, j)),
    ]
    out_specs = [pl.BlockSpec((tm, tn), lambda i, j: (i, j))]
    out_shape = [jax.ShapeDtypeStruct((m, n), F32)]
    args = [x2, mod_s, g_pre.reshape(1, d), w]
    if has_small:
        ns = w_small.shape[1]
        in_specs.append(pl.BlockSpec((d, ns), lambda i, j: (0, 0)))
        out_specs.append(pl.BlockSpec((tm, ns), lambda i, j: (i, 0)))
        out_shape.append(jax.ShapeDtypeStruct((m, ns), F32))
        args.append(w_small)
    out = pl.pallas_call(
        functools.partial(_inproj_kernel, has_small=has_small),
        grid=(m // tm, n // tn),
        in_specs=in_specs,
        out_specs=out_specs,
        out_shape=out_shape,
        scratch_shapes=[pltpu.VMEM((tm, d), BF16)],
        compiler_params=_cparams("arbitrary", "arbitrary"),
        name="mixer_inproj",
    )(*args)
    return out if has_small else out[0]


def _outproj_kernel(*refs, n_in, res_w):
    a_refs = refs[:n_in]
    w_refs = refs[n_in:2 * n_in]
    x_ref, mod_ref, gpost_ref, o_ref = refs[2 * n_in:]
    f = _dot(a_refs[0][...], w_refs[0][...])
    for a_ref, w_ref in zip(a_refs[1:], w_refs[1:]):
        f = f + _dot(a_ref[...], w_ref[...])
    o_ref[...] = _post_residual(x_ref[...], f, gpost_ref[...], mod_ref, res_w)


def _outproj(acts, w_out, x2, mod_s, g_post, seq, res_w):
    m, d = x2.shape
    tm = _pick(seq, 512)
    per_b = seq // tm
    n_in = len(acts)
    in_specs, args = [], []
    for a in acts:
        in_specs.append(pl.BlockSpec((tm, a.shape[1]), lambda i: (i, 0)))
        args.append(a)
    row = 0
    for a in acts:
        wi = a.shape[1]
        assert row % wi == 0
        in_specs.append(pl.BlockSpec((wi, d), lambda i, r=row // wi: (r, 0)))
        args.append(w_out)
        row += wi
    in_specs += [
        pl.BlockSpec((tm, d), lambda i: (i, 0)),
        pl.BlockSpec((1, 3, d), lambda i: (i // per_b, 0, 0)),
        pl.BlockSpec((1, d), lambda i: (0, 0)),
    ]
    args += [x2, mod_s, g_post.reshape(1, d)]
    return pl.pallas_call(
        functools.partial(_outproj_kernel, n_in=n_in, res_w=res_w),
        grid=(m // tm,),
        in_specs=in_specs,
        out_specs=pl.BlockSpec((tm, d), lambda i: (i, 0)),
        out_shape=jax.ShapeDtypeStruct((m, d), F32),
        compiler_params=_cparams("arbitrary"),
        name="mixer_outproj",
    )(*args)


def _conv_from_scratch(cs_ref, w_ref, rows, lo, hi):
    acc = cs_ref[SUBLANES:SUBLANES + rows, lo:hi] * w_ref[CONV_WIDTH - 1:CONV_WIDTH, lo:hi]
    for back in range(1, CONV_WIDTH):
        tap = CONV_WIDTH - 1 - back
        acc = acc + cs_ref[SUBLANES - back:SUBLANES - back + rows, lo:hi] * w_ref[tap:tap + 1, lo:hi]
    return acc


GDN_BLOCK = 256
GDN_PACK = 4


def _tile_rows(x, n):
    return jnp.concatenate([x] * n, axis=0)


def _gdn_kernel(q_ref, k_ref, v_ref, z_ref, sm_ref, cw_ref, alog_ref, dtb_ref, onorm_ref,
                o_ref, s_scr, cs_scr, qn_scr, kn_scr, kb_scr, vb_scr, kbe_scr, kst_scr, egq_scr,
                a_scr, qk_scr, xo_scr, l_scr, ku_scr, oin_scr, *, n_heads, rows):
    t = pl.program_id(1)
    dk = GDN_HEAD_DIM
    gw = n_heads * dk
    n_chunks = rows // CHUNK
    n_groups = n_heads // GDN_PACK
    pw = GDN_PACK * CHUNK
    gk = GDN_PACK * dk

    @pl.when(t == 0)
    def _():
        s_scr[...] = jnp.zeros_like(s_scr)
        cs_scr[0:SUBLANES, :] = jnp.zeros((SUBLANES, 3 * gw), F32)

    cs_scr[SUBLANES:SUBLANES + rows, 0:gw] = q_ref[0]
    cs_scr[SUBLANES:SUBLANES + rows, gw:2 * gw] = k_ref[0]
    cs_scr[SUBLANES:SUBLANES + rows, 2 * gw:3 * gw] = v_ref[0]

    sm = sm_ref[0]
    beta_all = _sigmoid(sm)
    g_all = -jnp.exp(alog_ref[...]) * _softplus(sm + dtb_ref[...])
    in_chunk = lax.broadcasted_iota(jnp.int32, (rows, LANES), 0) & (CHUNK - 1)
    d = 1
    while d < CHUNK:
        g_all = g_all + jnp.where(in_chunk >= d, pltpu.roll(g_all, d, 0), 0.0)
        d *= 2
    eg_all = jnp.exp(g_all)
    glast_all = jnp.concatenate(
        [jnp.broadcast_to(g_all[(c + 1) * CHUNK - 1:(c + 1) * CHUNK, :], (CHUNK, LANES))
         for c in range(n_chunks)], axis=0)
    ekl_all = jnp.exp(glast_all - g_all)

    for h in range(n_heads):
        lo, hi = h * dk, (h + 1) * dk
        bcast = lambda arr, idx: jnp.broadcast_to(arr[:, idx:idx + 1], (rows, dk))
        beta_b = bcast(beta_all, h)
        eg_b = bcast(eg_all, n_heads + h)
        q = _silu(_conv_from_scratch(cs_scr, cw_ref, rows, lo, hi))
        qn = (q * lax.rsqrt(jnp.sum(q * q, axis=-1, keepdims=True) + EPS)) * dk ** -0.5
        qn_scr[:, lo:hi] = qn.astype(BF16)
        egq_scr[:, lo:hi] = qn * eg_b
        k = _silu(_conv_from_scratch(cs_scr, cw_ref, rows, gw + lo, gw + hi))
        kn = k * lax.rsqrt(jnp.sum(k * k, axis=-1, keepdims=True) + EPS)
        kb = kn * beta_b
        kn_scr[:, lo:hi] = kn.astype(BF16)
        kb_scr[:, lo:hi] = kb.astype(BF16)
        kbe_scr[:, lo:hi] = kb * eg_b
        kst_scr[:, lo:hi] = (kn * bcast(ekl_all, n_heads + h)).astype(BF16)
        v = _silu(_conv_from_scratch(cs_scr, cw_ref, rows, 2 * gw + lo, 2 * gw + hi))
        vb_scr[:, lo:hi] = v * beta_b
    cs_scr[0:SUBLANES, :] = cs_scr[rows:rows + SUBLANES, :]

    ri = lax.broadcasted_iota(jnp.int32, (CHUNK, pw), 0)
    li = lax.broadcasted_iota(jnp.int32, (CHUNK, pw), 1)
    cj = li & (CHUNK - 1)
    lh = li // CHUNK
    causal = ri >= cj
    strict = ri > cj
    eye = ri == cj
    bd_r = lax.broadcasted_iota(jnp.int32, (pw, pw), 0) // CHUNK
    bd_c = lax.broadcasted_iota(jnp.int32, (pw, pw), 1) // CHUNK
    bd_sq = bd_r == bd_c
    bk_r = lax.broadcasted_iota(jnp.int32, (pw, gk), 0) // CHUNK
    bk_c = lax.broadcasted_iota(jnp.int32, (pw, gk), 1) // dk
    bd_k = bk_r == bk_c
    zero_sq = jnp.zeros((pw, pw), BF16)
    zero_k = jnp.zeros((pw, gk), BF16)

    def block_diag(x_rp):
        return jnp.where(bd_sq, _tile_rows(x_rp.astype(BF16), GDN_PACK), zero_sq)

    def block_diag_k(x_cat):
        return jnp.where(bd_k, _tile_rows(x_cat, GDN_PACK), zero_k)

    probs = [(c, gi) for c in range(n_chunks) for gi in range(n_groups)]
    for p, (c, gi) in enumerate(probs):
        r0, r1 = c * CHUNK, (c + 1) * CHUNK
        c0 = gi * gk
        gcs = [jnp.broadcast_to(g_all[r0:r1, n_heads + gi * GDN_PACK + s:n_heads + gi * GDN_PACK + s + 1],
                                (CHUNK, pw)) for s in range(GDN_PACK)]
        gc = gcs[GDN_PACK - 1]
        for s in range(GDN_PACK - 2, -1, -1):
            gc = jnp.where(lh == s, gcs[s], gc)
        gr = jnp.sum(jnp.where(eye, gc, 0.0), axis=0, keepdims=True)
        decay = jnp.exp(jnp.where(causal, gc - gr, -jnp.inf))
        lhs = jnp.concatenate([kb_scr[r0:r1, c0:c0 + gk], qn_scr[r0:r1, c0:c0 + gk]], axis=0)
        kq = _dot_nt(lhs, block_diag_k(kn_scr[r0:r1, c0:c0 + gk]))
        a = jnp.where(strict, kq[0:CHUNK] * decay, 0.0)
        a_scr[p] = a
        qk_scr[p] = jnp.where(causal, kq[CHUNK:2 * CHUNK] * decay, 0.0)
        xo_scr[p] = -jnp.where((ri - cj == 1) & ((ri & 1) == 1), a, 0.0)

    s = 2
    while s < CHUNK:
        rb = ri // s
        off = ((rb & 1) == 1) & ((cj // s) == rb - 1)
        for p in range(len(probs)):
            a_off = jnp.where(off, a_scr[p], 0.0)
            xo = xo_scr[p]
            y = a_off + _dot(xo.astype(BF16), block_diag(a_off))
            xo_scr[p] = xo - (y + _dot(y.astype(BF16), block_diag(xo)))
        s *= 2

    for p, (c, gi) in enumerate(probs):
        r0, r1 = c * CHUNK, (c + 1) * CHUNK
        rhs = jnp.concatenate(
            [jnp.concatenate([vb_scr[r0:r1, h * dk:(h + 1) * dk], kbe_scr[r0:r1, h * dk:(h + 1) * dk]], axis=1)
             for h in range(gi * GDN_PACK, (gi + 1) * GDN_PACK)], axis=0)
        sol = rhs + _dot(block_diag(xo_scr[p]), rhs.astype(BF16))
        sol_b = sol.astype(BF16)
        qks = _dot(block_diag(qk_scr[p]), sol_b)
        ks = _dot_tn(block_diag_k(kst_scr[r0:r1, gi * gk:(gi + 1) * gk]), sol_b)
        for s_ in range(GDN_PACK):
            h = gi * GDN_PACK + s_
            half = (h % 2) * dk
            rr = slice(s_ * CHUNK, (s_ + 1) * CHUNK)
            kr = slice(s_ * dk, (s_ + 1) * dk)
            q_eff = egq_scr[r0:r1, h * dk:(h + 1) * dk] - qks[rr, dk:2 * dk]
            l_scr[c, h // 2, 0:dk, half:half + dk] = ks[kr, dk:2 * dk].astype(BF16)
            l_scr[c, h // 2, dk:dk + CHUNK, half:half + dk] = q_eff.astype(BF16)
            ku_scr[c, h] = ks[kr, 0:dk]
            oin_scr[r0:r1, h * dk:(h + 1) * dk] = qks[rr, 0:dk]

    zero = jnp.zeros((dk, dk), BF16)
    for c in range(n_chunks):
        r0, r1 = c * CHUNK, (c + 1) * CHUNK
        for pr in range(n_heads // 2):
            h0, h1 = 2 * pr, 2 * pr + 1
            s0 = s_scr[h0]
            s1 = s_scr[h1]
            s_bd = jnp.concatenate(
                [jnp.concatenate([s0.astype(BF16), zero], axis=1),
                 jnp.concatenate([zero, s1.astype(BF16)], axis=1)], axis=0)
            r = _dot(l_scr[c, pr], s_bd)
            for h, st, off_ in ((h0, s0, 0), (h1, s1, dk)):
                cd = eg_all[r1 - 1:r1, n_heads + h:n_heads + h + 1]
                s_scr[h] = st * cd - r[0:dk, off_:off_ + dk] + ku_scr[c, h]
                o = r[dk:dk + CHUNK, off_:off_ + dk] + oin_scr[r0:r1, h * dk:(h + 1) * dk]
                out = _rms(o, onorm_ref[...]) * _silu(z_ref[0, r0:r1, h * dk:(h + 1) * dk])
                o_ref[0, r0:r1, h * dk:(h + 1) * dk] = out.astype(o_ref.dtype)


def _gdn(proj3, small3, conv_w, a_log, dt_bias, o_norm):
    b, seq, _ = proj3.shape
    n_heads = a_log.shape[0]
    assert n_heads % GDN_PACK == 0 and GDN_HEAD_DIM == LANES
    gw = n_heads * GDN_HEAD_DIM
    rows = _pick(seq, GDN_BLOCK)
    n_chunks = rows // CHUNK
    n_prob = n_chunks * (n_heads // GDN_PACK)
    pad = LANES - 2 * n_heads
    alog_row = jnp.pad(a_log, (n_heads, pad)).reshape(1, LANES)
    dtb_row = jnp.pad(dt_bias, (n_heads, pad)).reshape(1, LANES)
    col = lambda c: pl.BlockSpec((1, rows, gw), lambda i, t, c=c: (i, t, c))
    return pl.pallas_call(
        functools.partial(_gdn_kernel, n_heads=n_heads, rows=rows),
        grid=(b, seq // rows),
        in_specs=[
            col(0), col(1), col(2), col(3),
            pl.BlockSpec((1, rows, LANES), lambda i, t: (i, t, 0)),
            pl.BlockSpec((CONV_WIDTH, 3 * gw), lambda i, t: (0, 0)),
            pl.BlockSpec((1, LANES), lambda i, t: (0, 0)),
            pl.BlockSpec((1, LANES), lambda i, t: (0, 0)),
            pl.BlockSpec((1, GDN_HEAD_DIM), lambda i, t: (0, 0)),
        ],
        out_specs=pl.BlockSpec((1, rows, gw), lambda i, t: (i, t, 0)),
        out_shape=jax.ShapeDtypeStruct((b, seq, gw), BF16),
        scratch_shapes=[
            pltpu.VMEM((n_heads, GDN_HEAD_DIM, GDN_HEAD_DIM), F32),
            pltpu.VMEM((rows + SUBLANES, 3 * gw), F32),
            pltpu.VMEM((rows, gw), BF16),
            pltpu.VMEM((rows, gw), BF16),
            pltpu.VMEM((rows, gw), BF16),
            pltpu.VMEM((rows, gw), F32),
            pltpu.VMEM((rows, gw), F32),
            pltpu.VMEM((rows, gw), BF16),
            pltpu.VMEM((rows, gw), F32),
            pltpu.VMEM((n_prob, CHUNK, GDN_PACK * CHUNK), F32),
            pltpu.VMEM((n_prob, CHUNK, GDN_PACK * CHUNK), F32),
            pltpu.VMEM((n_prob, CHUNK, GDN_PACK * CHUNK), F32),
            pltpu.VMEM((n_chunks, n_heads // 2, GDN_HEAD_DIM + CHUNK, 2 * GDN_HEAD_DIM), BF16),
            pltpu.VMEM((n_chunks, n_heads, GDN_HEAD_DIM, GDN_HEAD_DIM), F32),
            pltpu.VMEM((rows, gw), F32),
        ],
        compiler_params=_cparams("arbitrary", "arbitrary"),
        name="gdn",
    )(proj3, proj3, proj3, proj3, small3, conv_w, alog_row, dtb_row,
      o_norm.reshape(1, GDN_HEAD_DIM))


def _rotary(x, cos, sin):
    half = x.shape[-1] // 2
    x1, x2 = x[:, :half], x[:, half:]
    return jnp.concatenate([x1 * cos - x2 * sin, x2 * cos + x1 * sin], axis=-1)


def _ret_kernel(q_ref, k_ref, v_ref, g_ref, cos_ref, sin_ref, norm_ref, o_ref, s_scr,
                *, n_heads, rows):
    t = pl.program_id(1)
    dk = RET_HEAD_DIM

    @pl.when(t == 0)
    def _():
        s_scr[...] = jnp.zeros_like(s_scr)

    cos, sin = cos_ref[...], sin_ref[...]
    ri = lax.broadcasted_iota(jnp.int32, (rows, rows), 0)
    ci = lax.broadcasted_iota(jnp.int32, (rows, rows), 1)
    delta = (ri - ci).astype(F32)
    causal = ri >= ci
    pos = lax.broadcasted_iota(jnp.int32, (rows, 1), 0).astype(F32)

    for h in range(n_heads):
        lo, hi = h * dk, (h + 1) * dk
        log_gamma = math.log1p(-(2.0 ** (-5.0 - h)))
        q = _rotary(q_ref[0, :, lo:hi], cos, sin)
        k = _rotary(k_ref[0, :, lo:hi], cos, sin) * dk ** -0.5
        v_b = v_ref[0, :, lo:hi].astype(BF16)
        intra = jnp.exp(jnp.where(causal, delta * log_gamma, -jnp.inf))
        scores = _dot_nt(q.astype(BF16), k.astype(BF16)) * intra
        q_in = q * jnp.exp((pos + 1.0) * log_gamma)
        k_st = k * jnp.exp((rows - 1.0 - pos) * log_gamma)
        state = s_scr[h]
        o = _dot(scores.astype(BF16), v_b) + _dot(q_in.astype(BF16), state.astype(BF16))
        s_scr[h] = state * math.exp(rows * log_gamma) + _dot_tn(k_st.astype(BF16), v_b)
        out = _rms(o, norm_ref[:, lo:hi]) * _silu(g_ref[0, :, lo:hi])
        o_ref[0, :, lo:hi] = out.astype(o_ref.dtype)


def _retention(proj3, ret_norm, col0):
    b, seq, _ = proj3.shape
    rw = ret_norm.shape[0]
    n_heads = rw // RET_HEAD_DIM
    rows = _pick(seq, 256)
    half = RET_HEAD_DIM // 2
    inv_freq = ROPE_BASE ** (-jnp.arange(half, dtype=F32) / half)
    ang = jnp.arange(seq, dtype=F32)[:, None] * inv_freq[None, :]
    cos, sin = jnp.cos(ang), jnp.sin(ang)
    assert col0 % rw == 0
    c0 = col0 // rw
    col = lambda c: pl.BlockSpec((1, rows, rw), lambda i, t, c=c: (i, t, c0 + c))
    return pl.pallas_call(
        functools.partial(_ret_kernel, n_heads=n_heads, rows=rows),
        grid=(b, seq // rows),
        in_specs=[
            col(0), col(1), col(2), col(3),
            pl.BlockSpec((rows, half), lambda i, t: (t, 0)),
            pl.BlockSpec((rows, half), lambda i, t: (t, 0)),
            pl.BlockSpec((1, rw), lambda i, t: (0, 0)),
        ],
        out_specs=pl.BlockSpec((1, rows, rw), lambda i, t: (i, t, 0)),
        out_shape=jax.ShapeDtypeStruct((b, seq, rw), BF16),
        scratch_shapes=[pltpu.VMEM((n_heads, RET_HEAD_DIM, RET_HEAD_DIM), F32)],
        compiler_params=_cparams("arbitrary", "arbitrary"),
        name="retention",
    )(proj3, proj3, proj3, proj3, cos, sin, ret_norm.reshape(1, rw))


LRU_ROWS = 256


def _lru_block(p_ref, p_prev_ref, seq_start, between, perm_ref, cw_ref, cb_ref, wa_ref, ba_ref, wx_ref,
               bx_ref, lam_ref, o_ref, row0, h_scr, act_scr, *, n_blocks, rows, bw):
    width = n_blocks * bw
    nj = rows // SUBLANES
    sub = lax.broadcasted_iota(jnp.int32, (SUBLANES, bw), 0)
    tails = [jnp.where(seq_start, 0.0, p_prev_ref[rows - SUBLANES * (k - 1) - 1:rows - SUBLANES * (k - 1),
                                                  width:2 * width]) for k in range(1, CONV_WIDTH)]

    for n in range(n_blocks):
        between(n)
        lo, hi = n * bw, (n + 1) * bw
        xp = p_ref[:, width + lo:width + hi]

        def edge(k):
            prev = pltpu.roll(xp[rows - k * SUBLANES:rows - (k - 1) * SUBLANES], 1, 0)
            return jnp.where(sub == 0, tails[k - 1][:, lo:hi], prev)

        edges = [edge(k) for k in range(1, CONV_WIDTH)]
        xc = xp * cw_ref[CONV_WIDTH - 1:CONV_WIDTH, lo:hi]
        for back in range(1, CONV_WIDTH):
            tap = CONV_WIDTH - 1 - back
            shifted = jnp.concatenate(edges[:back][::-1] + [xp[:rows - back * SUBLANES]], axis=0)
            xc = xc + shifted * cw_ref[tap:tap + 1, lo:hi]
        xc = xc + cb_ref[:, lo:hi]
        xc_b = xc.astype(BF16)
        r = _sigmoid(_dot(xc_b, wa_ref[n]) + ba_ref[:, lo:hi])
        i = _sigmoid(_dot(xc_b, wx_ref[n]) + bx_ref[:, lo:hi])
        log_a = (-LRU_C * r) * _softplus(-lam_ref[:, lo:hi])
        a = jnp.exp(log_a)
        mult = jnp.sqrt(-jnp.tanh(log_a) * (a * a + 1.0))
        bv = mult * (i * xc)

        h = bv[0:SUBLANES]
        ac = a[0:SUBLANES]
        hl, al = [h], [ac]
        for j in range(1, nj):
            aj = a[j * SUBLANES:(j + 1) * SUBLANES]
            h = aj * h + bv[j * SUBLANES:(j + 1) * SUBLANES]
            ac = aj * ac
            hl.append(h)
            al.append(ac)
        c = jnp.where(seq_start, 0.0, h_scr[0:1, lo:hi])
        cs = []
        for s in range(SUBLANES):
            cs.append(c)
            c = h[s:s + 1] + ac[s:s + 1] * c
        h_scr[0:1, lo:hi] = c
        c_in = jnp.concatenate(cs, axis=0)
        hs = jnp.concatenate([hl[j] + al[j] * c_in for j in range(nj)], axis=0)
        act_scr[:, lo:hi] = (hs * _gelu_tanh(p_ref[:, lo:hi])).astype(BF16)

    o_ref[row0:row0 + rows, :] = _dot_tn(perm_ref[...], act_scr[...]).astype(o_ref.dtype)


def _lru_in_kernel(x0_ref, mod0_ref, xa_ref, moda_ref, xb_ref, modb_ref, gpre_ref, win_ref, perm_ref,
                   cw_ref, cb_ref, wa_ref, ba_ref, wx_ref, bx_ref, lam_ref, o_ref,
                   p0_scr, p1_scr, h_scr, act_scr, *, n_blocks, rows, bw, blocks_per_seq):
    s = pl.program_id(0)

    def in_proj_cols(h, p_ref, n):
        cw = win_ref.shape[1] // n_blocks
        p_ref[:, n * cw:(n + 1) * cw] = _dot(h, win_ref[:, n * cw:(n + 1) * cw])

    def normed(x_ref, mod_ref):
        h = _norm_mod(x_ref[...], gpre_ref[...], mod_ref).astype(BF16)
        return _dot(perm_ref[...], h).astype(BF16)

    @pl.when(s == 0)
    def _():
        h0 = normed(x0_ref, mod0_ref)
        for n in range(n_blocks):
            in_proj_cols(h0, p0_scr, n)
        p1_scr[...] = jnp.zeros_like(p1_scr)
        h_scr[...] = jnp.zeros_like(h_scr)

    lru = functools.partial(_lru_block, perm_ref=perm_ref, cw_ref=cw_ref, cb_ref=cb_ref, wa_ref=wa_ref,
                            ba_ref=ba_ref, wx_ref=wx_ref, bx_ref=bx_ref, lam_ref=lam_ref, o_ref=o_ref,
                            h_scr=h_scr, act_scr=act_scr, n_blocks=n_blocks, rows=rows, bw=bw)
    seq_start = (2 * s) % blocks_per_seq == 0
    ha = normed(xa_ref, moda_ref)
    lru(p0_scr, p1_scr, seq_start, lambda n: in_proj_cols(ha, p1_scr, n), row0=0)
    hb = normed(xb_ref, modb_ref)
    lru(p1_scr, p0_scr, False, lambda n: in_proj_cols(hb, p0_scr, n), row0=rows)


def _lru_in(x2, mod_s, g_pre, w_in, conv_w, conv_b, gate_a_w, gate_a_b, gate_x_w, gate_x_b, lam, seq):
    m, d = x2.shape
    width = w_in.shape[1] // 2
    n_blocks, bw, _ = gate_a_w.shape
    rows = _pick(seq, LRU_ROWS)
    bps = seq // rows
    nb = m // rows
    assert bps % 2 == 0 and nb % 2 == 0
    last = nb - 1
    r_idx = jnp.arange(rows)
    t_of_r = (r_idx % SUBLANES) * (rows // SUBLANES) + r_idx // SUBLANES
    perm = (t_of_r[:, None] == r_idx[None, :]).astype(BF16)
    xa = lambda s: 2 * s + 1
    xb = lambda s: jnp.minimum(2 * s + 2, last)
    vec = lambda: pl.BlockSpec((1, width), lambda s: (0, 0))
    gate = lambda: pl.BlockSpec((n_blocks, bw, bw), lambda s: (0, 0, 0))
    return pl.pallas_call(
        functools.partial(_lru_in_kernel, n_blocks=n_blocks, rows=rows, bw=bw, blocks_per_seq=bps),
        grid=(nb // 2,),
        in_specs=[
            pl.BlockSpec((rows, d), lambda s: (0, 0)),
            pl.BlockSpec((1, 3, d), lambda s: (0, 0, 0)),
            pl.BlockSpec((rows, d), lambda s: (xa(s), 0)),
            pl.BlockSpec((1, 3, d), lambda s: (xa(s) // bps, 0, 0)),
            pl.BlockSpec((rows, d), lambda s: (xb(s), 0)),
            pl.BlockSpec((1, 3, d), lambda s: (xb(s) // bps, 0, 0)),
            pl.BlockSpec((1, d), lambda s: (0, 0)),
            pl.BlockSpec((d, 2 * width), lambda s: (0, 0)),
            pl.BlockSpec((rows, rows), lambda s: (0, 0)),
            pl.BlockSpec((CONV_WIDTH, width), lambda s: (0, 0)),
            vec(), gate(), vec(), gate(), vec(), vec(),
        ],
        out_specs=pl.BlockSpec((2 * rows, width), lambda s: (s, 0)),
        out_shape=jax.ShapeDtypeStruct((m, width), BF16),
        scratch_shapes=[
            pltpu.VMEM((rows, 2 * width), F32),
            pltpu.VMEM((rows, 2 * width), F32),
            pltpu.VMEM((SUBLANES, width), F32),
            pltpu.VMEM((rows, width), BF16),
        ],
        compiler_params=_cparams("arbitrary"),
        name="rglru_in",
    )(x2, mod_s, x2, mod_s, x2, mod_s, g_pre.reshape(1, d), w_in, perm, conv_w, conv_b.reshape(1, width),
      gate_a_w.astype(BF16), gate_a_b.reshape(1, width), gate_x_w.astype(BF16),
      gate_x_b.reshape(1, width), lam.reshape(1, width))


def kernel(x, c, ada_w, ada_b, norm_pre, norm_post, ffn_w13, ffn_w2, ev_w_in, ev_conv_w, ev_a_log,
           ev_dt_bias, ev_o_norm, ev_ret_norm, ev_w_out, od_w_in, od_conv_w, od_conv_b,
           od_gate_a_w, od_gate_a_b, od_gate_x_w, od_gate_x_b, od_lambda, od_w_out):
    b, seq, d = x.shape
    depth = ada_w.shape[0]
    m = b * seq
    mod = _ada(c, ada_w, ada_b).reshape(depth, b, N_SUB, 3, d)
    x2 = x.reshape(m, d)
    w13_b = ffn_w13.astype(BF16)
    w2_b = ffn_w2.astype(BF16)

    for layer in range(depth):
        mod_l = mod[layer]
        x2 = _ffn(x2, mod_l[:, 0], norm_pre[layer, 0], norm_post[layer, 0], w13_b, w2_b, layer, 0, seq, 0.5)
        if layer % 2 == 0:
            e = layer // 2
            n_heads = ev_a_log.shape[1]
            gw = n_heads * GDN_HEAD_DIM
            w_in = ev_w_in[e]
            w_main = jnp.concatenate([w_in[:, :3 * gw], w_in[:, 3 * gw + 2 * n_heads:]], axis=1)
            w_small = jnp.pad(w_in[:, 3 * gw:3 * gw + 2 * n_heads],
                              ((0, 0), (0, LANES - 2 * n_heads)))
            proj, small = _inproj(x2, mod_l[:, 1], norm_pre[layer, 1], w_main.astype(BF16), seq,
                                  w_small.astype(BF16))
            proj3 = proj.reshape(b, seq, -1)
            o_a = _gdn(proj3, small.reshape(b, seq, LANES), ev_conv_w[e], ev_a_log[e],
                       ev_dt_bias[e], ev_o_norm[e])
            o_b = _retention(proj3, ev_ret_norm[e], 4 * gw)
            acts = [o_a.reshape(m, -1), o_b.reshape(m, -1)]
            w_out = ev_w_out[e]
        else:
            o = layer // 2
            hs = _lru_in(x2, mod_l[:, 1], norm_pre[layer, 1], od_w_in[o].astype(BF16), od_conv_w[o],
                         od_conv_b[o], od_gate_a_w[o], od_gate_a_b[o], od_gate_x_w[o], od_gate_x_b[o],
                         od_lambda[o], seq)
            acts = [hs.reshape(m, -1)]
            w_out = od_w_out[o]
        x2 = _outproj(acts, w_out.astype(BF16), x2, mod_l[:, 1], norm_post[layer, 1], seq, 1.0)
        x2 = _ffn(x2, mod_l[:, 2], norm_pre[layer, 2], norm_post[layer, 2], w13_b, w2_b, layer, 1, seq, 0.5)
    return x2.reshape(b, seq, d)
```

```python
import functools
import math

import jax
import jax.numpy as jnp
from jax import lax
from jax.experimental import pallas as pl
from jax.experimental.pallas import tpu as pltpu

F32 = jnp.float32
BF16 = jnp.bfloat16

EPS = 1e-6
GDN_HEAD_DIM = 128
RET_HEAD_DIM = 256
CHUNK = 64
CONV_WIDTH = 4
ROPE_BASE = 10000.0
LRU_C = 8.0
N_SUB = 3

SUBLANES = 8
LANES = 128
V7X_VMEM_LIMIT_BYTES = 56 * 1024 * 1024


def _cparams(*semantics):
    return pltpu.CompilerParams(dimension_semantics=semantics,
                                vmem_limit_bytes=V7X_VMEM_LIMIT_BYTES)


def _dot(a, b):
    return jnp.dot(a, b, preferred_element_type=F32)


def _dot_nt(a, b):
    return lax.dot_general(a, b, (((1,), (1,)), ((), ())), preferred_element_type=F32)


def _dot_tn(a, b):
    return lax.dot_general(a, b, (((0,), (0,)), ((), ())), preferred_element_type=F32)


def _sigmoid(x):
    return jax.nn.sigmoid(x)


def _silu(x):
    return x * jax.nn.sigmoid(x)


def _softplus(x):
    return jnp.maximum(x, 0.0) + jnp.log1p(jnp.exp(-jnp.abs(x)))


def _gelu_tanh(x):
    c = math.sqrt(2.0 / math.pi)
    return 0.5 * x * (1.0 + jnp.tanh(c * (x + 0.044715 * (x * x * x))))


def _rms(x, gain):
    return x * lax.rsqrt(jnp.mean(x * x, axis=-1, keepdims=True) + EPS) * gain


def _inv_rms(x):
    return lax.rsqrt(jnp.mean(x * x, axis=-1, keepdims=True) + EPS)


def _norm_mod(x, g_pre, mod_ref):
    return (x * _inv_rms(x)) * (g_pre * (1.0 + mod_ref[0, 1:2, :])) + mod_ref[0, 0:1, :]


def _post_residual(x, f, g_post, mod_ref, res_w):
    return x + (f * _inv_rms(f)) * (g_post * (res_w * (1.0 + mod_ref[0, 2:3, :])))


def _pick(n, pref):
    if n <= pref:
        return n
    for t in range(pref - pref % LANES, 0, -LANES):
        if n % t == 0:
            return t
    raise ValueError(f"no lane-aligned tile of {n} at or below {pref}")


def _ada_kernel(c_ref, w_ref, b_ref, o_ref):
    a = _silu(c_ref[...]).astype(BF16)
    o_ref[0] = _dot(a, w_ref[0].astype(BF16)) + b_ref[0]


def _ada(c, ada_w, ada_b):
    n_layers, d, n = ada_w.shape
    b = c.shape[0]
    tn = _pick(n, 1024)
    return pl.pallas_call(
        _ada_kernel,
        grid=(n_layers, n // tn),
        in_specs=[
            pl.BlockSpec((b, d), lambda l, j: (0, 0)),
            pl.BlockSpec((1, d, tn), lambda l, j: (l, 0, j)),
            pl.BlockSpec((1, 1, tn), lambda l, j: (l, 0, j)),
        ],
        out_specs=pl.BlockSpec((1, b, tn), lambda l, j: (l, 0, j)),
        out_shape=jax.ShapeDtypeStruct((n_layers, b, n), F32),
        compiler_params=_cparams("arbitrary", "arbitrary"),
        name="ada_mod",
    )(c, ada_w, ada_b.reshape(n_layers, 1, n))


def _ffn_kernel(x_ref, mod_ref, gpre_ref, gpost_ref, w1_ref, w3_ref, w2_ref, o_ref,
                h_scr, acc_scr, *, res_w, n_f):
    j = pl.program_id(1)

    @pl.when(j == 0)
    def _():
        h_scr[...] = _norm_mod(x_ref[...], gpre_ref[...], mod_ref).astype(BF16)
        acc_scr[...] = jnp.zeros_like(acc_scr)

    h = h_scr[...]
    g = _dot(h, w1_ref[...])
    u = _dot(h, w3_ref[...])
    a = (_silu(g) * u).astype(BF16)
    acc_scr[...] += _dot(a, w2_ref[...])

    @pl.when(j == n_f - 1)
    def _():
        o_ref[...] = _post_residual(x_ref[...], acc_scr[...], gpost_ref[...], mod_ref, res_w)


def _ffn(x2, mod_s, g_pre, g_post, w13_all, w2_all, layer, which, seq, res_w):
    m, d = x2.shape
    f = w2_all.shape[2]
    tm = _pick(seq, 512)
    tf = _pick(f, 512)
    n_f = f // tf
    per_b = seq // tm
    return pl.pallas_call(
        functools.partial(_ffn_kernel, res_w=res_w, n_f=n_f),
        grid=(m // tm, n_f),
        in_specs=[
            pl.BlockSpec((tm, d), lambda i, j: (i, 0)),
            pl.BlockSpec((1, 3, d), lambda i, j: (i // per_b, 0, 0)),
            pl.BlockSpec((1, d), lambda i, j: (0, 0)),
            pl.BlockSpec((1, d), lambda i, j: (0, 0)),
            pl.BlockSpec((None, None, d, tf), lambda i, j: (layer, which, 0, j)),
            pl.BlockSpec((None, None, d, tf), lambda i, j: (layer, which, 0, j + n_f)),
            pl.BlockSpec((None, None, tf, d), lambda i, j: (layer, which, j, 0)),
        ],
        out_specs=pl.BlockSpec((tm, d), lambda i, j: (i, 0)),
        out_shape=jax.ShapeDtypeStruct((m, d), F32),
        scratch_shapes=[pltpu.VMEM((tm, d), BF16), pltpu.VMEM((tm, d), F32)],
        compiler_params=_cparams("arbitrary", "arbitrary"),
        name="ffn",
    )(x2, mod_s, g_pre.reshape(1, d), g_post.reshape(1, d), w13_all, w13_all, w2_all)


def _inproj_kernel(x_ref, mod_ref, gpre_ref, w_ref, *rest, has_small):
    if has_small:
        ws_ref, o_ref, os_ref, h_scr = rest
    else:
        o_ref, h_scr = rest
    j = pl.program_id(1)

    @pl.when(j == 0)
    def _():
        h = _norm_mod(x_ref[...], gpre_ref[...], mod_ref).astype(BF16)
        h_scr[...] = h
        if has_small:
            os_ref[...] = _dot(h, ws_ref[...])

    o_ref[...] = _dot(h_scr[...], w_ref[...])


def _inproj(x2, mod_s, g_pre, w, seq, w_small=None):
    m, d = x2.shape
    n = w.shape[1]
    tm = _pick(seq, 1024)
    tn = _pick(n, 1024)
    per_b = seq // tm
    has_small = w_small is not None
    in_specs = [
        pl.BlockSpec((tm, d), lambda i, j: (i, 0)),
        pl.BlockSpec((1, 3, d), lambda i, j: (i // per_b, 0, 0)),
        pl.BlockSpec((1, d), lambda i, j: (0, 0)),
        pl.BlockSpec((d, tn), lambda i, j: (0, j)),
    ]
    out_specs = [pl.BlockSpec((tm, tn), lambda i, j: (i, j))]
    out_shape = [jax.ShapeDtypeStruct((m, n), F32)]
    args = [x2, mod_s, g_pre.reshape(1, d), w]
    if has_small:
        ns = w_small.shape[1]
        in_specs.append(pl.BlockSpec((d, ns), lambda i, j: (0, 0)))
        out_specs.append(pl.BlockSpec((tm, ns), lambda i, j: (i, 0)))
        out_shape.append(jax.ShapeDtypeStruct((m, ns), F32))
        args.append(w_small)
    out = pl.pallas_call(
        functools.partial(_inproj_kernel, has_small=has_small),
        grid=(m // tm, n // tn),
        in_specs=in_specs,
        out_specs=out_specs,
        out_shape=out_shape,
        scratch_shapes=[pltpu.VMEM((tm, d), BF16)],
        compiler_params=_cparams("arbitrary", "arbitrary"),
        name="mixer_inproj",
    )(*args)
    return out if has_small else out[0]


def _outproj_kernel(*refs, n_in, res_w):
    a_refs = refs[:n_in]
    w_refs = refs[n_in:2 * n_in]
    x_ref, mod_ref, gpost_ref, o_ref = refs[2 * n_in:]
    f = _dot(a_refs[0][...], w_refs[0][...])
    for a_ref, w_ref in zip(a_refs[1:], w_refs[1:]):
        f = f + _dot(a_ref[...], w_ref[...])
    o_ref[...] = _post_residual(x_ref[...], f, gpost_ref[...], mod_ref, res_w)


def _outproj(acts, w_out, x2, mod_s, g_post, seq, res_w):
    m, d = x2.shape
    tm = _pick(seq, 512)
    per_b = seq // tm
    n_in = len(acts)
    in_specs, args = [], []
    for a in acts:
        in_specs.append(pl.BlockSpec((tm, a.shape[1]), lambda i: (i, 0)))
        args.append(a)
    row = 0
    for a in acts:
        wi = a.shape[1]
        assert row % wi == 0
        in_specs.append(pl.BlockSpec((wi, d), lambda i, r=row // wi: (r, 0)))
        args.append(w_out)
        row += wi
    in_specs += [
        pl.BlockSpec((tm, d), lambda i: (i, 0)),
        pl.BlockSpec((1, 3, d), lambda i: (i // per_b, 0, 0)),
        pl.BlockSpec((1, d), lambda i: (0, 0)),
    ]
    args += [x2, mod_s, g_post.reshape(1, d)]
    return pl.pallas_call(
        functools.partial(_outproj_kernel, n_in=n_in, res_w=res_w),
        grid=(m // tm,),
        in_specs=in_specs,
        out_specs=pl.BlockSpec((tm, d), lambda i: (i, 0)),
        out_shape=jax.ShapeDtypeStruct((m, d), F32),
        compiler_params=_cparams("arbitrary"),
        name="mixer_outproj",
    )(*args)


def _conv_from_scratch(cs_ref, w_ref, rows, lo, hi):
    acc = cs_ref[SUBLANES:SUBLANES + rows, lo:hi] * w_ref[CONV_WIDTH - 1:CONV_WIDTH, lo:hi]
    for back in range(1, CONV_WIDTH):
        tap = CONV_WIDTH - 1 - back
        acc = acc + cs_ref[SUBLANES - back:SUBLANES - back + rows, lo:hi] * w_ref[tap:tap + 1, lo:hi]
    return acc


GDN_BLOCK = 256
GDN_PACK = 4


def _tile_rows(x, n):
    return jnp.concatenate([x] * n, axis=0)


def _gdn_kernel(q_ref, k_ref, v_ref, z_ref, sm_ref, cw_ref, alog_ref, dtb_ref, onorm_ref,
                o_ref, s_scr, cs_scr, qn_scr, kn_scr, kb_scr, vb_scr, kbe_scr, kst_scr, egq_scr,
                a_scr, qk_scr, xo_scr, l_scr, ku_scr, oin_scr, *, n_heads, rows):
    t = pl.program_id(1)
    dk = GDN_HEAD_DIM
    gw = n_heads * dk
    n_chunks = rows // CHUNK
    n_groups = n_heads // GDN_PACK
    pw = GDN_PACK * CHUNK
    gk = GDN_PACK * dk

    @pl.when(t == 0)
    def _():
        s_scr[...] = jnp.zeros_like(s_scr)
        cs_scr[0:SUBLANES, :] = jnp.zeros((SUBLANES, 3 * gw), F32)

    cs_scr[SUBLANES:SUBLANES + rows, 0:gw] = q_ref[0]
    cs_scr[SUBLANES:SUBLANES + rows, gw:2 * gw] = k_ref[0]
    cs_scr[SUBLANES:SUBLANES + rows, 2 * gw:3 * gw] = v_ref[0]

    sm = sm_ref[0]
    beta_all = _sigmoid(sm)
    g_all = -jnp.exp(alog_ref[...]) * _softplus(sm + dtb_ref[...])
    in_chunk = lax.broadcasted_iota(jnp.int32, (rows, LANES), 0) & (CHUNK - 1)
    d = 1
    while d < CHUNK:
        g_all = g_all + jnp.where(in_chunk >= d, pltpu.roll(g_all, d, 0), 0.0)
        d *= 2
    eg_all = jnp.exp(g_all)
    glast_all = jnp.concatenate(
        [jnp.broadcast_to(g_all[(c + 1) * CHUNK - 1:(c + 1) * CHUNK, :], (CHUNK, LANES))
         for c in range(n_chunks)], axis=0)
    ekl_all = jnp.exp(glast_all - g_all)

    for h in range(n_heads):
        lo, hi = h * dk, (h + 1) * dk
        bcast = lambda arr, idx: jnp.broadcast_to(arr[:, idx:idx + 1], (rows, dk))
        beta_b = bcast(beta_all, h)
        eg_b = bcast(eg_all, n_heads + h)
        q = _silu(_conv_from_scratch(cs_scr, cw_ref, rows, lo, hi))
        qn = (q * lax.rsqrt(jnp.sum(q * q, axis=-1, keepdims=True) + EPS)) * dk ** -0.5
        qn_scr[:, lo:hi] = qn.astype(BF16)
        egq_scr[:, lo:hi] = qn * eg_b
        k = _silu(_conv_from_scratch(cs_scr, cw_ref, rows, gw + lo, gw + hi))
        kn = k * lax.rsqrt(jnp.sum(k * k, axis=-1, keepdims=True) + EPS)
        kb = kn * beta_b
        kn_scr[:, lo:hi] = kn.astype(BF16)
        kb_scr[:, lo:hi] = kb.astype(BF16)
        kbe_scr[:, lo:hi] = kb * eg_b
        kst_scr[:, lo:hi] = (kn * bcast(ekl_all, n_heads + h)).astype(BF16)
        v = _silu(_conv_from_scratch(cs_scr, cw_ref, rows, 2 * gw + lo, 2 * gw + hi))
        vb_scr[:, lo:hi] = v * beta_b
    cs_scr[0:SUBLANES, :] = cs_scr[rows:rows + SUBLANES, :]

    ri = lax.broadcasted_iota(jnp.int32, (CHUNK, pw), 0)
    li = lax.broadcasted_iota(jnp.int32, (CHUNK, pw), 1)
    cj = li & (CHUNK - 1)
    lh = li // CHUNK
    causal = ri >= cj
    strict = ri > cj
    eye = ri == cj
    bd_r = lax.broadcasted_iota(jnp.int32, (pw, pw), 0) // CHUNK
    bd_c = lax.broadcasted_iota(jnp.int32, (pw, pw), 1) // CHUNK
    bd_sq = bd_r == bd_c
    bk_r = lax.broadcasted_iota(jnp.int32, (pw, gk), 0) // CHUNK
    bk_c = lax.broadcasted_iota(jnp.int32, (pw, gk), 1) // dk
    bd_k = bk_r == bk_c
    zero_sq = jnp.zeros((pw, pw), BF16)
    zero_k = jnp.zeros((pw, gk), BF16)

    def block_diag(x_rp):
        return jnp.where(bd_sq, _tile_rows(x_rp.astype(BF16), GDN_PACK), zero_sq)

    def block_diag_k(x_cat):
        return jnp.where(bd_k, _tile_rows(x_cat, GDN_PACK), zero_k)

    probs = [(c, gi) for c in range(n_chunks) for gi in range(n_groups)]
    for p, (c, gi) in enumerate(probs):
        r0, r1 = c * CHUNK, (c + 1) * CHUNK
        c0 = gi * gk
        gcs = [jnp.broadcast_to(g_all[r0:r1, n_heads + gi * GDN_PACK + s:n_heads + gi * GDN_PACK + s + 1],
                                (CHUNK, pw)) for s in range(GDN_PACK)]
        gc = gcs[GDN_PACK - 1]
        for s in range(GDN_PACK - 2, -1, -1):
            gc = jnp.where(lh == s, gcs[s], gc)
        gr = jnp.sum(jnp.where(eye, gc, 0.0), axis=0, keepdims=True)
        decay = jnp.exp(jnp.where(causal, gc - gr, -jnp.inf))
        lhs = jnp.concatenate([kb_scr[r0:r1, c0:c0 + gk], qn_scr[r0:r1, c0:c0 + gk]], axis=0)
        kq = _dot_nt(lhs, block_diag_k(kn_scr[r0:r1, c0:c0 + gk]))
        a = jnp.where(strict, kq[0:CHUNK] * decay, 0.0)
        a_scr[p] = a
        qk_scr[p] = jnp.where(causal, kq[CHUNK:2 * CHUNK] * decay, 0.0)
        xo_scr[p] = -jnp.where((ri - cj == 1) & ((ri & 1) == 1), a, 0.0)

    s = 2
    while s < CHUNK:
        rb = ri // s
        off = ((rb & 1) == 1) & ((cj // s) == rb - 1)
        for p in range(len(probs)):
            a_off = jnp.where(off, a_scr[p], 0.0)
            xo = xo_scr[p]
            y = a_off + _dot(xo.astype(BF16), block_diag(a_off))
            xo_scr[p] = xo - (y + _dot(y.astype(BF16), block_diag(xo)))
        s *= 2

    for p, (c, gi) in enumerate(probs):
        r0, r1 = c * CHUNK, (c + 1) * CHUNK
        rhs = jnp.concatenate(
            [jnp.concatenate([vb_scr[r0:r1, h * dk:(h + 1) * dk], kbe_scr[r0:r1, h * dk:(h + 1) * dk]], axis=1)
             for h in range(gi * GDN_PACK, (gi + 1) * GDN_PACK)], axis=0)
        sol = rhs + _dot(block_diag(xo_scr[p]), rhs.astype(BF16))
        sol_b = sol.astype(BF16)
        qks = _dot(block_diag(qk_scr[p]), sol_b)
        ks = _dot_tn(block_diag_k(kst_scr[r0:r1, gi * gk:(gi + 1) * gk]), sol_b)
        for s_ in range(GDN_PACK):
            h = gi * GDN_PACK + s_
            half = (h % 2) * dk
            rr = slice(s_ * CHUNK, (s_ + 1) * CHUNK)
            kr = slice(s_ * dk, (s_ + 1) * dk)
            q_eff = egq_scr[r0:r1, h * dk:(h + 1) * dk] - qks[rr, dk:2 * dk]
            l_scr[c, h // 2, 0:dk, half:half + dk] = ks[kr, dk:2 * dk].astype(BF16)
            l_scr[c, h // 2, dk:dk + CHUNK, half:half + dk] = q_eff.astype(BF16)
            ku_scr[c, h] = ks[kr, 0:dk]
            oin_scr[r0:r1, h * dk:(h + 1) * dk] = qks[rr, 0:dk]

    zero = jnp.zeros((dk, dk), BF16)
    for c in range(n_chunks):
        r0, r1 = c * CHUNK, (c + 1) * CHUNK
        for pr in range(n_heads // 2):
            h0, h1 = 2 * pr, 2 * pr + 1
            s0 = s_scr[h0]
            s1 = s_scr[h1]
            s_bd = jnp.concatenate(
                [jnp.concatenate([s0.astype(BF16), zero], axis=1),
                 jnp.concatenate([zero, s1.astype(BF16)], axis=1)], axis=0)
            r = _dot(l_scr[c, pr], s_bd)
            for h, st, off_ in ((h0, s0, 0), (h1, s1, dk)):
                cd = eg_all[r1 - 1:r1, n_heads + h:n_heads + h + 1]
                s_scr[h] = st * cd - r[0:dk, off_:off_ + dk] + ku_scr[c, h]
                o = r[dk:dk + CHUNK, off_:off_ + dk] + oin_scr[r0:r1, h * dk:(h + 1) * dk]
                out = _rms(o, onorm_ref[...]) * _silu(z_ref[0, r0:r1, h * dk:(h + 1) * dk])
                o_ref[0, r0:r1, h * dk:(h + 1) * dk] = out.astype(o_ref.dtype)


def _gdn(proj3, small3, conv_w, a_log, dt_bias, o_norm):
    b, seq, _ = proj3.shape
    n_heads = a_log.shape[0]
    assert n_heads % GDN_PACK == 0 and GDN_HEAD_DIM == LANES
    gw = n_heads * GDN_HEAD_DIM
    rows = _pick(seq, GDN_BLOCK)
    n_chunks = rows // CHUNK
    n_prob = n_chunks * (n_heads // GDN_PACK)
    pad = LANES - 2 * n_heads
    alog_row = jnp.pad(a_log, (n_heads, pad)).reshape(1, LANES)
    dtb_row = jnp.pad(dt_bias, (n_heads, pad)).reshape(1, LANES)
    col = lambda c: pl.BlockSpec((1, rows, gw), lambda i, t, c=c: (i, t, c))
    return pl.pallas_call(
        functools.partial(_gdn_kernel, n_heads=n_heads, rows=rows),
        grid=(b, seq // rows),
        in_specs=[
            col(0), col(1), col(2), col(3),
            pl.BlockSpec((1, rows, LANES), lambda i, t: (i, t, 0)),
            pl.BlockSpec((CONV_WIDTH, 3 * gw), lambda i, t: (0, 0)),
            pl.BlockSpec((1, LANES), lambda i, t: (0, 0)),
            pl.BlockSpec((1, LANES), lambda i, t: (0, 0)),
            pl.BlockSpec((1, GDN_HEAD_DIM), lambda i, t: (0, 0)),
        ],
        out_specs=pl.BlockSpec((1, rows, gw), lambda i, t: (i, t, 0)),
        out_shape=jax.ShapeDtypeStruct((b, seq, gw), BF16),
        scratch_shapes=[
            pltpu.VMEM((n_heads, GDN_HEAD_DIM, GDN_HEAD_DIM), F32),
            pltpu.VMEM((rows + SUBLANES, 3 * gw), F32),
            pltpu.VMEM((rows, gw), BF16),
            pltpu.VMEM((rows, gw), BF16),
            pltpu.VMEM((rows, gw), BF16),
            pltpu.VMEM((rows, gw), F32),
            pltpu.VMEM((rows, gw), F32),
            pltpu.VMEM((rows, gw), BF16),
            pltpu.VMEM((rows, gw), F32),
            pltpu.VMEM((n_prob, CHUNK, GDN_PACK * CHUNK), F32),
            pltpu.VMEM((n_prob, CHUNK, GDN_PACK * CHUNK), F32),
            pltpu.VMEM((n_prob, CHUNK, GDN_PACK * CHUNK), F32),
            pltpu.VMEM((n_chunks, n_heads // 2, GDN_HEAD_DIM + CHUNK, 2 * GDN_HEAD_DIM), BF16),
            pltpu.VMEM((n_chunks, n_heads, GDN_HEAD_DIM, GDN_HEAD_DIM), F32),
            pltpu.VMEM((rows, gw), F32),
        ],
        compiler_params=_cparams("arbitrary", "arbitrary"),
        name="gdn",
    )(proj3, proj3, proj3, proj3, small3, conv_w, alog_row, dtb_row,
      o_norm.reshape(1, GDN_HEAD_DIM))


def _rotary(x, cos, sin):
    half = x.shape[-1] // 2
    x1, x2 = x[:, :half], x[:, half:]
    return jnp.concatenate([x1 * cos - x2 * sin, x2 * cos + x1 * sin], axis=-1)


def _ret_kernel(q_ref, k_ref, v_ref, g_ref, cos_ref, sin_ref, norm_ref, o_ref, s_scr,
                *, n_heads, rows):
    t = pl.program_id(1)
    dk = RET_HEAD_DIM

    @pl.when(t == 0)
    def _():
        s_scr[...] = jnp.zeros_like(s_scr)

    cos, sin = cos_ref[...], sin_ref[...]
    ri = lax.broadcasted_iota(jnp.int32, (rows, rows), 0)
    ci = lax.broadcasted_iota(jnp.int32, (rows, rows), 1)
    delta = (ri - ci).astype(F32)
    causal = ri >= ci
    pos = lax.broadcasted_iota(jnp.int32, (rows, 1), 0).astype(F32)

    for h in range(n_heads):
        lo, hi = h * dk, (h + 1) * dk
        log_gamma = math.log1p(-(2.0 ** (-5.0 - h)))
        q = _rotary(q_ref[0, :, lo:hi], cos, sin)
        k = _rotary(k_ref[0, :, lo:hi], cos, sin) * dk ** -0.5
        v_b = v_ref[0, :, lo:hi].astype(BF16)
        intra = jnp.exp(jnp.where(causal, delta * log_gamma, -jnp.inf))
        scores = _dot_nt(q.astype(BF16), k.astype(BF16)) * intra
        q_in = q * jnp.exp((pos + 1.0) * log_gamma)
        k_st = k * jnp.exp((rows - 1.0 - pos) * log_gamma)
        state = s_scr[h]
        o = _dot(scores.astype(BF16), v_b) + _dot(q_in.astype(BF16), state.astype(BF16))
        s_scr[h] = state * math.exp(rows * log_gamma) + _dot_tn(k_st.astype(BF16), v_b)
        out = _rms(o, norm_ref[:, lo:hi]) * _silu(g_ref[0, :, lo:hi])
        o_ref[0, :, lo:hi] = out.astype(o_ref.dtype)


def _retention(proj3, ret_norm, col0):
    b, seq, _ = proj3.shape
    rw = ret_norm.shape[0]
    n_heads = rw // RET_HEAD_DIM
    rows = _pick(seq, 256)
    half = RET_HEAD_DIM // 2
    inv_freq = ROPE_BASE ** (-jnp.arange(half, dtype=F32) / half)
    ang = jnp.arange(seq, dtype=F32)[:, None] * inv_freq[None, :]
    cos, sin = jnp.cos(ang), jnp.sin(ang)
    assert col0 % rw == 0
    c0 = col0 // rw
    col = lambda c: pl.BlockSpec((1, rows, rw), lambda i, t, c=c: (i, t, c0 + c))
    return pl.pallas_call(
        functools.partial(_ret_kernel, n_heads=n_heads, rows=rows),
        grid=(b, seq // rows),
        in_specs=[
            col(0), col(1), col(2), col(3),
            pl.BlockSpec((rows, half), lambda i, t: (t, 0)),
            pl.BlockSpec((rows, half), lambda i, t: (t, 0)),
            pl.BlockSpec((1, rw), lambda i, t: (0, 0)),
        ],
        out_specs=pl.BlockSpec((1, rows, rw), lambda i, t: (i, t, 0)),
        out_shape=jax.ShapeDtypeStruct((b, seq, rw), BF16),
        scratch_shapes=[pltpu.VMEM((n_heads, RET_HEAD_DIM, RET_HEAD_DIM), F32)],
        compiler_params=_cparams("arbitrary", "arbitrary"),
        name="retention",
    )(proj3, proj3, proj3, proj3, cos, sin, ret_norm.reshape(1, rw))


LRU_ROWS = 256


def _lru_block(p_ref, p_prev_ref, seq_start, between, perm_ref, cw_ref, cb_ref, wa_ref, ba_ref, wx_ref,
               bx_ref, lam_ref, o_ref, row0, h_scr, act_scr, *, n_blocks, rows, bw):
    width = n_blocks * bw
    nj = rows // SUBLANES
    sub = lax.broadcasted_iota(jnp.int32, (SUBLANES, bw), 0)
    tails = [jnp.where(seq_start, 0.0, p_prev_ref[rows - SUBLANES * (k - 1) - 1:rows - SUBLANES * (k - 1),
                                                  width:2 * width]) for k in range(1, CONV_WIDTH)]

    for n in range(n_blocks):
        between(n)
        lo, hi = n * bw, (n + 1) * bw
        xp = p_ref[:, width + lo:width + hi]

        def edge(k):
            prev = pltpu.roll(xp[rows - k * SUBLANES:rows - (k - 1) * SUBLANES], 1, 0)
            return jnp.where(sub == 0, tails[k - 1][:, lo:hi], prev)

        edges = [edge(k) for k in range(1, CONV_WIDTH)]
        xc = xp * cw_ref[CONV_WIDTH - 1:CONV_WIDTH, lo:hi]
        for back in range(1, CONV_WIDTH):
            tap = CONV_WIDTH - 1 - back
            shifted = jnp.concatenate(edges[:back][::-1] + [xp[:rows - back * SUBLANES]], axis=0)
            xc = xc + shifted * cw_ref[tap:tap + 1, lo:hi]
        xc = xc + cb_ref[:, lo:hi]
        xc_b = xc.astype(BF16)
        r = _sigmoid(_dot(xc_b, wa_ref[n]) + ba_ref[:, lo:hi])
        i = _sigmoid(_dot(xc_b, wx_ref[n]) + bx_ref[:, lo:hi])
        log_a = (-LRU_C * r) * _softplus(-lam_ref[:, lo:hi])
        a = jnp.exp(log_a)
        mult = jnp.sqrt(-jnp.tanh(log_a) * (a * a + 1.0))
        bv = mult * (i * xc)

        h = bv[0:SUBLANES]
        ac = a[0:SUBLANES]
        hl, al = [h], [ac]
        for j in range(1, nj):
            aj = a[j * SUBLANES:(j + 1) * SUBLANES]
            h = aj * h + bv[j * SUBLANES:(j + 1) * SUBLANES]
            ac = aj * ac
            hl.append(h)
            al.append(ac)
        c = jnp.where(seq_start, 0.0, h_scr[0:1, lo:hi])
        cs = []
        for s in range(SUBLANES):
            cs.append(c)
            c = h[s:s + 1] + ac[s:s + 1] * c
        h_scr[0:1, lo:hi] = c
        c_in = jnp.concatenate(cs, axis=0)
        hs = jnp.concatenate([hl[j] + al[j] * c_in for j in range(nj)], axis=0)
        act_scr[:, lo:hi] = (hs * _gelu_tanh(p_ref[:, lo:hi])).astype(BF16)

    o_ref[row0:row0 + rows, :] = _dot_tn(perm_ref[...], act_scr[...]).astype(o_ref.dtype)


def _lru_in_kernel(x0_ref, mod0_ref, xa_ref, moda_ref, xb_ref, modb_ref, gpre_ref, win_ref, perm_ref,
                   cw_ref, cb_ref, wa_ref, ba_ref, wx_ref, bx_ref, lam_ref, o_ref,
                   p0_scr, p1_scr, h_scr, act_scr, *, n_blocks, rows, bw, blocks_per_seq):
    s = pl.program_id(0)

    def in_proj_cols(h, p_ref, n):
        cw = win_ref.shape[1] // n_blocks
        p_ref[:, n * cw:(n + 1) * cw] = _dot(h, win_ref[:, n * cw:(n + 1) * cw])

    def normed(x_ref, mod_ref):
        h = _norm_mod(x_ref[...], gpre_ref[...], mod_ref).astype(BF16)
        return _dot(perm_ref[...], h).astype(BF16)

    @pl.when(s == 0)
    def _():
        h0 = normed(x0_ref, mod0_ref)
        for n in range(n_blocks):
            in_proj_cols(h0, p0_scr, n)
        p1_scr[...] = jnp.zeros_like(p1_scr)
        h_scr[...] = jnp.zeros_like(h_scr)

    lru = functools.partial(_lru_block, perm_ref=perm_ref, cw_ref=cw_ref, cb_ref=cb_ref, wa_ref=wa_ref,
                            ba_ref=ba_ref, wx_ref=wx_ref, bx_ref=bx_ref, lam_ref=lam_ref, o_ref=o_ref,
                            h_scr=h_scr, act_scr=act_scr, n_blocks=n_blocks, rows=rows, bw=bw)
    seq_start = (2 * s) % blocks_per_seq == 0
    ha = normed(xa_ref, moda_ref)
    lru(p0_scr, p1_scr, seq_start, lambda n: in_proj_cols(ha, p1_scr, n), row0=0)
    hb = normed(xb_ref, modb_ref)
    lru(p1_scr, p0_scr, False, lambda n: in_proj_cols(hb, p0_scr, n), row0=rows)


def _lru_in(x2, mod_s, g_pre, w_in, conv_w, conv_b, gate_a_w, gate_a_b, gate_x_w, gate_x_b, lam, seq):
    m, d = x2.shape
    width = w_in.shape[1] // 2
    n_blocks, bw, _ = gate_a_w.shape
    rows = _pick(seq, LRU_ROWS)
    bps = seq // rows
    nb = m // rows
    assert bps % 2 == 0 and nb % 2 == 0
    last = nb - 1
    r_idx = jnp.arange(rows)
    t_of_r = (r_idx % SUBLANES) * (rows // SUBLANES) + r_idx // SUBLANES
    perm = (t_of_r[:, None] == r_idx[None, :]).astype(BF16)
    xa = lambda s: 2 * s + 1
    xb = lambda s: jnp.minimum(2 * s + 2, last)
    vec = lambda: pl.BlockSpec((1, width), lambda s: (0, 0))
    gate = lambda: pl.BlockSpec((n_blocks, bw, bw), lambda s: (0, 0, 0))
    return pl.pallas_call(
        functools.partial(_lru_in_kernel, n_blocks=n_blocks, rows=rows, bw=bw, blocks_per_seq=bps),
        grid=(nb // 2,),
        in_specs=[
            pl.BlockSpec((rows, d), lambda s: (0, 0)),
            pl.BlockSpec((1, 3, d), lambda s: (0, 0, 0)),
            pl.BlockSpec((rows, d), lambda s: (xa(s), 0)),
            pl.BlockSpec((1, 3, d), lambda s: (xa(s) // bps, 0, 0)),
            pl.BlockSpec((rows, d), lambda s: (xb(s), 0)),
            pl.BlockSpec((1, 3, d), lambda s: (xb(s) // bps, 0, 0)),
            pl.BlockSpec((1, d), lambda s: (0, 0)),
            pl.BlockSpec((d, 2 * width), lambda s: (0, 0)),
            pl.BlockSpec((rows, rows), lambda s: (0, 0)),
            pl.BlockSpec((CONV_WIDTH, width), lambda s: (0, 0)),
            vec(), gate(), vec(), gate(), vec(), vec(),
        ],
        out_specs=pl.BlockSpec((2 * rows, width), lambda s: (s, 0)),
        out_shape=jax.ShapeDtypeStruct((m, width), BF16),
        scratch_shapes=[
            pltpu.VMEM((rows, 2 * width), F32),
            pltpu.VMEM((rows, 2 * width), F32),
            pltpu.VMEM((SUBLANES, width), F32),
            pltpu.VMEM((rows, width), BF16),
        ],
        compiler_params=_cparams("arbitrary"),
        name="rglru_in",
    )(x2, mod_s, x2, mod_s, x2, mod_s, g_pre.reshape(1, d), w_in, perm, conv_w, conv_b.reshape(1, width),
      gate_a_w.astype(BF16), gate_a_b.reshape(1, width), gate_x_w.astype(BF16),
      gate_x_b.reshape(1, width), lam.reshape(1, width))


def kernel(x, c, ada_w, ada_b, norm_pre, norm_post, ffn_w13, ffn_w2, ev_w_in, ev_conv_w, ev_a_log,
           ev_dt_bias, ev_o_norm, ev_ret_norm, ev_w_out, od_w_in, od_conv_w, od_conv_b,
           od_gate_a_w, od_gate_a_b, od_gate_x_w, od_gate_x_b, od_lambda, od_w_out):
    b, seq, d = x.shape
    depth = ada_w.shape[0]
    m = b * seq
    mod = _ada(c, ada_w, ada_b).reshape(depth, b, N_SUB, 3, d)
    x2 = x.reshape(m, d)
    w13_b = ffn_w13.astype(BF16)
    w2_b = ffn_w2.astype(BF16)

    for layer in range(depth):
        mod_l = mod[layer]
        x2 = _ffn(x2, mod_l[:, 0], norm_pre[layer, 0], norm_post[layer, 0], w13_b, w2_b, layer, 0, seq, 0.5)
        if layer % 2 == 0:
            e = layer // 2
            n_heads = ev_a_log.shape[1]
            gw = n_heads * GDN_HEAD_DIM
            w_in = ev_w_in[e]
            w_main = jnp.concatenate([w_in[:, :3 * gw], w_in[:, 3 * gw + 2 * n_heads:]], axis=1)
            w_small = jnp.pad(w_in[:, 3 * gw:3 * gw + 2 * n_heads],
                              ((0, 0), (0, LANES - 2 * n_heads)))
            proj, small = _inproj(x2, mod_l[:, 1], norm_pre[layer, 1], w_main.astype(BF16), seq,
                                  w_small.astype(BF16))
            proj3 = proj.reshape(b, seq, -1)
            o_a = _gdn(proj3, small.reshape(b, seq, LANES), ev_conv_w[e], ev_a_log[e],
                       ev_dt_bias[e], ev_o_norm[e])
            o_b = _retention(proj3, ev_ret_norm[e], 4 * gw)
            acts = [o_a.reshape(m, -1), o_b.reshape(m, -1)]
            w_out = ev_w_out[e]
        else:
            o = layer // 2
            hs = _lru_in(x2, mod_l[:, 1], norm_pre[layer, 1], od_w_in[o].astype(BF16), od_conv_w[o],
                         od_conv_b[o], od_gate_a_w[o], od_gate_a_b[o], od_gate_x_w[o], od_gate_x_b[o],
                         od_lambda[o], seq)
            acts = [hs.reshape(m, -1)]
            w_out = od_w_out[o]
        x2 = _outproj(acts, w_out.astype(BF16), x2, mod_l[:, 1], norm_post[layer, 1], seq, 1.0)
        x2 = _ffn(x2, mod_l[:, 2], norm_pre[layer, 2], norm_post[layer, 2], w13_b, w2_b, layer, 1, seq, 0.5)
    return x2.reshape(b, seq, d)
```

```python
import functools
import math

import jax
import jax.numpy as jnp
from jax import lax
from jax.experimental import pallas as pl
from jax.experimental.pallas import tpu as pltpu

F32 = jnp.float32
BF16 = jnp.bfloat16

EPS = 1e-6
GDN_HEAD_DIM = 128
RET_HEAD_DIM = 256
CHUNK = 64
CONV_WIDTH = 4
ROPE_BASE = 10000.0
LRU_C = 8.0
N_SUB = 3

SUBLANES = 8
LANES = 128
V7X_VMEM_LIMIT_BYTES = 56 * 1024 * 1024


def _cparams(*semantics):
    return pltpu.CompilerParams(dimension_semantics=semantics,
                                vmem_limit_bytes=V7X_VMEM_LIMIT_BYTES)


def _dot(a, b):
    return jnp.dot(a, b, preferred_element_type=F32)


def _dot_nt(a, b):
    return lax.dot_general(a, b, (((1,), (1,)), ((), ())), preferred_element_type=F32)


def _dot_tn(a, b):
    return lax.dot_general(a, b, (((0,), (0,)), ((), ())), preferred_element_type=F32)


def _sigmoid(x):
    return jax.nn.sigmoid(x)


def _silu(x):
    return x * jax.nn.sigmoid(x)


def _softplus(x):
    return jnp.maximum(x, 0.0) + jnp.log1p(jnp.exp(-jnp.abs(x)))


def _gelu_tanh(x):
    c = math.sqrt(2.0 / math.pi)
    return 0.5 * x * (1.0 + jnp.tanh(c * (x + 0.044715 * (x * x * x))))


def _rms(x, gain):
    return x * lax.rsqrt(jnp.mean(x * x, axis=-1, keepdims=True) + EPS) * gain


def _inv_rms(x):
    return lax.rsqrt(jnp.mean(x * x, axis=-1, keepdims=True) + EPS)


def _norm_mod(x, g_pre, mod_ref):
    return (x * _inv_rms(x)) * (g_pre * (1.0 + mod_ref[0, 1:2, :])) + mod_ref[0, 0:1, :]


def _post_residual(x, f, g_post, mod_ref, res_w):
    return x + (f * _inv_rms(f)) * (g_post * (res_w * (1.0 + mod_ref[0, 2:3, :])))


def _pick(n, pref):
    if n <= pref:
        return n
    for t in range(pref - pref % LANES, 0, -LANES):
        if n % t == 0:
            return t
    raise ValueError(f"no lane-aligned tile of {n} at or below {pref}")


def _ada_kernel(c_ref, w_ref, b_ref, o_ref):
    a = _silu(c_ref[...]).astype(BF16)
    o_ref[0] = _dot(a, w_ref[0].astype(BF16)) + b_ref[0]


def _ada(c, ada_w, ada_b):
    n_layers, d, n = ada_w.shape
    b = c.shape[0]
    tn = _pick(n, 1024)
    return pl.pallas_call(
        _ada_kernel,
        grid=(n_layers, n // tn),
        in_specs=[
            pl.BlockSpec((b, d), lambda l, j: (0, 0)),
            pl.BlockSpec((1, d, tn), lambda l, j: (l, 0, j)),
            pl.BlockSpec((1, 1, tn), lambda l, j: (l, 0, j)),
        ],
        out_specs=pl.BlockSpec((1, b, tn), lambda l, j: (l, 0, j)),
        out_shape=jax.ShapeDtypeStruct((n_layers, b, n), F32),
        compiler_params=_cparams("arbitrary", "arbitrary"),
        name="ada_mod",
    )(c, ada_w, ada_b.reshape(n_layers, 1, n))


def _ffn_kernel(x_ref, mod_ref, gpre_ref, gpost_ref, w1_ref, w3_ref, w2_ref, o_ref,
                h_scr, acc_scr, *, res_w, n_f):
    j = pl.program_id(1)

    @pl.when(j == 0)
    def _():
        h_scr[...] = _norm_mod(x_ref[...], gpre_ref[...], mod_ref).astype(BF16)
        acc_scr[...] = jnp.zeros_like(acc_scr)

    h = h_scr[...]
    g = _dot(h, w1_ref[...])
    u = _dot(h, w3_ref[...])
    a = (_silu(g) * u).astype(BF16)
    acc_scr[...] += _dot(a, w2_ref[...])

    @pl.when(j == n_f - 1)
    def _():
        o_ref[...] = _post_residual(x_ref[...], acc_scr[...], gpost_ref[...], mod_ref, res_w)


def _ffn(x2, mod_s, g_pre, g_post, w13_all, w2_all, layer, which, seq, res_w):
    m, d = x2.shape
    f = w2_all.shape[2]
    tm = _pick(seq, 512)
    tf = _pick(f, 512)
    n_f = f // tf
    per_b = seq // tm
    return pl.pallas_call(
        functools.partial(_ffn_kernel, res_w=res_w, n_f=n_f),
        grid=(m // tm, n_f),
        in_specs=[
            pl.BlockSpec((tm, d), lambda i, j: (i, 0)),
            pl.BlockSpec((1, 3, d), lambda i, j: (i // per_b, 0, 0)),
            pl.BlockSpec((1, d), lambda i, j: (0, 0)),
            pl.BlockSpec((1, d), lambda i, j: (0, 0)),
            pl.BlockSpec((None, None, d, tf), lambda i, j: (layer, which, 0, j)),
            pl.BlockSpec((None, None, d, tf), lambda i, j: (layer, which, 0, j + n_f)),
            pl.BlockSpec((None, None, tf, d), lambda i, j: (layer, which, j, 0)),
        ],
        out_specs=pl.BlockSpec((tm, d), lambda i, j: (i, 0)),
        out_shape=jax.ShapeDtypeStruct((m, d), F32),
        scratch_shapes=[pltpu.VMEM((tm, d), BF16), pltpu.VMEM((tm, d), F32)],
        compiler_params=_cparams("arbitrary", "arbitrary"),
        name="ffn",
    )(x2, mod_s, g_pre.reshape(1, d), g_post.reshape(1, d), w13_all, w13_all, w2_all)


def _inproj_kernel(x_ref, mod_ref, gpre_ref, w_ref, *rest, has_small):
    if has_small:
        ws_ref, o_ref, os_ref, h_scr = rest
    else:
        o_ref, h_scr = rest
    j = pl.program_id(1)

    @pl.when(j == 0)
    def _():
        h = _norm_mod(x_ref[...], gpre_ref[...], mod_ref).astype(BF16)
        h_scr[...] = h
        if has_small:
            os_ref[...] = _dot(h, ws_ref[...])

    o_ref[...] = _dot(h_scr[...], w_ref[...])


def _inproj(x2, mod_s, g_pre, w, seq, w_small=None):
    m, d = x2.shape
    n = w.shape[1]
    tm = _pick(seq, 1024)
    tn = _pick(n, 1024)
    per_b = seq // tm
    has_small = w_small is not None
    in_specs = [
        pl.BlockSpec((tm, d), lambda i, j: (i, 0)),
        pl.BlockSpec((1, 3, d), lambda i, j: (i // per_b, 0, 0)),
        pl.BlockSpec((1, d), lambda i, j: (0, 0)),
        pl.BlockSpec((d, tn), lambda i, j: (0, j)),
    ]
    out_specs = [pl.BlockSpec((tm, tn), lambda i, j: (i, j))]
    out_shape = [jax.ShapeDtypeStruct((m, n), F32)]
    args = [x2, mod_s, g_pre.reshape(1, d), w]
    if has_small:
        ns = w_small.shape[1]
        in_specs.append(pl.BlockSpec((d, ns), lambda i, j: (0, 0)))
        out_specs.append(pl.BlockSpec((tm, ns), lambda i, j: (i, 0)))
        out_shape.append(jax.ShapeDtypeStruct((m, ns), F32))
        args.append(w_small)
    out = pl.pallas_call(
        functools.partial(_inproj_kernel, has_small=has_small),
        grid=(m // tm, n // tn),
        in_specs=in_specs,
        out_specs=out_specs,
        out_shape=out_shape,
        scratch_shapes=[pltpu.VMEM((tm, d), BF16)],
        compiler_params=_cparams("arbitrary", "arbitrary"),
        name="mixer_inproj",
    )(*args)
    return out if has_small else out[0]


def _outproj_kernel(*refs, n_in, res_w):
    a_refs = refs[:n_in]
    w_refs = refs[n_in:2 * n_in]
    x_ref, mod_ref, gpost_ref, o_ref = refs[2 * n_in:]
    f = _dot(a_refs[0][...], w_refs[0][...])
    for a_ref, w_ref in zip(a_refs[1:], w_refs[1:]):
        f = f + _dot(a_ref[...], w_ref[...])
    o_ref[...] = _post_residual(x_ref[...], f, gpost_ref[...], mod_ref, res_w)


def _outproj(acts, w_out, x2, mod_s, g_post, seq, res_w):
    m, d = x2.shape
    tm = _pick(seq, 512)
    per_b = seq // tm
    n_in = len(acts)
    in_specs, args = [], []
    for a in acts:
        in_specs.append(pl.BlockSpec((tm, a.shape[1]), lambda i: (i, 0)))
        args.append(a)
    row = 0
    for a in acts:
        wi = a.shape[1]
        assert row % wi == 0
        in_specs.append(pl.BlockSpec((wi, d), lambda i, r=row // wi: (r, 0)))
        args.append(w_out)
        row += wi
    in_specs += [
        pl.BlockSpec((tm, d), lambda i: (i, 0)),
        pl.BlockSpec((1, 3, d), lambda i: (i // per_b, 0, 0)),
        pl.BlockSpec((1, d), lambda i: (0, 0)),
    ]
    args += [x2, mod_s, g_post.reshape(1, d)]
    return pl.pallas_call(
        functools.partial(_outproj_kernel, n_in=n_in, res_w=res_w),
        grid=(m // tm,),
        in_specs=in_specs,
        out_specs=pl.BlockSpec((tm, d), lambda i: (i, 0)),
        out_shape=jax.ShapeDtypeStruct((m, d), F32),
        compiler_params=_cparams("arbitrary"),
        name="mixer_outproj",
    )(*args)


def _conv_from_scratch(cs_ref, w_ref, rows, lo, hi):
    acc = cs_ref[SUBLANES:SUBLANES + rows, lo:hi] * w_ref[CONV_WIDTH - 1:CONV_WIDTH, lo:hi]
    for back in range(1, CONV_WIDTH):
        tap = CONV_WIDTH - 1 - back
        acc = acc + cs_ref[SUBLANES - back:SUBLANES - back + rows, lo:hi] * w_ref[tap:tap + 1, lo:hi]
    return acc


GDN_BLOCK = 256
GDN_PACK = 4


def _tile_rows(x, n):
    return jnp.concatenate([x] * n, axis=0)


def _gdn_block(p_ref, p_prev_ref, seq_start, between, cw_ref, alog_ref, dtb_ref, onorm_ref,
               o_ref, row0, s_scr, qn_scr, kn_scr, kb_scr, vb_scr, kbe_scr, kst_scr, egq_scr,
               a_scr, qk_scr, xo_scr, l_scr, ku_scr, oin_scr, *, n_heads, rows, n_pieces):
    dk = GDN_HEAD_DIM
    gw = n_heads * dk
    n_chunks = rows // CHUNK
    n_groups = n_heads // GDN_PACK
    pw = GDN_PACK * CHUNK
    gk = GDN_PACK * dk

    tail = p_prev_ref[rows:rows + SUBLANES, 0:3 * gw]
    p_ref[0:SUBLANES, 0:3 * gw] = jnp.where(seq_start, 0.0, tail)
    cs_scr = p_ref
    n_between = [0]

    def next_piece():
        between(n_between[0])
        n_between[0] += 1

    sm = p_ref[SUBLANES:SUBLANES + rows, 4 * gw:4 * gw + LANES]
    beta_all = _sigmoid(sm)
    g_all = -jnp.exp(alog_ref[...]) * _softplus(sm + dtb_ref[...])
    in_chunk = lax.broadcasted_iota(jnp.int32, (rows, LANES), 0) & (CHUNK - 1)
    d = 1
    while d < CHUNK:
        g_all = g_all + jnp.where(in_chunk >= d, pltpu.roll(g_all, d, 0), 0.0)
        d *= 2
    eg_all = jnp.exp(g_all)
    glast_all = jnp.concatenate(
        [jnp.broadcast_to(g_all[(c + 1) * CHUNK - 1:(c + 1) * CHUNK, :], (CHUNK, LANES))
         for c in range(n_chunks)], axis=0)
    ekl_all = jnp.exp(glast_all - g_all)

    for h in range(n_heads):
        next_piece()
        lo, hi = h * dk, (h + 1) * dk
        bcast = lambda arr, idx: jnp.broadcast_to(arr[:, idx:idx + 1], (rows, dk))
        beta_b = bcast(beta_all, h)
        eg_b = bcast(eg_all, n_heads + h)
        q = _silu(_conv_from_scratch(cs_scr, cw_ref, rows, lo, hi))
        qn = (q * lax.rsqrt(jnp.sum(q * q, axis=-1, keepdims=True) + EPS)) * dk ** -0.5
        qn_scr[:, lo:hi] = qn.astype(BF16)
        egq_scr[:, lo:hi] = qn * eg_b
        k = _silu(_conv_from_scratch(cs_scr, cw_ref, rows, gw + lo, gw + hi))
        kn = k * lax.rsqrt(jnp.sum(k * k, axis=-1, keepdims=True) + EPS)
        kb = kn * beta_b
        kn_scr[:, lo:hi] = kn.astype(BF16)
        kb_scr[:, lo:hi] = kb.astype(BF16)
        kbe_scr[:, lo:hi] = kb * eg_b
        kst_scr[:, lo:hi] = (kn * bcast(ekl_all, n_heads + h)).astype(BF16)
        v = _silu(_conv_from_scratch(cs_scr, cw_ref, rows, 2 * gw + lo, 2 * gw + hi))
        vb_scr[:, lo:hi] = v * beta_b

    ri = lax.broadcasted_iota(jnp.int32, (CHUNK, pw), 0)
    li = lax.broadcasted_iota(jnp.int32, (CHUNK, pw), 1)
    cj = li & (CHUNK - 1)
    lh = li // CHUNK
    causal = ri >= cj
    strict = ri > cj
    eye = ri == cj
    bd_r = lax.broadcasted_iota(jnp.int32, (pw, pw), 0) // CHUNK
    bd_c = lax.broadcasted_iota(jnp.int32, (pw, pw), 1) // CHUNK
    bd_sq = bd_r == bd_c
    bk_r = lax.broadcasted_iota(jnp.int32, (pw, gk), 0) // CHUNK
    bk_c = lax.broadcasted_iota(jnp.int32, (pw, gk), 1) // dk
    bd_k = bk_r == bk_c
    zero_sq = jnp.zeros((pw, pw), BF16)
    zero_k = jnp.zeros((pw, gk), BF16)

    def block_diag(x_rp):
        return jnp.where(bd_sq, _tile_rows(x_rp.astype(BF16), GDN_PACK), zero_sq)

    def block_diag_k(x_cat):
        return jnp.where(bd_k, _tile_rows(x_cat, GDN_PACK), zero_k)

    probs = [(c, gi) for c in range(n_chunks) for gi in range(n_groups)]
    for p, (c, gi) in enumerate(probs):
        next_piece()
        r0, r1 = c * CHUNK, (c + 1) * CHUNK
        c0 = gi * gk
        gcs = [jnp.broadcast_to(g_all[r0:r1, n_heads + gi * GDN_PACK + s:n_heads + gi * GDN_PACK + s + 1],
                                (CHUNK, pw)) for s in range(GDN_PACK)]
        gc = gcs[GDN_PACK - 1]
        for s in range(GDN_PACK - 2, -1, -1):
            gc = jnp.where(lh == s, gcs[s], gc)
        gr = jnp.sum(jnp.where(eye, gc, 0.0), axis=0, keepdims=True)
        decay = jnp.exp(jnp.where(causal, gc - gr, -jnp.inf))
        lhs = jnp.concatenate([kb_scr[r0:r1, c0:c0 + gk], qn_scr[r0:r1, c0:c0 + gk]], axis=0)
        kq = _dot_nt(lhs, block_diag_k(kn_scr[r0:r1, c0:c0 + gk]))
        a = jnp.where(strict, kq[0:CHUNK] * decay, 0.0)
        a_scr[p] = a
        qk_scr[p] = jnp.where(causal, kq[CHUNK:2 * CHUNK] * decay, 0.0)
        xo_scr[p] = -jnp.where((ri - cj == 1) & ((ri & 1) == 1), a, 0.0)

    next_piece()
    s = 2
    while s < CHUNK:
        rb = ri // s
        off = ((rb & 1) == 1) & ((cj // s) == rb - 1)
        for p in range(len(probs)):
            a_off = jnp.where(off, a_scr[p], 0.0)
            xo = xo_scr[p]
            y = a_off + _dot(xo.astype(BF16), block_diag(a_off))
            xo_scr[p] = xo - (y + _dot(y.astype(BF16), block_diag(xo)))
        s *= 2

    for p, (c, gi) in enumerate(probs):
        r0, r1 = c * CHUNK, (c + 1) * CHUNK
        rhs = jnp.concatenate(
            [jnp.concatenate([vb_scr[r0:r1, h * dk:(h + 1) * dk], kbe_scr[r0:r1, h * dk:(h + 1) * dk]], axis=1)
             for h in range(gi * GDN_PACK, (gi + 1) * GDN_PACK)], axis=0)
        sol = rhs + _dot(block_diag(xo_scr[p]), rhs.astype(BF16))
        sol_b = sol.astype(BF16)
        qks = _dot(block_diag(qk_scr[p]), sol_b)
        ks = _dot_tn(block_diag_k(kst_scr[r0:r1, gi * gk:(gi + 1) * gk]), sol_b)
        for s_ in range(GDN_PACK):
            h = gi * GDN_PACK + s_
            half = (h % 2) * dk
            rr = slice(s_ * CHUNK, (s_ + 1) * CHUNK)
            kr = slice(s_ * dk, (s_ + 1) * dk)
            q_eff = egq_scr[r0:r1, h * dk:(h + 1) * dk] - qks[rr, dk:2 * dk]
            l_scr[c, h // 2, 0:dk, half:half + dk] = ks[kr, dk:2 * dk].astype(BF16)
            l_scr[c, h // 2, dk:dk + CHUNK, half:half + dk] = q_eff.astype(BF16)
            ku_scr[c, h] = ks[kr, 0:dk]
            oin_scr[r0:r1, h * dk:(h + 1) * dk] = qks[rr, 0:dk]

    zero = jnp.zeros((dk, dk), BF16)
    for c in range(n_chunks):
        r0, r1 = c * CHUNK, (c + 1) * CHUNK
        for pr in range(n_heads // 2):
            h0, h1 = 2 * pr, 2 * pr + 1
            s0 = s_scr[h0]
            s1 = s_scr[h1]
            if c == 0:
                s0 = jnp.where(seq_start, 0.0, s0)
                s1 = jnp.where(seq_start, 0.0, s1)
            s_bd = jnp.concatenate(
                [jnp.concatenate([s0.astype(BF16), zero], axis=1),
                 jnp.concatenate([zero, s1.astype(BF16)], axis=1)], axis=0)
            r = _dot(l_scr[c, pr], s_bd)
            for h, st, off_ in ((h0, s0, 0), (h1, s1, dk)):
                cd = eg_all[r1 - 1:r1, n_heads + h:n_heads + h + 1]
                s_scr[h] = st * cd - r[0:dk, off_:off_ + dk] + ku_scr[c, h]
                o = r[dk:dk + CHUNK, off_:off_ + dk] + oin_scr[r0:r1, h * dk:(h + 1) * dk]
                z = p_ref[SUBLANES + r0:SUBLANES + r1, 3 * gw + h * dk:3 * gw + (h + 1) * dk]
                out = _rms(o, onorm_ref[...]) * _silu(z)
                o_ref[row0 + r0:row0 + r1, h * dk:(h + 1) * dk] = out.astype(o_ref.dtype)

    while n_between[0] < n_pieces:
        next_piece()


GDN_PIECES = 17


def _gdn_in_kernel(xa_ref, moda_ref, xb_ref, modb_ref, gpre_ref, win_ref,
                   cw_ref, alog_ref, dtb_ref, onorm_ref, o_ref, p0_scr, p1_scr, *scratch,
                   n_heads, rows, blocks_per_seq):
    s = pl.program_id(0)
    n_cols = win_ref.shape[1]
    piece = -(-n_cols // (GDN_PIECES * LANES)) * LANES
    bounds = [(c, min(c + piece, n_cols)) for c in range(0, n_cols, piece)]
    assert len(bounds) <= GDN_PIECES

    def in_proj_piece(h, p_ref, i):
        if i < len(bounds):
            c0, c1 = bounds[i]
            p_ref[SUBLANES:SUBLANES + rows, c0:c1] = _dot(h, win_ref[:, c0:c1])

    n_pieces = len(bounds)

    def normed(x_ref, mod_ref):
        return _norm_mod(x_ref[...], gpre_ref[...], mod_ref).astype(BF16)

    @pl.when(s == 0)
    def _():
        p0_scr[...] = jnp.zeros_like(p0_scr)
        p1_scr[...] = jnp.zeros_like(p1_scr)
        scratch[0][...] = jnp.zeros_like(scratch[0])

    seq_start = (2 * s - 2) % blocks_per_seq == 0
    ha = normed(xa_ref, moda_ref)
    _gdn_block(p0_scr, p1_scr, seq_start, lambda i: in_proj_piece(ha, p1_scr, i),
               cw_ref, alog_ref, dtb_ref, onorm_ref, o_ref, 0, *scratch, n_heads=n_heads, rows=rows,
               n_pieces=n_pieces)
    hb = normed(xb_ref, modb_ref)
    _gdn_block(p1_scr, p0_scr, False, lambda i: in_proj_piece(hb, p0_scr, i),
               cw_ref, alog_ref, dtb_ref, onorm_ref, o_ref, rows, *scratch, n_heads=n_heads, rows=rows,
               n_pieces=n_pieces)


def _gdn_in(x2, mod_s, g_pre, w_g, conv_w, a_log, dt_bias, o_norm, seq):
    m, d = x2.shape
    n_heads = a_log.shape[0]
    assert n_heads % GDN_PACK == 0 and GDN_HEAD_DIM == LANES
    gw = n_heads * GDN_HEAD_DIM
    n_cols = w_g.shape[1]
    assert n_cols == 4 * gw + LANES
    rows = _pick(seq, GDN_BLOCK)
    bps = seq // rows
    nb = m // rows
    assert bps % 2 == 0 and nb % 2 == 0
    last = nb - 1
    n_chunks = rows // CHUNK
    n_prob = n_chunks * (n_heads // GDN_PACK)
    pad = LANES - 2 * n_heads
    alog_row = jnp.pad(a_log, (n_heads, pad)).reshape(1, LANES)
    dtb_row = jnp.pad(dt_bias, (n_heads, pad)).reshape(1, LANES)
    xa = lambda s: jnp.maximum(2 * s - 1, 0)
    xb = lambda s: jnp.minimum(2 * s, last)
    return pl.pallas_call(
        functools.partial(_gdn_in_kernel, n_heads=n_heads, rows=rows, blocks_per_seq=bps),
        grid=(nb // 2 + 1,),
        in_specs=[
            pl.BlockSpec((rows, d), lambda s: (xa(s), 0)),
            pl.BlockSpec((1, 3, d), lambda s: (xa(s) // bps, 0, 0)),
            pl.BlockSpec((rows, d), lambda s: (xb(s), 0)),
            pl.BlockSpec((1, 3, d), lambda s: (xb(s) // bps, 0, 0)),
            pl.BlockSpec((1, d), lambda s: (0, 0)),
            pl.BlockSpec((d, n_cols), lambda s: (0, 0)),
            pl.BlockSpec((CONV_WIDTH, 3 * gw), lambda s: (0, 0)),
            pl.BlockSpec((1, LANES), lambda s: (0, 0)),
            pl.BlockSpec((1, LANES), lambda s: (0, 0)),
            pl.BlockSpec((1, GDN_HEAD_DIM), lambda s: (0, 0)),
        ],
        out_specs=pl.BlockSpec((2 * rows, gw), lambda s: (jnp.maximum(s - 1, 0), 0)),
        out_shape=jax.ShapeDtypeStruct((m, gw), BF16),
        scratch_shapes=[
            pltpu.VMEM((rows + SUBLANES, n_cols), F32),
            pltpu.VMEM((rows + SUBLANES, n_cols), F32),
            pltpu.VMEM((n_heads, GDN_HEAD_DIM, GDN_HEAD_DIM), F32),
            pltpu.VMEM((rows, gw), BF16),
            pltpu.VMEM((rows, gw), BF16),
            pltpu.VMEM((rows, gw), BF16),
            pltpu.VMEM((rows, gw), F32),
            pltpu.VMEM((rows, gw), F32),
            pltpu.VMEM((rows, gw), BF16),
            pltpu.VMEM((rows, gw), F32),
            pltpu.VMEM((n_prob, CHUNK, GDN_PACK * CHUNK), F32),
            pltpu.VMEM((n_prob, CHUNK, GDN_PACK * CHUNK), F32),
            pltpu.VMEM((n_prob, CHUNK, GDN_PACK * CHUNK), F32),
            pltpu.VMEM((n_chunks, n_heads // 2, GDN_HEAD_DIM + CHUNK, 2 * GDN_HEAD_DIM), BF16),
            pltpu.VMEM((n_chunks, n_heads, GDN_HEAD_DIM, GDN_HEAD_DIM), F32),
            pltpu.VMEM((rows, gw), F32),
        ],
        compiler_params=_cparams("arbitrary"),
        name="gdn_in",
    )(x2, mod_s, x2, mod_s, g_pre.reshape(1, d), w_g, conv_w, alog_row, dtb_row,
      o_norm.reshape(1, GDN_HEAD_DIM))


def _rotary(x, cos, sin):
    half = x.shape[-1] // 2
    x1, x2 = x[:, :half], x[:, half:]
    return jnp.concatenate([x1 * cos - x2 * sin, x2 * cos + x1 * sin], axis=-1)


def _ret_kernel(q_ref, k_ref, v_ref, g_ref, cos_ref, sin_ref, norm_ref, o_ref, s_scr,
                *, n_heads, rows):
    t = pl.program_id(1)
    dk = RET_HEAD_DIM

    @pl.when(t == 0)
    def _():
        s_scr[...] = jnp.zeros_like(s_scr)

    cos, sin = cos_ref[...], sin_ref[...]
    ri = lax.broadcasted_iota(jnp.int32, (rows, rows), 0)
    ci = lax.broadcasted_iota(jnp.int32, (rows, rows), 1)
    delta = (ri - ci).astype(F32)
    causal = ri >= ci
    pos = lax.broadcasted_iota(jnp.int32, (rows, 1), 0).astype(F32)

    for h in range(n_heads):
        lo, hi = h * dk, (h + 1) * dk
        log_gamma = math.log1p(-(2.0 ** (-5.0 - h)))
        q = _rotary(q_ref[0, :, lo:hi], cos, sin)
        k = _rotary(k_ref[0, :, lo:hi], cos, sin) * dk ** -0.5
        v_b = v_ref[0, :, lo:hi].astype(BF16)
        intra = jnp.exp(jnp.where(causal, delta * log_gamma, -jnp.inf))
        scores = _dot_nt(q.astype(BF16), k.astype(BF16)) * intra
        q_in = q * jnp.exp((pos + 1.0) * log_gamma)
        k_st = k * jnp.exp((rows - 1.0 - pos) * log_gamma)
        state = s_scr[h]
        o = _dot(scores.astype(BF16), v_b) + _dot(q_in.astype(BF16), state.astype(BF16))
        s_scr[h] = state * math.exp(rows * log_gamma) + _dot_tn(k_st.astype(BF16), v_b)
        out = _rms(o, norm_ref[:, lo:hi]) * _silu(g_ref[0, :, lo:hi])
        o_ref[0, :, lo:hi] = out.astype(o_ref.dtype)


def _retention(proj3, ret_norm, col0):
    b, seq, _ = proj3.shape
    rw = ret_norm.shape[0]
    n_heads = rw // RET_HEAD_DIM
    rows = _pick(seq, 256)
    half = RET_HEAD_DIM // 2
    inv_freq = ROPE_BASE ** (-jnp.arange(half, dtype=F32) / half)
    ang = jnp.arange(seq, dtype=F32)[:, None] * inv_freq[None, :]
    cos, sin = jnp.cos(ang), jnp.sin(ang)
    assert col0 % rw == 0
    c0 = col0 // rw
    col = lambda c: pl.BlockSpec((1, rows, rw), lambda i, t, c=c: (i, t, c0 + c))
    return pl.pallas_call(
        functools.partial(_ret_kernel, n_heads=n_heads, rows=rows),
        grid=(b, seq // rows),
        in_specs=[
            col(0), col(1), col(2), col(3),
            pl.BlockSpec((rows, half), lambda i, t: (t, 0)),
            pl.BlockSpec((rows, half), lambda i, t: (t, 0)),
            pl.BlockSpec((1, rw), lambda i, t: (0, 0)),
        ],
        out_specs=pl.BlockSpec((1, rows, rw), lambda i, t: (i, t, 0)),
        out_shape=jax.ShapeDtypeStruct((b, seq, rw), BF16),
        scratch_shapes=[pltpu.VMEM((n_heads, RET_HEAD_DIM, RET_HEAD_DIM), F32)],
        compiler_params=_cparams("arbitrary", "arbitrary"),
        name="retention",
    )(proj3, proj3, proj3, proj3, cos, sin, ret_norm.reshape(1, rw))


LRU_ROWS = 256


def _lru_block(p_ref, p_prev_ref, seq_start, between, perm_ref, cw_ref, cb_ref, wa_ref, ba_ref, wx_ref,
               bx_ref, lam_ref, o_ref, row0, h_scr, act_scr, *, n_blocks, rows, bw):
    width = n_blocks * bw
    nj = rows // SUBLANES
    sub = lax.broadcasted_iota(jnp.int32, (SUBLANES, bw), 0)
    tails = [jnp.where(seq_start, 0.0, p_prev_ref[rows - SUBLANES * (k - 1) - 1:rows - SUBLANES * (k - 1),
                                                  width:2 * width]) for k in range(1, CONV_WIDTH)]

    for n in range(n_blocks):
        between(n)
        lo, hi = n * bw, (n + 1) * bw
        xp = p_ref[:, width + lo:width + hi]

        def edge(k):
            prev = pltpu.roll(xp[rows - k * SUBLANES:rows - (k - 1) * SUBLANES], 1, 0)
            return jnp.where(sub == 0, tails[k - 1][:, lo:hi], prev)

        edges = [edge(k) for k in range(1, CONV_WIDTH)]
        xc = xp * cw_ref[CONV_WIDTH - 1:CONV_WIDTH, lo:hi]
        for back in range(1, CONV_WIDTH):
            tap = CONV_WIDTH - 1 - back
            shifted = jnp.concatenate(edges[:back][::-1] + [xp[:rows - back * SUBLANES]], axis=0)
            xc = xc + shifted * cw_ref[tap:tap + 1, lo:hi]
        xc = xc + cb_ref[:, lo:hi]
        xc_b = xc.astype(BF16)
        r = _sigmoid(_dot(xc_b, wa_ref[n]) + ba_ref[:, lo:hi])
        i = _sigmoid(_dot(xc_b, wx_ref[n]) + bx_ref[:, lo:hi])
        log_a = (-LRU_C * r) * _softplus(-lam_ref[:, lo:hi])
        a = jnp.exp(log_a)
        mult = jnp.sqrt(-jnp.tanh(log_a) * (a * a + 1.0))
        bv = mult * (i * xc)

        h = bv[0:SUBLANES]
        ac = a[0:SUBLANES]
        hl, al = [h], [ac]
        for j in range(1, nj):
            aj = a[j * SUBLANES:(j + 1) * SUBLANES]
            h = aj * h + bv[j * SUBLANES:(j + 1) * SUBLANES]
            ac = aj * ac
            hl.append(h)
            al.append(ac)
        c = jnp.where(seq_start, 0.0, h_scr[0:1, lo:hi])
        cs = []
        for s in range(SUBLANES):
            cs.append(c)
            c = h[s:s + 1] + ac[s:s + 1] * c
        h_scr[0:1, lo:hi] = c
        c_in = jnp.concatenate(cs, axis=0)
        hs = jnp.concatenate([hl[j] + al[j] * c_in for j in range(nj)], axis=0)
        act_scr[:, lo:hi] = (hs * _gelu_tanh(p_ref[:, lo:hi])).astype(BF16)

    o_ref[row0:row0 + rows, :] = _dot_tn(perm_ref[...], act_scr[...]).astype(o_ref.dtype)


def _lru_in_kernel(x0_ref, mod0_ref, xa_ref, moda_ref, xb_ref, modb_ref, gpre_ref, win_ref, perm_ref,
                   cw_ref, cb_ref, wa_ref, ba_ref, wx_ref, bx_ref, lam_ref, o_ref,
                   p0_scr, p1_scr, h_scr, act_scr, *, n_blocks, rows, bw, blocks_per_seq):
    s = pl.program_id(0)

    def in_proj_cols(h, p_ref, n):
        cw = win_ref.shape[1] // n_blocks
        p_ref[:, n * cw:(n + 1) * cw] = _dot(h, win_ref[:, n * cw:(n + 1) * cw])

    def normed(x_ref, mod_ref):
        h = _norm_mod(x_ref[...], gpre_ref[...], mod_ref).astype(BF16)
        return _dot(perm_ref[...], h).astype(BF16)

    @pl.when(s == 0)
    def _():
        h0 = normed(x0_ref, mod0_ref)
        for n in range(n_blocks):
            in_proj_cols(h0, p0_scr, n)
        p1_scr[...] = jnp.zeros_like(p1_scr)
        h_scr[...] = jnp.zeros_like(h_scr)

    lru = functools.partial(_lru_block, perm_ref=perm_ref, cw_ref=cw_ref, cb_ref=cb_ref, wa_ref=wa_ref,
                            ba_ref=ba_ref, wx_ref=wx_ref, bx_ref=bx_ref, lam_ref=lam_ref, o_ref=o_ref,
                            h_scr=h_scr, act_scr=act_scr, n_blocks=n_blocks, rows=rows, bw=bw)
    seq_start = (2 * s) % blocks_per_seq == 0
    ha = normed(xa_ref, moda_ref)
    lru(p0_scr, p1_scr, seq_start, lambda n: in_proj_cols(ha, p1_scr, n), row0=0)
    hb = normed(xb_ref, modb_ref)
    lru(p1_scr, p0_scr, False, lambda n: in_proj_cols(hb, p0_scr, n), row0=rows)


def _lru_in(x2, mod_s, g_pre, w_in, conv_w, conv_b, gate_a_w, gate_a_b, gate_x_w, gate_x_b, lam, seq):
    m, d = x2.shape
    width = w_in.shape[1] // 2
    n_blocks, bw, _ = gate_a_w.shape
    rows = _pick(seq, LRU_ROWS)
    bps = seq // rows
    nb = m // rows
    assert bps % 2 == 0 and nb % 2 == 0
    last = nb - 1
    r_idx = jnp.arange(rows)
    t_of_r = (r_idx % SUBLANES) * (rows // SUBLANES) + r_idx // SUBLANES
    perm = (t_of_r[:, None] == r_idx[None, :]).astype(BF16)
    xa = lambda s: 2 * s + 1
    xb = lambda s: jnp.minimum(2 * s + 2, last)
    vec = lambda: pl.BlockSpec((1, width), lambda s: (0, 0))
    gate = lambda: pl.BlockSpec((n_blocks, bw, bw), lambda s: (0, 0, 0))
    return pl.pallas_call(
        functools.partial(_lru_in_kernel, n_blocks=n_blocks, rows=rows, bw=bw, blocks_per_seq=bps),
        grid=(nb // 2,),
        in_specs=[
            pl.BlockSpec((rows, d), lambda s: (0, 0)),
            pl.BlockSpec((1, 3, d), lambda s: (0, 0, 0)),
            pl.BlockSpec((rows, d), lambda s: (xa(s), 0)),
            pl.BlockSpec((1, 3, d), lambda s: (xa(s) // bps, 0, 0)),
            pl.BlockSpec((rows, d), lambda s: (xb(s), 0)),
            pl.BlockSpec((1, 3, d), lambda s: (xb(s) // bps, 0, 0)),
            pl.BlockSpec((1, d), lambda s: (0, 0)),
            pl.BlockSpec((d, 2 * width), lambda s: (0, 0)),
            pl.BlockSpec((rows, rows), lambda s: (0, 0)),
            pl.BlockSpec((CONV_WIDTH, width), lambda s: (0, 0)),
            vec(), gate(), vec(), gate(), vec(), vec(),
        ],
        out_specs=pl.BlockSpec((2 * rows, width), lambda s: (s, 0)),
        out_shape=jax.ShapeDtypeStruct((m, width), BF16),
        scratch_shapes=[
            pltpu.VMEM((rows, 2 * width), F32),
            pltpu.VMEM((rows, 2 * width), F32),
            pltpu.VMEM((SUBLANES, width), F32),
            pltpu.VMEM((rows, width), BF16),
        ],
        compiler_params=_cparams("arbitrary"),
        name="rglru_in",
    )(x2, mod_s, x2, mod_s, x2, mod_s, g_pre.reshape(1, d), w_in, perm, conv_w, conv_b.reshape(1, width),
      gate_a_w.astype(BF16), gate_a_b.reshape(1, width), gate_x_w.astype(BF16),
      gate_x_b.reshape(1, width), lam.reshape(1, width))


def kernel(x, c, ada_w, ada_b, norm_pre, norm_post, ffn_w13, ffn_w2, ev_w_in, ev_conv_w, ev_a_log,
           ev_dt_bias, ev_o_norm, ev_ret_norm, ev_w_out, od_w_in, od_conv_w, od_conv_b,
           od_gate_a_w, od_gate_a_b, od_gate_x_w, od_gate_x_b, od_lambda, od_w_out):
    b, seq, d = x.shape
    depth = ada_w.shape[0]
    m = b * seq
    mod = _ada(c, ada_w, ada_b).reshape(depth, b, N_SUB, 3, d)
    x2 = x.reshape(m, d)
    w13_b = ffn_w13.astype(BF16)
    w2_b = ffn_w2.astype(BF16)

    for layer in range(depth):
        mod_l = mod[layer]
        x2 = _ffn(x2, mod_l[:, 0], norm_pre[layer, 0], norm_post[layer, 0], w13_b, w2_b, layer, 0, seq, 0.5)
        if layer % 2 == 0:
            e = layer // 2
            n_heads = ev_a_log.shape[1]
            gw = n_heads * GDN_HEAD_DIM
            w_in = ev_w_in[e]
            w_small = jnp.pad(w_in[:, 3 * gw:3 * gw + 2 * n_heads], ((0, 0), (0, LANES - 2 * n_heads)))
            w_g = jnp.concatenate([w_in[:, :3 * gw], w_in[:, 3 * gw + 2 * n_heads:4 * gw + 2 * n_heads],
                                   w_small], axis=1).astype(BF16)
            w_r = w_in[:, 4 * gw + 2 * n_heads:].astype(BF16)
            o_a = _gdn_in(x2, mod_l[:, 1], norm_pre[layer, 1], w_g, ev_conv_w[e], ev_a_log[e],
                          ev_dt_bias[e], ev_o_norm[e], seq)
            proj_r = _inproj(x2, mod_l[:, 1], norm_pre[layer, 1], w_r, seq)
            o_b = _retention(proj_r.reshape(b, seq, -1), ev_ret_norm[e], 0)
            acts = [o_a, o_b.reshape(m, -1)]
            w_out = ev_w_out[e]
        else:
            o = layer // 2
            hs = _lru_in(x2, mod_l[:, 1], norm_pre[layer, 1], od_w_in[o].astype(BF16), od_conv_w[o],
                         od_conv_b[o], od_gate_a_w[o], od_gate_a_b[o], od_gate_x_w[o], od_gate_x_b[o],
                         od_lambda[o], seq)
            acts = [hs.reshape(m, -1)]
            w_out = od_w_out[o]
        x2 = _outproj(acts, w_out.astype(BF16), x2, mod_l[:, 1], norm_post[layer, 1], seq, 1.0)
        x2 = _ffn(x2, mod_l[:, 2], norm_pre[layer, 2], norm_post[layer, 2], w13_b, w2_b, layer, 1, seq, 0.5)
    return x2.reshape(b, seq, d)
```

```python
import functools
import math

import jax
import jax.numpy as jnp
from jax import lax
from jax.experimental import pallas as pl
from jax.experimental.pallas import tpu as pltpu

F32 = jnp.float32
BF16 = jnp.bfloat16

EPS = 1e-6
GDN_HEAD_DIM = 128
RET_HEAD_DIM = 256
CHUNK = 64
CONV_WIDTH = 4
ROPE_BASE = 10000.0
LRU_C = 8.0
N_SUB = 3

SUBLANES = 8
LANES = 128
V7X_VMEM_LIMIT_BYTES = 56 * 1024 * 1024


def _cparams(*semantics):
    return pltpu.CompilerParams(dimension_semantics=semantics,
                                vmem_limit_bytes=V7X_VMEM_LIMIT_BYTES)


def _dot(a, b):
    return jnp.dot(a, b, preferred_element_type=F32)


def _dot_nt(a, b):
    return lax.dot_general(a, b, (((1,), (1,)), ((), ())), preferred_element_type=F32)


def _dot_tn(a, b):
    return lax.dot_general(a, b, (((0,), (0,)), ((), ())), preferred_element_type=F32)


def _sigmoid(x):
    return jax.nn.sigmoid(x)


def _silu(x):
    return x * jax.nn.sigmoid(x)


def _softplus(x):
    return jnp.maximum(x, 0.0) + jnp.log1p(jnp.exp(-jnp.abs(x)))


def _gelu_tanh(x):
    c = math.sqrt(2.0 / math.pi)
    return 0.5 * x * (1.0 + jnp.tanh(c * (x + 0.044715 * (x * x * x))))


def _rms(x, gain):
    return x * lax.rsqrt(jnp.mean(x * x, axis=-1, keepdims=True) + EPS) * gain


def _inv_rms(x):
    return lax.rsqrt(jnp.mean(x * x, axis=-1, keepdims=True) + EPS)


def _norm_mod(x, g_pre, mod_ref):
    return (x * _inv_rms(x)) * (g_pre * (1.0 + mod_ref[0, 1:2, :])) + mod_ref[0, 0:1, :]


def _post_residual(x, f, g_post, mod_ref, res_w):
    return x + (f * _inv_rms(f)) * (g_post * (res_w * (1.0 + mod_ref[0, 2:3, :])))


def _pick(n, pref):
    if n <= pref:
        return n
    for t in range(pref - pref % LANES, 0, -LANES):
        if n % t == 0:
            return t
    raise ValueError(f"no lane-aligned tile of {n} at or below {pref}")


def _ada_kernel(c_ref, w_ref, b_ref, o_ref):
    a = _silu(c_ref[...]).astype(BF16)
    o_ref[0] = _dot(a, w_ref[0].astype(BF16)) + b_ref[0]


def _ada(c, ada_w, ada_b):
    n_layers, d, n = ada_w.shape
    b = c.shape[0]
    tn = _pick(n, 1024)
    return pl.pallas_call(
        _ada_kernel,
        grid=(n_layers, n // tn),
        in_specs=[
            pl.BlockSpec((b, d), lambda l, j: (0, 0)),
            pl.BlockSpec((1, d, tn), lambda l, j: (l, 0, j)),
            pl.BlockSpec((1, 1, tn), lambda l, j: (l, 0, j)),
        ],
        out_specs=pl.BlockSpec((1, b, tn), lambda l, j: (l, 0, j)),
        out_shape=jax.ShapeDtypeStruct((n_layers, b, n), F32),
        compiler_params=_cparams("arbitrary", "arbitrary"),
        name="ada_mod",
    )(c, ada_w, ada_b.reshape(n_layers, 1, n))


def _ffn_kernel(x_ref, mod_ref, gpre_ref, gpost_ref, w1_ref, w3_ref, w2_ref, o_ref,
                h_scr, acc_scr, *, res_w, n_f):
    j = pl.program_id(1)

    @pl.when(j == 0)
    def _():
        h_scr[...] = _norm_mod(x_ref[...], gpre_ref[...], mod_ref).astype(BF16)
        acc_scr[...] = jnp.zeros_like(acc_scr)

    h = h_scr[...]
    g = _dot(h, w1_ref[...])
    u = _dot(h, w3_ref[...])
    a = (_silu(g) * u).astype(BF16)
    acc_scr[...] += _dot(a, w2_ref[...])

    @pl.when(j == n_f - 1)
    def _():
        o_ref[...] = _post_residual(x_ref[...], acc_scr[...], gpost_ref[...], mod_ref, res_w)


def _ffn(x2, mod_s, g_pre, g_post, w13_all, w2_all, layer, which, seq, res_w):
    m, d = x2.shape
    f = w2_all.shape[2]
    tm = _pick(seq, 512)
    tf = _pick(f, 512)
    n_f = f // tf
    per_b = seq // tm
    return pl.pallas_call(
        functools.partial(_ffn_kernel, res_w=res_w, n_f=n_f),
        grid=(m // tm, n_f),
        in_specs=[
            pl.BlockSpec((tm, d), lambda i, j: (i, 0)),
            pl.BlockSpec((1, 3, d), lambda i, j: (i // per_b, 0, 0)),
            pl.BlockSpec((1, d), lambda i, j: (0, 0)),
            pl.BlockSpec((1, d), lambda i, j: (0, 0)),
            pl.BlockSpec((None, None, d, tf), lambda i, j: (layer, which, 0, j)),
            pl.BlockSpec((None, None, d, tf), lambda i, j: (layer, which, 0, j + n_f)),
            pl.BlockSpec((None, None, tf, d), lambda i, j: (layer, which, j, 0)),
        ],
        out_specs=pl.BlockSpec((tm, d), lambda i, j: (i, 0)),
        out_shape=jax.ShapeDtypeStruct((m, d), F32),
        scratch_shapes=[pltpu.VMEM((tm, d), BF16), pltpu.VMEM((tm, d), F32)],
        compiler_params=_cparams("arbitrary", "arbitrary"),
        name="ffn",
    )(x2, mod_s, g_pre.reshape(1, d), g_post.reshape(1, d), w13_all, w13_all, w2_all)


def _outproj_kernel(*refs, n_in, res_w):
    a_refs = refs[:n_in]
    w_refs = refs[n_in:2 * n_in]
    x_ref, mod_ref, gpost_ref, o_ref = refs[2 * n_in:]
    f = _dot(a_refs[0][...], w_refs[0][...])
    for a_ref, w_ref in zip(a_refs[1:], w_refs[1:]):
        f = f + _dot(a_ref[...], w_ref[...])
    o_ref[...] = _post_residual(x_ref[...], f, gpost_ref[...], mod_ref, res_w)


def _outproj(acts, w_out, x2, mod_s, g_post, seq, res_w):
    m, d = x2.shape
    tm = _pick(seq, 512)
    per_b = seq // tm
    n_in = len(acts)
    in_specs, args = [], []
    for a in acts:
        in_specs.append(pl.BlockSpec((tm, a.shape[1]), lambda i: (i, 0)))
        args.append(a)
    row = 0
    for a in acts:
        wi = a.shape[1]
        assert row % wi == 0
        in_specs.append(pl.BlockSpec((wi, d), lambda i, r=row // wi: (r, 0)))
        args.append(w_out)
        row += wi
    in_specs += [
        pl.BlockSpec((tm, d), lambda i: (i, 0)),
        pl.BlockSpec((1, 3, d), lambda i: (i // per_b, 0, 0)),
        pl.BlockSpec((1, d), lambda i: (0, 0)),
    ]
    args += [x2, mod_s, g_post.reshape(1, d)]
    return pl.pallas_call(
        functools.partial(_outproj_kernel, n_in=n_in, res_w=res_w),
        grid=(m // tm,),
        in_specs=in_specs,
        out_specs=pl.BlockSpec((tm, d), lambda i: (i, 0)),
        out_shape=jax.ShapeDtypeStruct((m, d), F32),
        compiler_params=_cparams("arbitrary"),
        name="mixer_outproj",
    )(*args)


def _conv_from_scratch(cs_ref, w_ref, rows, lo, hi):
    acc = cs_ref[SUBLANES:SUBLANES + rows, lo:hi] * w_ref[CONV_WIDTH - 1:CONV_WIDTH, lo:hi]
    for back in range(1, CONV_WIDTH):
        tap = CONV_WIDTH - 1 - back
        acc = acc + cs_ref[SUBLANES - back:SUBLANES - back + rows, lo:hi] * w_ref[tap:tap + 1, lo:hi]
    return acc


GDN_BLOCK = 256
GDN_PACK = 4


def _tile_rows(x, n):
    return jnp.concatenate([x] * n, axis=0)


def _gdn_block(p_ref, p_prev_ref, seq_start, between, cw_ref, alog_ref, dtb_ref, onorm_ref,
               o_ref, row0, s_scr, qn_scr, kn_scr, kb_scr, vb_scr, kbe_scr, kst_scr, egq_scr,
               a_scr, qk_scr, xo_scr, l_scr, ku_scr, oin_scr, *, n_heads, rows, n_pieces):
    dk = GDN_HEAD_DIM
    gw = n_heads * dk
    n_chunks = rows // CHUNK
    n_groups = n_heads // GDN_PACK
    pw = GDN_PACK * CHUNK
    gk = GDN_PACK * dk

    tail = p_prev_ref[rows:rows + SUBLANES, 0:3 * gw]
    p_ref[0:SUBLANES, 0:3 * gw] = jnp.where(seq_start, 0.0, tail)
    cs_scr = p_ref
    n_between = [0]

    def next_piece():
        between(n_between[0])
        n_between[0] += 1

    sm = p_ref[SUBLANES:SUBLANES + rows, 4 * gw:4 * gw + LANES]
    beta_all = _sigmoid(sm)
    g_all = -jnp.exp(alog_ref[...]) * _softplus(sm + dtb_ref[...])
    in_chunk = lax.broadcasted_iota(jnp.int32, (rows, LANES), 0) & (CHUNK - 1)
    d = 1
    while d < CHUNK:
        g_all = g_all + jnp.where(in_chunk >= d, pltpu.roll(g_all, d, 0), 0.0)
        d *= 2
    eg_all = jnp.exp(g_all)
    glast_all = jnp.concatenate(
        [jnp.broadcast_to(g_all[(c + 1) * CHUNK - 1:(c + 1) * CHUNK, :], (CHUNK, LANES))
         for c in range(n_chunks)], axis=0)
    ekl_all = jnp.exp(glast_all - g_all)

    for h in range(n_heads):
        next_piece()
        lo, hi = h * dk, (h + 1) * dk
        bcast = lambda arr, idx: jnp.broadcast_to(arr[:, idx:idx + 1], (rows, dk))
        beta_b = bcast(beta_all, h)
        eg_b = bcast(eg_all, n_heads + h)
        q = _silu(_conv_from_scratch(cs_scr, cw_ref, rows, lo, hi))
        qn = (q * lax.rsqrt(jnp.sum(q * q, axis=-1, keepdims=True) + EPS)) * dk ** -0.5
        qn_scr[:, lo:hi] = qn.astype(BF16)
        egq_scr[:, lo:hi] = qn * eg_b
        k = _silu(_conv_from_scratch(cs_scr, cw_ref, rows, gw + lo, gw + hi))
        kn = k * lax.rsqrt(jnp.sum(k * k, axis=-1, keepdims=True) + EPS)
        kb = kn * beta_b
        kn_scr[:, lo:hi] = kn.astype(BF16)
        kb_scr[:, lo:hi] = kb.astype(BF16)
        kbe_scr[:, lo:hi] = kb * eg_b
        kst_scr[:, lo:hi] = (kn * bcast(ekl_all, n_heads + h)).astype(BF16)
        v = _silu(_conv_from_scratch(cs_scr, cw_ref, rows, 2 * gw + lo, 2 * gw + hi))
        vb_scr[:, lo:hi] = v * beta_b

    ri = lax.broadcasted_iota(jnp.int32, (CHUNK, pw), 0)
    li = lax.broadcasted_iota(jnp.int32, (CHUNK, pw), 1)
    cj = li & (CHUNK - 1)
    lh = li // CHUNK
    causal = ri >= cj
    strict = ri > cj
    eye = ri == cj
    bd_r = lax.broadcasted_iota(jnp.int32, (pw, pw), 0) // CHUNK
    bd_c = lax.broadcasted_iota(jnp.int32, (pw, pw), 1) // CHUNK
    bd_sq = bd_r == bd_c
    bk_r = lax.broadcasted_iota(jnp.int32, (pw, gk), 0) // CHUNK
    bk_c = lax.broadcasted_iota(jnp.int32, (pw, gk), 1) // dk
    bd_k = bk_r == bk_c
    zero_sq = jnp.zeros((pw, pw), BF16)
    zero_k = jnp.zeros((pw, gk), BF16)

    def block_diag(x_rp):
        return jnp.where(bd_sq, _tile_rows(x_rp.astype(BF16), GDN_PACK), zero_sq)

    def block_diag_k(x_cat):
        return jnp.where(bd_k, _tile_rows(x_cat, GDN_PACK), zero_k)

    probs = [(c, gi) for c in range(n_chunks) for gi in range(n_groups)]
    for p, (c, gi) in enumerate(probs):
        next_piece()
        r0, r1 = c * CHUNK, (c + 1) * CHUNK
        c0 = gi * gk
        gcs = [jnp.broadcast_to(g_all[r0:r1, n_heads + gi * GDN_PACK + s:n_heads + gi * GDN_PACK + s + 1],
                                (CHUNK, pw)) for s in range(GDN_PACK)]
        gc = gcs[GDN_PACK - 1]
        for s in range(GDN_PACK - 2, -1, -1):
            gc = jnp.where(lh == s, gcs[s], gc)
        gr = jnp.sum(jnp.where(eye, gc, 0.0), axis=0, keepdims=True)
        decay = jnp.exp(jnp.where(causal, gc - gr, -jnp.inf))
        lhs = jnp.concatenate([kb_scr[r0:r1, c0:c0 + gk], qn_scr[r0:r1, c0:c0 + gk]], axis=0)
        kq = _dot_nt(lhs, block_diag_k(kn_scr[r0:r1, c0:c0 + gk]))
        a = jnp.where(strict, kq[0:CHUNK] * decay, 0.0)
        a_scr[p] = a
        qk_scr[p] = jnp.where(causal, kq[CHUNK:2 * CHUNK] * decay, 0.0)
        xo_scr[p] = -jnp.where((ri - cj == 1) & ((ri & 1) == 1), a, 0.0)

    next_piece()
    s = 2
    while s < CHUNK:
        rb = ri // s
        off = ((rb & 1) == 1) & ((cj // s) == rb - 1)
        for p in range(len(probs)):
            a_off = jnp.where(off, a_scr[p], 0.0)
            xo = xo_scr[p]
            y = a_off + _dot(xo.astype(BF16), block_diag(a_off))
            xo_scr[p] = xo - (y + _dot(y.astype(BF16), block_diag(xo)))
        s *= 2

    for p, (c, gi) in enumerate(probs):
        r0, r1 = c * CHUNK, (c + 1) * CHUNK
        rhs = jnp.concatenate(
            [jnp.concatenate([vb_scr[r0:r1, h * dk:(h + 1) * dk], kbe_scr[r0:r1, h * dk:(h + 1) * dk]], axis=1)
             for h in range(gi * GDN_PACK, (gi + 1) * GDN_PACK)], axis=0)
        sol = rhs + _dot(block_diag(xo_scr[p]), rhs.astype(BF16))
        sol_b = sol.astype(BF16)
        qks = _dot(block_diag(qk_scr[p]), sol_b)
        ks = _dot_tn(block_diag_k(kst_scr[r0:r1, gi * gk:(gi + 1) * gk]), sol_b)
        for s_ in range(GDN_PACK):
            h = gi * GDN_PACK + s_
            half = (h % 2) * dk
            rr = slice(s_ * CHUNK, (s_ + 1) * CHUNK)
            kr = slice(s_ * dk, (s_ + 1) * dk)
            q_eff = egq_scr[r0:r1, h * dk:(h + 1) * dk] - qks[rr, dk:2 * dk]
            l_scr[c, h // 2, 0:dk, half:half + dk] = ks[kr, dk:2 * dk].astype(BF16)
            l_scr[c, h // 2, dk:dk + CHUNK, half:half + dk] = q_eff.astype(BF16)
            ku_scr[c, h] = ks[kr, 0:dk]
            oin_scr[r0:r1, h * dk:(h + 1) * dk] = qks[rr, 0:dk]

    zero = jnp.zeros((dk, dk), BF16)
    for c in range(n_chunks):
        r0, r1 = c * CHUNK, (c + 1) * CHUNK
        for pr in range(n_heads // 2):
            h0, h1 = 2 * pr, 2 * pr + 1
            s0 = s_scr[h0]
            s1 = s_scr[h1]
            if c == 0:
                s0 = jnp.where(seq_start, 0.0, s0)
                s1 = jnp.where(seq_start, 0.0, s1)
            s_bd = jnp.concatenate(
                [jnp.concatenate([s0.astype(BF16), zero], axis=1),
                 jnp.concatenate([zero, s1.astype(BF16)], axis=1)], axis=0)
            r = _dot(l_scr[c, pr], s_bd)
            for h, st, off_ in ((h0, s0, 0), (h1, s1, dk)):
                cd = eg_all[r1 - 1:r1, n_heads + h:n_heads + h + 1]
                s_scr[h] = st * cd - r[0:dk, off_:off_ + dk] + ku_scr[c, h]
                o = r[dk:dk + CHUNK, off_:off_ + dk] + oin_scr[r0:r1, h * dk:(h + 1) * dk]
                z = p_ref[SUBLANES + r0:SUBLANES + r1, 3 * gw + h * dk:3 * gw + (h + 1) * dk]
                out = _rms(o, onorm_ref[...]) * _silu(z)
                o_ref[row0 + r0:row0 + r1, h * dk:(h + 1) * dk] = out.astype(o_ref.dtype)

    while n_between[0] < n_pieces:
        next_piece()


GDN_PIECES = 17


def _gdn_in_kernel(xa_ref, moda_ref, xb_ref, modb_ref, gpre_ref, win_ref,
                   cw_ref, alog_ref, dtb_ref, onorm_ref, o_ref, p0_scr, p1_scr, *scratch,
                   n_heads, rows, blocks_per_seq):
    s = pl.program_id(0)
    n_cols = win_ref.shape[1]
    piece = -(-n_cols // (GDN_PIECES * LANES)) * LANES
    bounds = [(c, min(c + piece, n_cols)) for c in range(0, n_cols, piece)]
    assert len(bounds) <= GDN_PIECES

    def in_proj_piece(h, p_ref, i):
        if i < len(bounds):
            c0, c1 = bounds[i]
            p_ref[SUBLANES:SUBLANES + rows, c0:c1] = _dot(h, win_ref[:, c0:c1])

    n_pieces = len(bounds)

    def normed(x_ref, mod_ref):
        return _norm_mod(x_ref[...], gpre_ref[...], mod_ref).astype(BF16)

    @pl.when(s == 0)
    def _():
        p0_scr[...] = jnp.zeros_like(p0_scr)
        p1_scr[...] = jnp.zeros_like(p1_scr)
        scratch[0][...] = jnp.zeros_like(scratch[0])

    seq_start = (2 * s - 2) % blocks_per_seq == 0
    ha = normed(xa_ref, moda_ref)
    _gdn_block(p0_scr, p1_scr, seq_start, lambda i: in_proj_piece(ha, p1_scr, i),
               cw_ref, alog_ref, dtb_ref, onorm_ref, o_ref, 0, *scratch, n_heads=n_heads, rows=rows,
               n_pieces=n_pieces)
    hb = normed(xb_ref, modb_ref)
    _gdn_block(p1_scr, p0_scr, False, lambda i: in_proj_piece(hb, p0_scr, i),
               cw_ref, alog_ref, dtb_ref, onorm_ref, o_ref, rows, *scratch, n_heads=n_heads, rows=rows,
               n_pieces=n_pieces)


def _gdn_in(x2, mod_s, g_pre, w_g, conv_w, a_log, dt_bias, o_norm, seq):
    m, d = x2.shape
    n_heads = a_log.shape[0]
    assert n_heads % GDN_PACK == 0 and GDN_HEAD_DIM == LANES
    gw = n_heads * GDN_HEAD_DIM
    n_cols = w_g.shape[1]
    assert n_cols == 4 * gw + LANES
    rows = _pick(seq, GDN_BLOCK)
    bps = seq // rows
    nb = m // rows
    assert bps % 2 == 0 and nb % 2 == 0
    last = nb - 1
    n_chunks = rows // CHUNK
    n_prob = n_chunks * (n_heads // GDN_PACK)
    pad = LANES - 2 * n_heads
    alog_row = jnp.pad(a_log, (n_heads, pad)).reshape(1, LANES)
    dtb_row = jnp.pad(dt_bias, (n_heads, pad)).reshape(1, LANES)
    xa = lambda s: jnp.maximum(2 * s - 1, 0)
    xb = lambda s: jnp.minimum(2 * s, last)
    return pl.pallas_call(
        functools.partial(_gdn_in_kernel, n_heads=n_heads, rows=rows, blocks_per_seq=bps),
        grid=(nb // 2 + 1,),
        in_specs=[
            pl.BlockSpec((rows, d), lambda s: (xa(s), 0)),
            pl.BlockSpec((1, 3, d), lambda s: (xa(s) // bps, 0, 0)),
            pl.BlockSpec((rows, d), lambda s: (xb(s), 0)),
            pl.BlockSpec((1, 3, d), lambda s: (xb(s) // bps, 0, 0)),
            pl.BlockSpec((1, d), lambda s: (0, 0)),
            pl.BlockSpec((d, n_cols), lambda s: (0, 0)),
            pl.BlockSpec((CONV_WIDTH, 3 * gw), lambda s: (0, 0)),
            pl.BlockSpec((1, LANES), lambda s: (0, 0)),
            pl.BlockSpec((1, LANES), lambda s: (0, 0)),
            pl.BlockSpec((1, GDN_HEAD_DIM), lambda s: (0, 0)),
        ],
        out_specs=pl.BlockSpec((2 * rows, gw), lambda s: (jnp.maximum(s - 1, 0), 0)),
        out_shape=jax.ShapeDtypeStruct((m, gw), BF16),
        scratch_shapes=[
            pltpu.VMEM((rows + SUBLANES, n_cols), F32),
            pltpu.VMEM((rows + SUBLANES, n_cols), F32),
            pltpu.VMEM((n_heads, GDN_HEAD_DIM, GDN_HEAD_DIM), F32),
            pltpu.VMEM((rows, gw), BF16),
            pltpu.VMEM((rows, gw), BF16),
            pltpu.VMEM((rows, gw), BF16),
            pltpu.VMEM((rows, gw), F32),
            pltpu.VMEM((rows, gw), F32),
            pltpu.VMEM((rows, gw), BF16),
            pltpu.VMEM((rows, gw), F32),
            pltpu.VMEM((n_prob, CHUNK, GDN_PACK * CHUNK), F32),
            pltpu.VMEM((n_prob, CHUNK, GDN_PACK * CHUNK), F32),
            pltpu.VMEM((n_prob, CHUNK, GDN_PACK * CHUNK), F32),
            pltpu.VMEM((n_chunks, n_heads // 2, GDN_HEAD_DIM + CHUNK, 2 * GDN_HEAD_DIM), BF16),
            pltpu.VMEM((n_chunks, n_heads, GDN_HEAD_DIM, GDN_HEAD_DIM), F32),
            pltpu.VMEM((rows, gw), F32),
        ],
        compiler_params=_cparams("arbitrary"),
        name="gdn_in",
    )(x2, mod_s, x2, mod_s, g_pre.reshape(1, d), w_g, conv_w, alog_row, dtb_row,
      o_norm.reshape(1, GDN_HEAD_DIM))


def _rotary(x, cos, sin):
    half = x.shape[-1] // 2
    x1, x2 = x[:, :half], x[:, half:]
    return jnp.concatenate([x1 * cos - x2 * sin, x2 * cos + x1 * sin], axis=-1)


RET_PIECES = 8


def _ret_block(p_ref, seq_start, between, cos, sin, norm_ref, o_ref, row0, s_scr, *, n_heads, rows,
               n_pieces):
    dk = RET_HEAD_DIM
    rw = n_heads * dk
    ri = lax.broadcasted_iota(jnp.int32, (rows, rows), 0)
    ci = lax.broadcasted_iota(jnp.int32, (rows, rows), 1)
    delta = (ri - ci).astype(F32)
    causal = ri >= ci
    pos = lax.broadcasted_iota(jnp.int32, (rows, 1), 0).astype(F32)
    done = 0

    for h in range(n_heads):
        while done < min(n_pieces, (h + 1) * -(-n_pieces // n_heads)):
            between(done)
            done += 1
        lo, hi = h * dk, (h + 1) * dk
        log_gamma = math.log1p(-(2.0 ** (-5.0 - h)))
        q = _rotary(p_ref[:, lo:hi], cos, sin)
        k = _rotary(p_ref[:, rw + lo:rw + hi], cos, sin) * dk ** -0.5
        v_b = p_ref[:, 2 * rw + lo:2 * rw + hi].astype(BF16)
        intra = jnp.exp(jnp.where(causal, delta * log_gamma, -jnp.inf))
        scores = _dot_nt(q.astype(BF16), k.astype(BF16)) * intra
        q_in = q * jnp.exp((pos + 1.0) * log_gamma)
        k_st = k * jnp.exp((rows - 1.0 - pos) * log_gamma)
        state = jnp.where(seq_start, 0.0, s_scr[h])
        o = _dot(scores.astype(BF16), v_b) + _dot(q_in.astype(BF16), state.astype(BF16))
        s_scr[h] = state * math.exp(rows * log_gamma) + _dot_tn(k_st.astype(BF16), v_b)
        out = _rms(o, norm_ref[:, lo:hi]) * _silu(p_ref[:, 3 * rw + lo:3 * rw + hi])
        o_ref[row0:row0 + rows, lo:hi] = out.astype(o_ref.dtype)

    while done < n_pieces:
        between(done)
        done += 1


def _ret_in_kernel(xa_ref, moda_ref, xb_ref, modb_ref, gpre_ref, win_ref, cos_ref, sin_ref, norm_ref,
                   o_ref, p0_scr, p1_scr, s_scr, *, n_heads, rows, blocks_per_seq):
    s = pl.program_id(0)
    piece = win_ref.shape[1] // RET_PIECES

    def in_proj_piece(h, p_ref, i):
        p_ref[:, i * piece:(i + 1) * piece] = _dot(h, win_ref[:, i * piece:(i + 1) * piece])

    def normed(x_ref, mod_ref):
        return _norm_mod(x_ref[...], gpre_ref[...], mod_ref).astype(BF16)

    @pl.when(s == 0)
    def _():
        p0_scr[...] = jnp.zeros_like(p0_scr)
        p1_scr[...] = jnp.zeros_like(p1_scr)
        s_scr[...] = jnp.zeros_like(s_scr)

    seq_start = (2 * s - 2) % blocks_per_seq == 0
    ha = normed(xa_ref, moda_ref)
    _ret_block(p0_scr, seq_start, lambda i: in_proj_piece(ha, p1_scr, i), cos_ref[0:rows, :],
               sin_ref[0:rows, :], norm_ref, o_ref, 0, s_scr, n_heads=n_heads, rows=rows,
               n_pieces=RET_PIECES)
    hb = normed(xb_ref, modb_ref)
    _ret_block(p1_scr, False, lambda i: in_proj_piece(hb, p0_scr, i), cos_ref[rows:2 * rows, :],
               sin_ref[rows:2 * rows, :], norm_ref, o_ref, rows, s_scr, n_heads=n_heads, rows=rows,
               n_pieces=RET_PIECES)


def _ret_in(x2, mod_s, g_pre, w_r, ret_norm, seq):
    m, d = x2.shape
    rw = ret_norm.shape[0]
    n_heads = rw // RET_HEAD_DIM
    n_cols = w_r.shape[1]
    assert n_cols == 4 * rw and n_cols % (RET_PIECES * LANES) == 0
    rows = _pick(seq, 256)
    bps = seq // rows
    nb = m // rows
    assert bps % 2 == 0 and nb % 2 == 0
    last = nb - 1
    half = RET_HEAD_DIM // 2
    inv_freq = ROPE_BASE ** (-jnp.arange(half, dtype=F32) / half)
    ang = jnp.arange(seq, dtype=F32)[:, None] * inv_freq[None, :]
    cos, sin = jnp.cos(ang), jnp.sin(ang)
    xa = lambda s: jnp.maximum(2 * s - 1, 0)
    xb = lambda s: jnp.minimum(2 * s, last)
    pos_blk = lambda s: (jnp.maximum(2 * s - 2, 0) % bps) // 2
    return pl.pallas_call(
        functools.partial(_ret_in_kernel, n_heads=n_heads, rows=rows, blocks_per_seq=bps),
        grid=(nb // 2 + 1,),
        in_specs=[
            pl.BlockSpec((rows, d), lambda s: (xa(s), 0)),
            pl.BlockSpec((1, 3, d), lambda s: (xa(s) // bps, 0, 0)),
            pl.BlockSpec((rows, d), lambda s: (xb(s), 0)),
            pl.BlockSpec((1, 3, d), lambda s: (xb(s) // bps, 0, 0)),
            pl.BlockSpec((1, d), lambda s: (0, 0)),
            pl.BlockSpec((d, n_cols), lambda s: (0, 0)),
            pl.BlockSpec((2 * rows, half), lambda s: (pos_blk(s), 0)),
            pl.BlockSpec((2 * rows, half), lambda s: (pos_blk(s), 0)),
            pl.BlockSpec((1, rw), lambda s: (0, 0)),
        ],
        out_specs=pl.BlockSpec((2 * rows, rw), lambda s: (jnp.maximum(s - 1, 0), 0)),
        out_shape=jax.ShapeDtypeStruct((m, rw), BF16),
        scratch_shapes=[
            pltpu.VMEM((rows, n_cols), F32),
            pltpu.VMEM((rows, n_cols), F32),
            pltpu.VMEM((n_heads, RET_HEAD_DIM, RET_HEAD_DIM), F32),
        ],
        compiler_params=_cparams("arbitrary"),
        name="retention_in",
    )(x2, mod_s, x2, mod_s, g_pre.reshape(1, d), w_r, cos, sin, ret_norm.reshape(1, rw))


LRU_ROWS = 256


def _lru_block(p_ref, p_prev_ref, seq_start, between, perm_ref, cw_ref, cb_ref, wa_ref, ba_ref, wx_ref,
               bx_ref, lam_ref, o_ref, row0, h_scr, act_scr, *, n_blocks, rows, bw):
    width = n_blocks * bw
    nj = rows // SUBLANES
    sub = lax.broadcasted_iota(jnp.int32, (SUBLANES, bw), 0)
    tails = [jnp.where(seq_start, 0.0, p_prev_ref[rows - SUBLANES * (k - 1) - 1:rows - SUBLANES * (k - 1),
                                                  width:2 * width]) for k in range(1, CONV_WIDTH)]

    for n in range(n_blocks):
        between(n)
        lo, hi = n * bw, (n + 1) * bw
        xp = p_ref[:, width + lo:width + hi]

        def edge(k):
            prev = pltpu.roll(xp[rows - k * SUBLANES:rows - (k - 1) * SUBLANES], 1, 0)
            return jnp.where(sub == 0, tails[k - 1][:, lo:hi], prev)

        edges = [edge(k) for k in range(1, CONV_WIDTH)]
        xc = xp * cw_ref[CONV_WIDTH - 1:CONV_WIDTH, lo:hi]
        for back in range(1, CONV_WIDTH):
            tap = CONV_WIDTH - 1 - back
            shifted = jnp.concatenate(edges[:back][::-1] + [xp[:rows - back * SUBLANES]], axis=0)
            xc = xc + shifted * cw_ref[tap:tap + 1, lo:hi]
        xc = xc + cb_ref[:, lo:hi]
        xc_b = xc.astype(BF16)
        r = _sigmoid(_dot(xc_b, wa_ref[n]) + ba_ref[:, lo:hi])
        i = _sigmoid(_dot(xc_b, wx_ref[n]) + bx_ref[:, lo:hi])
        log_a = (-LRU_C * r) * _softplus(-lam_ref[:, lo:hi])
        a = jnp.exp(log_a)
        mult = jnp.sqrt(-jnp.tanh(log_a) * (a * a + 1.0))
        bv = mult * (i * xc)

        h = bv[0:SUBLANES]
        ac = a[0:SUBLANES]
        hl, al = [h], [ac]
        for j in range(1, nj):
            aj = a[j * SUBLANES:(j + 1) * SUBLANES]
            h = aj * h + bv[j * SUBLANES:(j + 1) * SUBLANES]
            ac = aj * ac
            hl.append(h)
            al.append(ac)
        c = jnp.where(seq_start, 0.0, h_scr[0:1, lo:hi])
        cs = []
        for s in range(SUBLANES):
            cs.append(c)
            c = h[s:s + 1] + ac[s:s + 1] * c
        h_scr[0:1, lo:hi] = c
        c_in = jnp.concatenate(cs, axis=0)
        hs = jnp.concatenate([hl[j] + al[j] * c_in for j in range(nj)], axis=0)
        act_scr[:, lo:hi] = (hs * _gelu_tanh(p_ref[:, lo:hi])).astype(BF16)

    o_ref[row0:row0 + rows, :] = _dot_tn(perm_ref[...], act_scr[...]).astype(o_ref.dtype)


def _lru_in_kernel(x0_ref, mod0_ref, xa_ref, moda_ref, xb_ref, modb_ref, gpre_ref, win_ref, perm_ref,
                   cw_ref, cb_ref, wa_ref, ba_ref, wx_ref, bx_ref, lam_ref, o_ref,
                   p0_scr, p1_scr, h_scr, act_scr, *, n_blocks, rows, bw, blocks_per_seq):
    s = pl.program_id(0)

    def in_proj_cols(h, p_ref, n):
        cw = win_ref.shape[1] // n_blocks
        p_ref[:, n * cw:(n + 1) * cw] = _dot(h, win_ref[:, n * cw:(n + 1) * cw])

    def normed(x_ref, mod_ref):
        h = _norm_mod(x_ref[...], gpre_ref[...], mod_ref).astype(BF16)
        return _dot(perm_ref[...], h).astype(BF16)

    @pl.when(s == 0)
    def _():
        h0 = normed(x0_ref, mod0_ref)
        for n in range(n_blocks):
            in_proj_cols(h0, p0_scr, n)
        p1_scr[...] = jnp.zeros_like(p1_scr)
        h_scr[...] = jnp.zeros_like(h_scr)

    lru = functools.partial(_lru_block, perm_ref=perm_ref, cw_ref=cw_ref, cb_ref=cb_ref, wa_ref=wa_ref,
                            ba_ref=ba_ref, wx_ref=wx_ref, bx_ref=bx_ref, lam_ref=lam_ref, o_ref=o_ref,
                            h_scr=h_scr, act_scr=act_scr, n_blocks=n_blocks, rows=rows, bw=bw)
    seq_start = (2 * s) % blocks_per_seq == 0
    ha = normed(xa_ref, moda_ref)
    lru(p0_scr, p1_scr, seq_start, lambda n: in_proj_cols(ha, p1_scr, n), row0=0)
    hb = normed(xb_ref, modb_ref)
    lru(p1_scr, p0_scr, False, lambda n: in_proj_cols(hb, p0_scr, n), row0=rows)


def _lru_in(x2, mod_s, g_pre, w_in, conv_w, conv_b, gate_a_w, gate_a_b, gate_x_w, gate_x_b, lam, seq):
    m, d = x2.shape
    width = w_in.shape[1] // 2
    n_blocks, bw, _ = gate_a_w.shape
    rows = _pick(seq, LRU_ROWS)
    bps = seq // rows
    nb = m // rows
    assert bps % 2 == 0 and nb % 2 == 0
    last = nb - 1
    r_idx = jnp.arange(rows)
    t_of_r = (r_idx % SUBLANES) * (rows // SUBLANES) + r_idx // SUBLANES
    perm = (t_of_r[:, None] == r_idx[None, :]).astype(BF16)
    xa = lambda s: 2 * s + 1
    xb = lambda s: jnp.minimum(2 * s + 2, last)
    vec = lambda: pl.BlockSpec((1, width), lambda s: (0, 0))
    gate = lambda: pl.BlockSpec((n_blocks, bw, bw), lambda s: (0, 0, 0))
    return pl.pallas_call(
        functools.partial(_lru_in_kernel, n_blocks=n_blocks, rows=rows, bw=bw, blocks_per_seq=bps),
        grid=(nb // 2,),
        in_specs=[
            pl.BlockSpec((rows, d), lambda s: (0, 0)),
            pl.BlockSpec((1, 3, d), lambda s: (0, 0, 0)),
            pl.BlockSpec((rows, d), lambda s: (xa(s), 0)),
            pl.BlockSpec((1, 3, d), lambda s: (xa(s) // bps, 0, 0)),
            pl.BlockSpec((rows, d), lambda s: (xb(s), 0)),
            pl.BlockSpec((1, 3, d), lambda s: (xb(s) // bps, 0, 0)),
            pl.BlockSpec((1, d), lambda s: (0, 0)),
            pl.BlockSpec((d, 2 * width), lambda s: (0, 0)),
            pl.BlockSpec((rows, rows), lambda s: (0, 0)),
            pl.BlockSpec((CONV_WIDTH, width), lambda s: (0, 0)),
            vec(), gate(), vec(), gate(), vec(), vec(),
        ],
        out_specs=pl.BlockSpec((2 * rows, width), lambda s: (s, 0)),
        out_shape=jax.ShapeDtypeStruct((m, width), BF16),
        scratch_shapes=[
            pltpu.VMEM((rows, 2 * width), F32),
            pltpu.VMEM((rows, 2 * width), F32),
            pltpu.VMEM((SUBLANES, width), F32),
            pltpu.VMEM((rows, width), BF16),
        ],
        compiler_params=_cparams("arbitrary"),
        name="rglru_in",
    )(x2, mod_s, x2, mod_s, x2, mod_s, g_pre.reshape(1, d), w_in, perm, conv_w, conv_b.reshape(1, width),
      gate_a_w.astype(BF16), gate_a_b.reshape(1, width), gate_x_w.astype(BF16),
      gate_x_b.reshape(1, width), lam.reshape(1, width))


def kernel(x, c, ada_w, ada_b, norm_pre, norm_post, ffn_w13, ffn_w2, ev_w_in, ev_conv_w, ev_a_log,
           ev_dt_bias, ev_o_norm, ev_ret_norm, ev_w_out, od_w_in, od_conv_w, od_conv_b,
           od_gate_a_w, od_gate_a_b, od_gate_x_w, od_gate_x_b, od_lambda, od_w_out):
    b, seq, d = x.shape
    depth = ada_w.shape[0]
    m = b * seq
    mod = _ada(c, ada_w, ada_b).reshape(depth, b, N_SUB, 3, d)
    x2 = x.reshape(m, d)
    w13_b = ffn_w13.astype(BF16)
    w2_b = ffn_w2.astype(BF16)

    for layer in range(depth):
        mod_l = mod[layer]
        x2 = _ffn(x2, mod_l[:, 0], norm_pre[layer, 0], norm_post[layer, 0], w13_b, w2_b, layer, 0, seq, 0.5)
        if layer % 2 == 0:
            e = layer // 2
            n_heads = ev_a_log.shape[1]
            gw = n_heads * GDN_HEAD_DIM
            w_in = ev_w_in[e]
            w_small = jnp.pad(w_in[:, 3 * gw:3 * gw + 2 * n_heads], ((0, 0), (0, LANES - 2 * n_heads)))
            w_g = jnp.concatenate([w_in[:, :3 * gw], w_in[:, 3 * gw + 2 * n_heads:4 * gw + 2 * n_heads],
                                   w_small], axis=1).astype(BF16)
            w_r = w_in[:, 4 * gw + 2 * n_heads:].astype(BF16)
            o_a = _gdn_in(x2, mod_l[:, 1], norm_pre[layer, 1], w_g, ev_conv_w[e], ev_a_log[e],
                          ev_dt_bias[e], ev_o_norm[e], seq)
            o_b = _ret_in(x2, mod_l[:, 1], norm_pre[layer, 1], w_r, ev_ret_norm[e], seq)
            acts = [o_a, o_b]
            w_out = ev_w_out[e]
        else:
            o = layer // 2
            hs = _lru_in(x2, mod_l[:, 1], norm_pre[layer, 1], od_w_in[o].astype(BF16), od_conv_w[o],
                         od_conv_b[o], od_gate_a_w[o], od_gate_a_b[o], od_gate_x_w[o], od_gate_x_b[o],
                         od_lambda[o], seq)
            acts = [hs.reshape(m, -1)]
            w_out = od_w_out[o]
        x2 = _outproj(acts, w_out.astype(BF16), x2, mod_l[:, 1], norm_post[layer, 1], seq, 1.0)
        x2 = _ffn(x2, mod_l[:, 2], norm_pre[layer, 2], norm_post[layer, 2], w13_b, w2_b, layer, 1, seq, 0.5)
    return x2.reshape(b, seq, d)
```

```python
import functools
import math

import jax
import jax.numpy as jnp
from jax import lax
from jax.experimental import pallas as pl
from jax.experimental.pallas import tpu as pltpu

F32 = jnp.float32
BF16 = jnp.bfloat16

EPS = 1e-6
GDN_HEAD_DIM = 128
RET_HEAD_DIM = 256
CHUNK = 64
CONV_WIDTH = 4
ROPE_BASE = 10000.0
LRU_C = 8.0
N_SUB = 3

SUBLANES = 8
LANES = 128
V7X_VMEM_LIMIT_BYTES = 56 * 1024 * 1024


def _cparams(*semantics):
    return pltpu.CompilerParams(dimension_semantics=semantics,
                                vmem_limit_bytes=V7X_VMEM_LIMIT_BYTES)


def _dot(a, b):
    return jnp.dot(a, b, preferred_element_type=F32)


def _dot_nt(a, b):
    return lax.dot_general(a, b, (((1,), (1,)), ((), ())), preferred_element_type=F32)


def _dot_tn(a, b):
    return lax.dot_general(a, b, (((0,), (0,)), ((), ())), preferred_element_type=F32)


def _sigmoid(x):
    return jax.nn.sigmoid(x)


def _silu(x):
    return x * jax.nn.sigmoid(x)


def _softplus(x):
    return jnp.maximum(x, 0.0) + jnp.log1p(jnp.exp(-jnp.abs(x)))


def _gelu_tanh(x):
    c = math.sqrt(2.0 / math.pi)
    return 0.5 * x * (1.0 + jnp.tanh(c * (x + 0.044715 * (x * x * x))))


def _rms(x, gain):
    return x * lax.rsqrt(jnp.mean(x * x, axis=-1, keepdims=True) + EPS) * gain


def _inv_rms(x):
    return lax.rsqrt(jnp.mean(x * x, axis=-1, keepdims=True) + EPS)


def _norm_mod(x, g_pre, mod_ref):
    return (x * _inv_rms(x)) * (g_pre * (1.0 + mod_ref[0, 1:2, :])) + mod_ref[0, 0:1, :]


def _post_residual(x, f, g_post, mod_ref, res_w):
    return x + (f * _inv_rms(f)) * (g_post * (res_w * (1.0 + mod_ref[0, 2:3, :])))


def _pick(n, pref):
    if n <= pref:
        return n
    for t in range(pref - pref % LANES, 0, -LANES):
        if n % t == 0:
            return t
    raise ValueError(f"no lane-aligned tile of {n} at or below {pref}")


def _ada_kernel(c_ref, w_ref, b_ref, o_ref):
    a = _silu(c_ref[...]).astype(BF16)
    o_ref[0] = _dot(a, w_ref[0].astype(BF16)) + b_ref[0]


def _ada(c, ada_w, ada_b):
    n_layers, d, n = ada_w.shape
    b = c.shape[0]
    tn = _pick(n, 1024)
    return pl.pallas_call(
        _ada_kernel,
        grid=(n_layers, n // tn),
        in_specs=[
            pl.BlockSpec((b, d), lambda l, j: (0, 0)),
            pl.BlockSpec((1, d, tn), lambda l, j: (l, 0, j)),
            pl.BlockSpec((1, 1, tn), lambda l, j: (l, 0, j)),
        ],
        out_specs=pl.BlockSpec((1, b, tn), lambda l, j: (l, 0, j)),
        out_shape=jax.ShapeDtypeStruct((n_layers, b, n), F32),
        compiler_params=_cparams("arbitrary", "arbitrary"),
        name="ada_mod",
    )(c, ada_w, ada_b.reshape(n_layers, 1, n))


FFN_HALVES = 2


def _ffn_kernel(x_ref, mod_ref, gpre_ref, gpost_ref, w1_ref, w3_ref, w2_ref, o_ref,
                h_scr, acc_scr, *, res_w, n_f):
    j = pl.program_id(1)
    tm = x_ref.shape[0]
    part = tm // FFN_HALVES

    def down(rs):
        h = h_scr[rs, :]
        a = (_silu(_dot(h, w1_ref[...])) * _dot(h, w3_ref[...])).astype(BF16)
        return _dot(a, w2_ref[...])

    @pl.when(j == 0)
    def _():
        for p in range(FFN_HALVES):
            rs = slice(p * part, (p + 1) * part)
            h_scr[rs, :] = _norm_mod(x_ref[rs, :], gpre_ref[...], mod_ref).astype(BF16)
            acc_scr[rs, :] = down(rs)

    @pl.when((j > 0) & (j < n_f - 1))
    def _():
        acc_scr[...] += down(slice(None))

    @pl.when(j == n_f - 1)
    def _():
        for p in range(FFN_HALVES):
            rs = slice(p * part, (p + 1) * part)
            f = acc_scr[rs, :] + down(rs)
            o_ref[rs, :] = _post_residual(x_ref[rs, :], f, gpost_ref[...], mod_ref, res_w)


def _ffn(x2, mod_s, g_pre, g_post, w13_all, w2_all, layer, which, seq, res_w):
    m, d = x2.shape
    f = w2_all.shape[2]
    tm = _pick(seq, 512)
    tf = _pick(f, 512)
    n_f = f // tf
    per_b = seq // tm
    assert n_f >= 2 and tm % (FFN_HALVES * 2 * SUBLANES) == 0
    return pl.pallas_call(
        functools.partial(_ffn_kernel, res_w=res_w, n_f=n_f),
        grid=(m // tm, n_f),
        in_specs=[
            pl.BlockSpec((tm, d), lambda i, j: (i, 0)),
            pl.BlockSpec((1, 3, d), lambda i, j: (i // per_b, 0, 0)),
            pl.BlockSpec((1, d), lambda i, j: (0, 0)),
            pl.BlockSpec((1, d), lambda i, j: (0, 0)),
            pl.BlockSpec((None, None, d, tf), lambda i, j: (layer, which, 0, j)),
            pl.BlockSpec((None, None, d, tf), lambda i, j: (layer, which, 0, j + n_f)),
            pl.BlockSpec((None, None, tf, d), lambda i, j: (layer, which, j, 0)),
        ],
        out_specs=pl.BlockSpec((tm, d), lambda i, j: (i, 0)),
        out_shape=jax.ShapeDtypeStruct((m, d), F32),
        scratch_shapes=[pltpu.VMEM((tm, d), BF16), pltpu.VMEM((tm, d), F32)],
        compiler_params=_cparams("arbitrary", "arbitrary"),
        name="ffn",
    )(x2, mod_s, g_pre.reshape(1, d), g_post.reshape(1, d), w13_all, w13_all, w2_all)


def _outproj_kernel(*refs, n_in, res_w):
    a_refs = refs[:n_in]
    w_refs = refs[n_in:2 * n_in]
    x_ref, mod_ref, gpost_ref, o_ref = refs[2 * n_in:]
    f = _dot(a_refs[0][...], w_refs[0][...])
    for a_ref, w_ref in zip(a_refs[1:], w_refs[1:]):
        f = f + _dot(a_ref[...], w_ref[...])
    o_ref[...] = _post_residual(x_ref[...], f, gpost_ref[...], mod_ref, res_w)


def _outproj(acts, w_out, x2, mod_s, g_post, seq, res_w):
    m, d = x2.shape
    tm = _pick(seq, 512)
    per_b = seq // tm
    n_in = len(acts)
    in_specs, args = [], []
    for a in acts:
        in_specs.append(pl.BlockSpec((tm, a.shape[1]), lambda i: (i, 0)))
        args.append(a)
    row = 0
    for a in acts:
        wi = a.shape[1]
        assert row % wi == 0
        in_specs.append(pl.BlockSpec((wi, d), lambda i, r=row // wi: (r, 0)))
        args.append(w_out)
        row += wi
    in_specs += [
        pl.BlockSpec((tm, d), lambda i: (i, 0)),
        pl.BlockSpec((1, 3, d), lambda i: (i // per_b, 0, 0)),
        pl.BlockSpec((1, d), lambda i: (0, 0)),
    ]
    args += [x2, mod_s, g_post.reshape(1, d)]
    return pl.pallas_call(
        functools.partial(_outproj_kernel, n_in=n_in, res_w=res_w),
        grid=(m // tm,),
        in_specs=in_specs,
        out_specs=pl.BlockSpec((tm, d), lambda i: (i, 0)),
        out_shape=jax.ShapeDtypeStruct((m, d), F32),
        compiler_params=_cparams("arbitrary"),
        name="mixer_outproj",
    )(*args)


def _conv_from_scratch(cs_ref, w_ref, rows, lo, hi):
    acc = cs_ref[SUBLANES:SUBLANES + rows, lo:hi] * w_ref[CONV_WIDTH - 1:CONV_WIDTH, lo:hi]
    for back in range(1, CONV_WIDTH):
        tap = CONV_WIDTH - 1 - back
        acc = acc + cs_ref[SUBLANES - back:SUBLANES - back + rows, lo:hi] * w_ref[tap:tap + 1, lo:hi]
    return acc


GDN_BLOCK = 256
GDN_PACK = 4


def _tile_rows(x, n):
    return jnp.concatenate([x] * n, axis=0)


def _gdn_block(p_ref, p_prev_ref, seq_start, between, cw_ref, alog_ref, dtb_ref, onorm_ref,
               o_ref, row0, s_scr, qn_scr, kn_scr, kb_scr, vb_scr, kbe_scr, kst_scr, egq_scr,
               a_scr, qk_scr, xo_scr, l_scr, ku_scr, oin_scr, *, n_heads, rows, n_pieces):
    dk = GDN_HEAD_DIM
    gw = n_heads * dk
    n_chunks = rows // CHUNK
    n_groups = n_heads // GDN_PACK
    pw = GDN_PACK * CHUNK
    gk = GDN_PACK * dk

    tail = p_prev_ref[rows:rows + SUBLANES, 0:3 * gw]
    p_ref[0:SUBLANES, 0:3 * gw] = jnp.where(seq_start, 0.0, tail)
    cs_scr = p_ref
    n_between = [0]

    def next_piece():
        between(n_between[0])
        n_between[0] += 1

    sm = p_ref[SUBLANES:SUBLANES + rows, 4 * gw:4 * gw + LANES]
    beta_all = _sigmoid(sm)
    g_all = -jnp.exp(alog_ref[...]) * _softplus(sm + dtb_ref[...])
    in_chunk = lax.broadcasted_iota(jnp.int32, (rows, LANES), 0) & (CHUNK - 1)
    d = 1
    while d < CHUNK:
        g_all = g_all + jnp.where(in_chunk >= d, pltpu.roll(g_all, d, 0), 0.0)
        d *= 2
    eg_all = jnp.exp(g_all)
    glast_all = jnp.concatenate(
        [jnp.broadcast_to(g_all[(c + 1) * CHUNK - 1:(c + 1) * CHUNK, :], (CHUNK, LANES))
         for c in range(n_chunks)], axis=0)
    ekl_all = jnp.exp(glast_all - g_all)

    for h in range(n_heads):
        next_piece()
        lo, hi = h * dk, (h + 1) * dk
        bcast = lambda arr, idx: jnp.broadcast_to(arr[:, idx:idx + 1], (rows, dk))
        beta_b = bcast(beta_all, h)
        eg_b = bcast(eg_all, n_heads + h)
        q = _silu(_conv_from_scratch(cs_scr, cw_ref, rows, lo, hi))
        qn = (q * lax.rsqrt(jnp.sum(q * q, axis=-1, keepdims=True) + EPS)) * dk ** -0.5
        qn_scr[:, lo:hi] = qn.astype(BF16)
        egq_scr[:, lo:hi] = qn * eg_b
        k = _silu(_conv_from_scratch(cs_scr, cw_ref, rows, gw + lo, gw + hi))
        kn = k * lax.rsqrt(jnp.sum(k * k, axis=-1, keepdims=True) + EPS)
        kb = kn * beta_b
        kn_scr[:, lo:hi] = kn.astype(BF16)
        kb_scr[:, lo:hi] = kb.astype(BF16)
        kbe_scr[:, lo:hi] = kb * eg_b
        kst_scr[:, lo:hi] = (kn * bcast(ekl_all, n_heads + h)).astype(BF16)
        v = _silu(_conv_from_scratch(cs_scr, cw_ref, rows, 2 * gw + lo, 2 * gw + hi))
        vb_scr[:, lo:hi] = v * beta_b

    ri = lax.broadcasted_iota(jnp.int32, (CHUNK, pw), 0)
    li = lax.broadcasted_iota(jnp.int32, (CHUNK, pw), 1)
    cj = li & (CHUNK - 1)
    lh = li // CHUNK
    causal = ri >= cj
    strict = ri > cj
    eye = ri == cj
    bd_r = lax.broadcasted_iota(jnp.int32, (pw, pw), 0) // CHUNK
    bd_c = lax.broadcasted_iota(jnp.int32, (pw, pw), 1) // CHUNK
    bd_sq = bd_r == bd_c
    bk_r = lax.broadcasted_iota(jnp.int32, (pw, gk), 0) // CHUNK
    bk_c = lax.broadcasted_iota(jnp.int32, (pw, gk), 1) // dk
    bd_k = bk_r == bk_c
    zero_sq = jnp.zeros((pw, pw), BF16)
    zero_k = jnp.zeros((pw, gk), BF16)

    def block_diag(x_rp):
        return jnp.where(bd_sq, _tile_rows(x_rp.astype(BF16), GDN_PACK), zero_sq)

    def block_diag_k(x_cat):
        return jnp.where(bd_k, _tile_rows(x_cat, GDN_PACK), zero_k)

    probs = [(c, gi) for c in range(n_chunks) for gi in range(n_groups)]
    for p, (c, gi) in enumerate(probs):
        next_piece()
        r0, r1 = c * CHUNK, (c + 1) * CHUNK
        c0 = gi * gk
        gcs = [jnp.broadcast_to(g_all[r0:r1, n_heads + gi * GDN_PACK + s:n_heads + gi * GDN_PACK + s + 1],
                                (CHUNK, pw)) for s in range(GDN_PACK)]
        gc = gcs[GDN_PACK - 1]
        for s in range(GDN_PACK - 2, -1, -1):
            gc = jnp.where(lh == s, gcs[s], gc)
        gr = jnp.sum(jnp.where(eye, gc, 0.0), axis=0, keepdims=True)
        decay = jnp.exp(jnp.where(causal, gc - gr, -jnp.inf))
        lhs = jnp.concatenate([kb_scr[r0:r1, c0:c0 + gk], qn_scr[r0:r1, c0:c0 + gk]], axis=0)
        kq = _dot_nt(lhs, block_diag_k(kn_scr[r0:r1, c0:c0 + gk]))
        a = jnp.where(strict, kq[0:CHUNK] * decay, 0.0)
        a_scr[p] = a
        qk_scr[p] = jnp.where(causal, kq[CHUNK:2 * CHUNK] * decay, 0.0)
        xo_scr[p] = -jnp.where((ri - cj == 1) & ((ri & 1) == 1), a, 0.0)

    next_piece()
    s = 2
    while s < CHUNK:
        rb = ri // s
        off = ((rb & 1) == 1) & ((cj // s) == rb - 1)
        for p in range(len(probs)):
            a_off = jnp.where(off, a_scr[p], 0.0)
            xo = xo_scr[p]
            y = a_off + _dot(xo.astype(BF16), block_diag(a_off))
            xo_scr[p] = xo - (y + _dot(y.astype(BF16), block_diag(xo)))
        s *= 2

    for p, (c, gi) in enumerate(probs):
        r0, r1 = c * CHUNK, (c + 1) * CHUNK
        rhs = jnp.concatenate(
            [jnp.concatenate([vb_scr[r0:r1, h * dk:(h + 1) * dk], kbe_scr[r0:r1, h * dk:(h + 1) * dk]], axis=1)
             for h in range(gi * GDN_PACK, (gi + 1) * GDN_PACK)], axis=0)
        sol = rhs + _dot(block_diag(xo_scr[p]), rhs.astype(BF16))
        sol_b = sol.astype(BF16)
        qks = _dot(block_diag(qk_scr[p]), sol_b)
        ks = _dot_tn(block_diag_k(kst_scr[r0:r1, gi * gk:(gi + 1) * gk]), sol_b)
        for s_ in range(GDN_PACK):
            h = gi * GDN_PACK + s_
            half = (h % 2) * dk
            rr = slice(s_ * CHUNK, (s_ + 1) * CHUNK)
            kr = slice(s_ * dk, (s_ + 1) * dk)
            q_eff = egq_scr[r0:r1, h * dk:(h + 1) * dk] - qks[rr, dk:2 * dk]
            l_scr[c, h // 2, 0:dk, half:half + dk] = ks[kr, dk:2 * dk].astype(BF16)
            l_scr[c, h // 2, dk:dk + CHUNK, half:half + dk] = q_eff.astype(BF16)
            ku_scr[c, h] = ks[kr, 0:dk]
            oin_scr[r0:r1, h * dk:(h + 1) * dk] = qks[rr, 0:dk]

    zero = jnp.zeros((dk, dk), BF16)
    for c in range(n_chunks):
        r0, r1 = c * CHUNK, (c + 1) * CHUNK
        for pr in range(n_heads // 2):
            h0, h1 = 2 * pr, 2 * pr + 1
            s0 = s_scr[h0]
            s1 = s_scr[h1]
            if c == 0:
                s0 = jnp.where(seq_start, 0.0, s0)
                s1 = jnp.where(seq_start, 0.0, s1)
            s_bd = jnp.concatenate(
                [jnp.concatenate([s0.astype(BF16), zero], axis=1),
                 jnp.concatenate([zero, s1.astype(BF16)], axis=1)], axis=0)
            r = _dot(l_scr[c, pr], s_bd)
            for h, st, off_ in ((h0, s0, 0), (h1, s1, dk)):
                cd = eg_all[r1 - 1:r1, n_heads + h:n_heads + h + 1]
                s_scr[h] = st * cd - r[0:dk, off_:off_ + dk] + ku_scr[c, h]
                o = r[dk:dk + CHUNK, off_:off_ + dk] + oin_scr[r0:r1, h * dk:(h + 1) * dk]
                z = p_ref[SUBLANES + r0:SUBLANES + r1, 3 * gw + h * dk:3 * gw + (h + 1) * dk]
                out = _rms(o, onorm_ref[...]) * _silu(z)
                o_ref[row0 + r0:row0 + r1, h * dk:(h + 1) * dk] = out.astype(o_ref.dtype)

    while n_between[0] < n_pieces:
        next_piece()


GDN_PIECES = 17


def _gdn_in_kernel(xa_ref, moda_ref, xb_ref, modb_ref, gpre_ref, win_ref,
                   cw_ref, alog_ref, dtb_ref, onorm_ref, o_ref, p0_scr, p1_scr, *scratch,
                   n_heads, rows, blocks_per_seq):
    s = pl.program_id(0)
    n_cols = win_ref.shape[1]
    piece = -(-n_cols // (GDN_PIECES * LANES)) * LANES
    bounds = [(c, min(c + piece, n_cols)) for c in range(0, n_cols, piece)]
    assert len(bounds) <= GDN_PIECES

    def in_proj_piece(h, p_ref, i):
        if i < len(bounds):
            c0, c1 = bounds[i]
            p_ref[SUBLANES:SUBLANES + rows, c0:c1] = _dot(h, win_ref[:, c0:c1])

    n_pieces = len(bounds)

    def normed(x_ref, mod_ref):
        return _norm_mod(x_ref[...], gpre_ref[...], mod_ref).astype(BF16)

    @pl.when(s == 0)
    def _():
        p0_scr[...] = jnp.zeros_like(p0_scr)
        p1_scr[...] = jnp.zeros_like(p1_scr)
        scratch[0][...] = jnp.zeros_like(scratch[0])

    seq_start = (2 * s - 2) % blocks_per_seq == 0
    ha = normed(xa_ref, moda_ref)
    _gdn_block(p0_scr, p1_scr, seq_start, lambda i: in_proj_piece(ha, p1_scr, i),
               cw_ref, alog_ref, dtb_ref, onorm_ref, o_ref, 0, *scratch, n_heads=n_heads, rows=rows,
               n_pieces=n_pieces)
    hb = normed(xb_ref, modb_ref)
    _gdn_block(p1_scr, p0_scr, False, lambda i: in_proj_piece(hb, p0_scr, i),
               cw_ref, alog_ref, dtb_ref, onorm_ref, o_ref, rows, *scratch, n_heads=n_heads, rows=rows,
               n_pieces=n_pieces)


def _gdn_in(x2, mod_s, g_pre, w_g, conv_w, a_log, dt_bias, o_norm, seq):
    m, d = x2.shape
    n_heads = a_log.shape[0]
    assert n_heads % GDN_PACK == 0 and GDN_HEAD_DIM == LANES
    gw = n_heads * GDN_HEAD_DIM
    n_cols = w_g.shape[1]
    assert n_cols == 4 * gw + LANES
    rows = _pick(seq, GDN_BLOCK)
    bps = seq // rows
    nb = m // rows
    assert bps % 2 == 0 and nb % 2 == 0
    last = nb - 1
    n_chunks = rows // CHUNK
    n_prob = n_chunks * (n_heads // GDN_PACK)
    pad = LANES - 2 * n_heads
    alog_row = jnp.pad(a_log, (n_heads, pad)).reshape(1, LANES)
    dtb_row = jnp.pad(dt_bias, (n_heads, pad)).reshape(1, LANES)
    xa = lambda s: jnp.maximum(2 * s - 1, 0)
    xb = lambda s: jnp.minimum(2 * s, last)
    return pl.pallas_call(
        functools.partial(_gdn_in_kernel, n_heads=n_heads, rows=rows, blocks_per_seq=bps),
        grid=(nb // 2 + 1,),
        in_specs=[
            pl.BlockSpec((rows, d), lambda s: (xa(s), 0)),
            pl.BlockSpec((1, 3, d), lambda s: (xa(s) // bps, 0, 0)),
            pl.BlockSpec((rows, d), lambda s: (xb(s), 0)),
            pl.BlockSpec((1, 3, d), lambda s: (xb(s) // bps, 0, 0)),
            pl.BlockSpec((1, d), lambda s: (0, 0)),
            pl.BlockSpec((d, n_cols), lambda s: (0, 0)),
            pl.BlockSpec((CONV_WIDTH, 3 * gw), lambda s: (0, 0)),
            pl.BlockSpec((1, LANES), lambda s: (0, 0)),
            pl.BlockSpec((1, LANES), lambda s: (0, 0)),
            pl.BlockSpec((1, GDN_HEAD_DIM), lambda s: (0, 0)),
        ],
        out_specs=pl.BlockSpec((2 * rows, gw), lambda s: (jnp.maximum(s - 1, 0), 0)),
        out_shape=jax.ShapeDtypeStruct((m, gw), BF16),
        scratch_shapes=[
            pltpu.VMEM((rows + SUBLANES, n_cols), F32),
            pltpu.VMEM((rows + SUBLANES, n_cols), F32),
            pltpu.VMEM((n_heads, GDN_HEAD_DIM, GDN_HEAD_DIM), F32),
            pltpu.VMEM((rows, gw), BF16),
            pltpu.VMEM((rows, gw), BF16),
            pltpu.VMEM((rows, gw), BF16),
            pltpu.VMEM((rows, gw), F32),
            pltpu.VMEM((rows, gw), F32),
            pltpu.VMEM((rows, gw), BF16),
            pltpu.VMEM((rows, gw), F32),
            pltpu.VMEM((n_prob, CHUNK, GDN_PACK * CHUNK), F32),
            pltpu.VMEM((n_prob, CHUNK, GDN_PACK * CHUNK), F32),
            pltpu.VMEM((n_prob, CHUNK, GDN_PACK * CHUNK), F32),
            pltpu.VMEM((n_chunks, n_heads // 2, GDN_HEAD_DIM + CHUNK, 2 * GDN_HEAD_DIM), BF16),
            pltpu.VMEM((n_chunks, n_heads, GDN_HEAD_DIM, GDN_HEAD_DIM), F32),
            pltpu.VMEM((rows, gw), F32),
        ],
        compiler_params=_cparams("arbitrary"),
        name="gdn_in",
    )(x2, mod_s, x2, mod_s, g_pre.reshape(1, d), w_g, conv_w, alog_row, dtb_row,
      o_norm.reshape(1, GDN_HEAD_DIM))


def _rotary(x, cos, sin):
    half = x.shape[-1] // 2
    x1, x2 = x[:, :half], x[:, half:]
    return jnp.concatenate([x1 * cos - x2 * sin, x2 * cos + x1 * sin], axis=-1)


RET_PIECES = 8


def _ret_block(p_ref, seq_start, between, cos, sin, norm_ref, o_ref, row0, s_scr, *, n_heads, rows,
               n_pieces):
    dk = RET_HEAD_DIM
    rw = n_heads * dk
    ri = lax.broadcasted_iota(jnp.int32, (rows, rows), 0)
    ci = lax.broadcasted_iota(jnp.int32, (rows, rows), 1)
    delta = (ri - ci).astype(F32)
    causal = ri >= ci
    pos = lax.broadcasted_iota(jnp.int32, (rows, 1), 0).astype(F32)
    done = 0

    for h in range(n_heads):
        while done < min(n_pieces, (h + 1) * -(-n_pieces // n_heads)):
            between(done)
            done += 1
        lo, hi = h * dk, (h + 1) * dk
        log_gamma = math.log1p(-(2.0 ** (-5.0 - h)))
        q = _rotary(p_ref[:, lo:hi], cos, sin)
        k = _rotary(p_ref[:, rw + lo:rw + hi], cos, sin) * dk ** -0.5
        v_b = p_ref[:, 2 * rw + lo:2 * rw + hi].astype(BF16)
        intra = jnp.exp(jnp.where(causal, delta * log_gamma, -jnp.inf))
        scores = _dot_nt(q.astype(BF16), k.astype(BF16)) * intra
        q_in = q * jnp.exp((pos + 1.0) * log_gamma)
        k_st = k * jnp.exp((rows - 1.0 - pos) * log_gamma)
        state = jnp.where(seq_start, 0.0, s_scr[h])
        o = _dot(scores.astype(BF16), v_b) + _dot(q_in.astype(BF16), state.astype(BF16))
        s_scr[h] = state * math.exp(rows * log_gamma) + _dot_tn(k_st.astype(BF16), v_b)
        out = _rms(o, norm_ref[:, lo:hi]) * _silu(p_ref[:, 3 * rw + lo:3 * rw + hi])
        o_ref[row0:row0 + rows, lo:hi] = out.astype(o_ref.dtype)

    while done < n_pieces:
        between(done)
        done += 1


def _ret_in_kernel(xa_ref, moda_ref, xb_ref, modb_ref, gpre_ref, win_ref, cos_ref, sin_ref, norm_ref,
                   o_ref, p0_scr, p1_scr, s_scr, *, n_heads, rows, blocks_per_seq):
    s = pl.program_id(0)
    piece = win_ref.shape[1] // RET_PIECES

    def in_proj_piece(h, p_ref, i):
        p_ref[:, i * piece:(i + 1) * piece] = _dot(h, win_ref[:, i * piece:(i + 1) * piece])

    def normed(x_ref, mod_ref):
        return _norm_mod(x_ref[...], gpre_ref[...], mod_ref).astype(BF16)

    @pl.when(s == 0)
    def _():
        p0_scr[...] = jnp.zeros_like(p0_scr)
        p1_scr[...] = jnp.zeros_like(p1_scr)
        s_scr[...] = jnp.zeros_like(s_scr)

    seq_start = (2 * s - 2) % blocks_per_seq == 0
    ha = normed(xa_ref, moda_ref)
    _ret_block(p0_scr, seq_start, lambda i: in_proj_piece(ha, p1_scr, i), cos_ref[0:rows, :],
               sin_ref[0:rows, :], norm_ref, o_ref, 0, s_scr, n_heads=n_heads, rows=rows,
               n_pieces=RET_PIECES)
    hb = normed(xb_ref, modb_ref)
    _ret_block(p1_scr, False, lambda i: in_proj_piece(hb, p0_scr, i), cos_ref[rows:2 * rows, :],
               sin_ref[rows:2 * rows, :], norm_ref, o_ref, rows, s_scr, n_heads=n_heads, rows=rows,
               n_pieces=RET_PIECES)


def _ret_in(x2, mod_s, g_pre, w_r, ret_norm, seq):
    m, d = x2.shape
    rw = ret_norm.shape[0]
    n_heads = rw // RET_HEAD_DIM
    n_cols = w_r.shape[1]
    assert n_cols == 4 * rw and n_cols % (RET_PIECES * LANES) == 0
    rows = _pick(seq, 256)
    bps = seq // rows
    nb = m // rows
    assert bps % 2 == 0 and nb % 2 == 0
    last = nb - 1
    half = RET_HEAD_DIM // 2
    inv_freq = ROPE_BASE ** (-jnp.arange(half, dtype=F32) / half)
    ang = jnp.arange(seq, dtype=F32)[:, None] * inv_freq[None, :]
    cos, sin = jnp.cos(ang), jnp.sin(ang)
    xa = lambda s: jnp.maximum(2 * s - 1, 0)
    xb = lambda s: jnp.minimum(2 * s, last)
    pos_blk = lambda s: (jnp.maximum(2 * s - 2, 0) % bps) // 2
    return pl.pallas_call(
        functools.partial(_ret_in_kernel, n_heads=n_heads, rows=rows, blocks_per_seq=bps),
        grid=(nb // 2 + 1,),
        in_specs=[
            pl.BlockSpec((rows, d), lambda s: (xa(s), 0)),
            pl.BlockSpec((1, 3, d), lambda s: (xa(s) // bps, 0, 0)),
            pl.BlockSpec((rows, d), lambda s: (xb(s), 0)),
            pl.BlockSpec((1, 3, d), lambda s: (xb(s) // bps, 0, 0)),
            pl.BlockSpec((1, d), lambda s: (0, 0)),
            pl.BlockSpec((d, n_cols), lambda s: (0, 0)),
            pl.BlockSpec((2 * rows, half), lambda s: (pos_blk(s), 0)),
            pl.BlockSpec((2 * rows, half), lambda s: (pos_blk(s), 0)),
            pl.BlockSpec((1, rw), lambda s: (0, 0)),
        ],
        out_specs=pl.BlockSpec((2 * rows, rw), lambda s: (jnp.maximum(s - 1, 0), 0)),
        out_shape=jax.ShapeDtypeStruct((m, rw), BF16),
        scratch_shapes=[
            pltpu.VMEM((rows, n_cols), F32),
            pltpu.VMEM((rows, n_cols), F32),
            pltpu.VMEM((n_heads, RET_HEAD_DIM, RET_HEAD_DIM), F32),
        ],
        compiler_params=_cparams("arbitrary"),
        name="retention_in",
    )(x2, mod_s, x2, mod_s, g_pre.reshape(1, d), w_r, cos, sin, ret_norm.reshape(1, rw))


LRU_ROWS = 256


def _lru_block(p_ref, p_prev_ref, seq_start, between, perm_ref, cw_ref, cb_ref, wa_ref, ba_ref, wx_ref,
               bx_ref, lam_ref, o_ref, row0, h_scr, act_scr, *, n_blocks, rows, bw):
    width = n_blocks * bw
    nj = rows // SUBLANES
    sub = lax.broadcasted_iota(jnp.int32, (SUBLANES, bw), 0)
    tails = [jnp.where(seq_start, 0.0, p_prev_ref[rows - SUBLANES * (k - 1) - 1:rows - SUBLANES * (k - 1),
                                                  width:2 * width]) for k in range(1, CONV_WIDTH)]

    for n in range(n_blocks):
        between(2 * n)
        lo, hi = n * bw, (n + 1) * bw
        xp = p_ref[:, width + lo:width + hi]

        def edge(k):
            prev = pltpu.roll(xp[rows - k * SUBLANES:rows - (k - 1) * SUBLANES], 1, 0)
            return jnp.where(sub == 0, tails[k - 1][:, lo:hi], prev)

        edges = [edge(k) for k in range(1, CONV_WIDTH)]
        xc = xp * cw_ref[CONV_WIDTH - 1:CONV_WIDTH, lo:hi]
        for back in range(1, CONV_WIDTH):
            tap = CONV_WIDTH - 1 - back
            shifted = jnp.concatenate(edges[:back][::-1] + [xp[:rows - back * SUBLANES]], axis=0)
            xc = xc + shifted * cw_ref[tap:tap + 1, lo:hi]
        xc = xc + cb_ref[:, lo:hi]
        xc_b = xc.astype(BF16)
        r = _sigmoid(_dot(xc_b, wa_ref[n]) + ba_ref[:, lo:hi])
        i = _sigmoid(_dot(xc_b, wx_ref[n]) + bx_ref[:, lo:hi])
        log_a = (-LRU_C * r) * _softplus(-lam_ref[:, lo:hi])
        a = jnp.exp(log_a)
        mult = jnp.sqrt(-jnp.tanh(log_a) * (a * a + 1.0))
        bv = mult * (i * xc)

        h = bv[0:SUBLANES]
        ac = a[0:SUBLANES]
        hl, al = [h], [ac]
        for j in range(1, nj):
            aj = a[j * SUBLANES:(j + 1) * SUBLANES]
            h = aj * h + bv[j * SUBLANES:(j + 1) * SUBLANES]
            ac = aj * ac
            hl.append(h)
            al.append(ac)
        between(2 * n + 1)
        c = jnp.where(seq_start, 0.0, h_scr[0:1, lo:hi])
        cs = []
        for s in range(SUBLANES):
            cs.append(c)
            c = h[s:s + 1] + ac[s:s + 1] * c
        h_scr[0:1, lo:hi] = c
        c_in = jnp.concatenate(cs, axis=0)
        hs = jnp.concatenate([hl[j] + al[j] * c_in for j in range(nj)], axis=0)
        act_scr[:, lo:hi] = (hs * _gelu_tanh(p_ref[:, lo:hi])).astype(BF16)

    o_ref[row0:row0 + rows, :] = _dot_tn(perm_ref[...], act_scr[...]).astype(o_ref.dtype)


def _lru_in_kernel(x0_ref, mod0_ref, xa_ref, moda_ref, xb_ref, modb_ref, gpre_ref, win_ref, perm_ref,
                   cw_ref, cb_ref, wa_ref, ba_ref, wx_ref, bx_ref, lam_ref, o_ref,
                   p0_scr, p1_scr, h_scr, act_scr, *, n_blocks, rows, bw, blocks_per_seq):
    s = pl.program_id(0)

    def in_proj_cols(h, p_ref, n):
        cw = win_ref.shape[1] // (2 * n_blocks)
        p_ref[:, n * cw:(n + 1) * cw] = _dot(h, win_ref[:, n * cw:(n + 1) * cw])

    def normed(x_ref, mod_ref):
        h = _norm_mod(x_ref[...], gpre_ref[...], mod_ref).astype(BF16)
        return _dot(perm_ref[...], h).astype(BF16)

    @pl.when(s == 0)
    def _():
        h0 = normed(x0_ref, mod0_ref)
        for n in range(2 * n_blocks):
            in_proj_cols(h0, p0_scr, n)
        p1_scr[...] = jnp.zeros_like(p1_scr)
        h_scr[...] = jnp.zeros_like(h_scr)

    lru = functools.partial(_lru_block, perm_ref=perm_ref, cw_ref=cw_ref, cb_ref=cb_ref, wa_ref=wa_ref,
                            ba_ref=ba_ref, wx_ref=wx_ref, bx_ref=bx_ref, lam_ref=lam_ref, o_ref=o_ref,
                            h_scr=h_scr, act_scr=act_scr, n_blocks=n_blocks, rows=rows, bw=bw)
    seq_start = (2 * s) % blocks_per_seq == 0
    ha = normed(xa_ref, moda_ref)
    lru(p0_scr, p1_scr, seq_start, lambda n: in_proj_cols(ha, p1_scr, n), row0=0)
    hb = normed(xb_ref, modb_ref)
    lru(p1_scr, p0_scr, False, lambda n: in_proj_cols(hb, p0_scr, n), row0=rows)


def _lru_in(x2, mod_s, g_pre, w_in, conv_w, conv_b, gate_a_w, gate_a_b, gate_x_w, gate_x_b, lam, seq):
    m, d = x2.shape
    width = w_in.shape[1] // 2
    n_blocks, bw, _ = gate_a_w.shape
    rows = _pick(seq, LRU_ROWS)
    bps = seq // rows
    nb = m // rows
    assert bps % 2 == 0 and nb % 2 == 0
    last = nb - 1
    r_idx = jnp.arange(rows)
    t_of_r = (r_idx % SUBLANES) * (rows // SUBLANES) + r_idx // SUBLANES
    perm = (t_of_r[:, None] == r_idx[None, :]).astype(BF16)
    xa = lambda s: 2 * s + 1
    xb = lambda s: jnp.minimum(2 * s + 2, last)
    vec = lambda: pl.BlockSpec((1, width), lambda s: (0, 0))
    gate = lambda: pl.BlockSpec((n_blocks, bw, bw), lambda s: (0, 0, 0))
    return pl.pallas_call(
        functools.partial(_lru_in_kernel, n_blocks=n_blocks, rows=rows, bw=bw, blocks_per_seq=bps),
        grid=(nb // 2,),
        in_specs=[
            pl.BlockSpec((rows, d), lambda s: (0, 0)),
            pl.BlockSpec((1, 3, d), lambda s: (0, 0, 0)),
            pl.BlockSpec((rows, d), lambda s: (xa(s), 0)),
            pl.BlockSpec((1, 3, d), lambda s: (xa(s) // bps, 0, 0)),
            pl.BlockSpec((rows, d), lambda s: (xb(s), 0)),
            pl.BlockSpec((1, 3, d), lambda s: (xb(s) // bps, 0, 0)),
            pl.BlockSpec((1, d), lambda s: (0, 0)),
            pl.BlockSpec((d, 2 * width), lambda s: (0, 0)),
            pl.BlockSpec((rows, rows), lambda s: (0, 0)),
            pl.BlockSpec((CONV_WIDTH, width), lambda s: (0, 0)),
            vec(), gate(), vec(), gate(), vec(), vec(),
        ],
        out_specs=pl.BlockSpec((2 * rows, width), lambda s: (s, 0)),
        out_shape=jax.ShapeDtypeStruct((m, width), BF16),
        scratch_shapes=[
            pltpu.VMEM((rows, 2 * width), F32),
            pltpu.VMEM((rows, 2 * width), F32),
            pltpu.VMEM((SUBLANES, width), F32),
            pltpu.VMEM((rows, width), BF16),
        ],
        compiler_params=_cparams("arbitrary"),
        name="rglru_in",
    )(x2, mod_s, x2, mod_s, x2, mod_s, g_pre.reshape(1, d), w_in, perm, conv_w, conv_b.reshape(1, width),
      gate_a_w.astype(BF16), gate_a_b.reshape(1, width), gate_x_w.astype(BF16),
      gate_x_b.reshape(1, width), lam.reshape(1, width))


def kernel(x, c, ada_w, ada_b, norm_pre, norm_post, ffn_w13, ffn_w2, ev_w_in, ev_conv_w, ev_a_log,
           ev_dt_bias, ev_o_norm, ev_ret_norm, ev_w_out, od_w_in, od_conv_w, od_conv_b,
           od_gate_a_w, od_gate_a_b, od_gate_x_w, od_gate_x_b, od_lambda, od_w_out):
    b, seq, d = x.shape
    depth = ada_w.shape[0]
    m = b * seq
    mod = _ada(c, ada_w, ada_b).reshape(depth, b, N_SUB, 3, d)
    x2 = x.reshape(m, d)
    w13_b = ffn_w13.astype(BF16)
    w2_b = ffn_w2.astype(BF16)

    for layer in range(depth):
        mod_l = mod[layer]
        x2 = _ffn(x2, mod_l[:, 0], norm_pre[layer, 0], norm_post[layer, 0], w13_b, w2_b, layer, 0, seq, 0.5)
        if layer % 2 == 0:
            e = layer // 2
            n_heads = ev_a_log.shape[1]
            gw = n_heads * GDN_HEAD_DIM
            w_in = ev_w_in[e]
            w_small = jnp.pad(w_in[:, 3 * gw:3 * gw + 2 * n_heads], ((0, 0), (0, LANES - 2 * n_heads)))
            w_g = jnp.concatenate([w_in[:, :3 * gw], w_in[:, 3 * gw + 2 * n_heads:4 * gw + 2 * n_heads],
                                   w_small], axis=1).astype(BF16)
            w_r = w_in[:, 4 * gw + 2 * n_heads:].astype(BF16)
            o_a = _gdn_in(x2, mod_l[:, 1], norm_pre[layer, 1], w_g, ev_conv_w[e], ev_a_log[e],
                          ev_dt_bias[e], ev_o_norm[e], seq)
            o_b = _ret_in(x2, mod_l[:, 1], norm_pre[layer, 1], w_r, ev_ret_norm[e], seq)
            acts = [o_a, o_b]
            w_out = ev_w_out[e]
        else:
            o = layer // 2
            hs = _lru_in(x2, mod_l[:, 1], norm_pre[layer, 1], od_w_in[o].astype(BF16), od_conv_w[o],
                         od_conv_b[o], od_gate_a_w[o], od_gate_a_b[o], od_gate_x_w[o], od_gate_x_b[o],
                         od_lambda[o], seq)
            acts = [hs.reshape(m, -1)]
            w_out = od_w_out[o]
        x2 = _outproj(acts, w_out.astype(BF16), x2, mod_l[:, 1], norm_post[layer, 1], seq, 1.0)
        x2 = _ffn(x2, mod_l[:, 2], norm_pre[layer, 2], norm_post[layer, 2], w13_b, w2_b, layer, 1, seq, 0.5)
    return x2.reshape(b, seq, d)
```

```python
import functools
import math

import jax
import jax.numpy as jnp
from jax import lax
from jax.experimental import pallas as pl
from jax.experimental.pallas import tpu as pltpu

F32 = jnp.float32
BF16 = jnp.bfloat16

EPS = 1e-6
GDN_HEAD_DIM = 128
RET_HEAD_DIM = 256
CHUNK = 64
CONV_WIDTH = 4
ROPE_BASE = 10000.0
LRU_C = 8.0
N_SUB = 3

SUBLANES = 8
LANES = 128
V7X_VMEM_LIMIT_BYTES = 56 * 1024 * 1024


def _cparams(*semantics):
    return pltpu.CompilerParams(dimension_semantics=semantics,
                                vmem_limit_bytes=V7X_VMEM_LIMIT_BYTES)


def _dot(a, b):
    return jnp.dot(a, b, preferred_element_type=F32)


def _dot_nt(a, b):
    return lax.dot_general(a, b, (((1,), (1,)), ((), ())), preferred_element_type=F32)


def _dot_tn(a, b):
    return lax.dot_general(a, b, (((0,), (0,)), ((), ())), preferred_element_type=F32)


def _sigmoid(x):
    return jax.nn.sigmoid(x)


def _silu(x):
    return x * jax.nn.sigmoid(x)


def _softplus(x):
    return jnp.maximum(x, 0.0) + jnp.log1p(jnp.exp(-jnp.abs(x)))


def _gelu_tanh(x):
    c = math.sqrt(2.0 / math.pi)
    return 0.5 * x * (1.0 + jnp.tanh(c * (x + 0.044715 * (x * x * x))))


def _rms(x, gain):
    return x * lax.rsqrt(jnp.mean(x * x, axis=-1, keepdims=True) + EPS) * gain


def _inv_rms(x):
    return lax.rsqrt(jnp.mean(x * x, axis=-1, keepdims=True) + EPS)


def _norm_mod(x, g_pre, mod_ref):
    return (x * _inv_rms(x)) * (g_pre * (1.0 + mod_ref[0, 1:2, :])) + mod_ref[0, 0:1, :]


def _post_residual(x, f, g_post, mod_ref, res_w):
    return x + (f * _inv_rms(f)) * (g_post * (res_w * (1.0 + mod_ref[0, 2:3, :])))


def _pick(n, pref):
    if n <= pref:
        return n
    for t in range(pref - pref % LANES, 0, -LANES):
        if n % t == 0:
            return t
    raise ValueError(f"no lane-aligned tile of {n} at or below {pref}")


def _ada_kernel(c_ref, w_ref, b_ref, o_ref):
    a = _silu(c_ref[...]).astype(BF16)
    o_ref[0] = _dot(a, w_ref[0].astype(BF16)) + b_ref[0]


def _ada(c, ada_w, ada_b):
    n_layers, d, n = ada_w.shape
    b = c.shape[0]
    tn = _pick(n, 1024)
    return pl.pallas_call(
        _ada_kernel,
        grid=(n_layers, n // tn),
        in_specs=[
            pl.BlockSpec((b, d), lambda l, j: (0, 0)),
            pl.BlockSpec((1, d, tn), lambda l, j: (l, 0, j)),
            pl.BlockSpec((1, 1, tn), lambda l, j: (l, 0, j)),
        ],
        out_specs=pl.BlockSpec((1, b, tn), lambda l, j: (l, 0, j)),
        out_shape=jax.ShapeDtypeStruct((n_layers, b, n), F32),
        compiler_params=_cparams("arbitrary", "arbitrary"),
        name="ada_mod",
    )(c, ada_w, ada_b.reshape(n_layers, 1, n))


FFN_HALVES = 2


def _ffn_kernel(x_ref, mod_ref, gpre_ref, gpost_ref, w1_ref, w3_ref, w2_ref, o_ref,
                h_scr, acc_scr, *, res_w, n_f):
    j = pl.program_id(1)
    tm = x_ref.shape[0]
    part = tm // FFN_HALVES

    def down(rs):
        h = h_scr[rs, :]
        a = (_silu(_dot(h, w1_ref[...])) * _dot(h, w3_ref[...])).astype(BF16)
        return _dot(a, w2_ref[...])

    @pl.when(j == 0)
    def _():
        for p in range(FFN_HALVES):
            rs = slice(p * part, (p + 1) * part)
            h_scr[rs, :] = _norm_mod(x_ref[rs, :], gpre_ref[...], mod_ref).astype(BF16)
            acc_scr[rs, :] = down(rs)

    @pl.when((j > 0) & (j < n_f - 1))
    def _():
        acc_scr[...] += down(slice(None))

    @pl.when(j == n_f - 1)
    def _():
        for p in range(FFN_HALVES):
            rs = slice(p * part, (p + 1) * part)
            f = acc_scr[rs, :] + down(rs)
            o_ref[rs, :] = _post_residual(x_ref[rs, :], f, gpost_ref[...], mod_ref, res_w)


def _ffn(x2, mod_s, g_pre, g_post, w13_all, w2_all, layer, which, seq, res_w):
    m, d = x2.shape
    f = w2_all.shape[2]
    tm = _pick(seq, 512)
    tf = _pick(f, 512)
    n_f = f // tf
    per_b = seq // tm
    assert n_f >= 2 and tm % (FFN_HALVES * 2 * SUBLANES) == 0
    return pl.pallas_call(
        functools.partial(_ffn_kernel, res_w=res_w, n_f=n_f),
        grid=(m // tm, n_f),
        in_specs=[
            pl.BlockSpec((tm, d), lambda i, j: (i, 0)),
            pl.BlockSpec((1, 3, d), lambda i, j: (i // per_b, 0, 0)),
            pl.BlockSpec((1, d), lambda i, j: (0, 0)),
            pl.BlockSpec((1, d), lambda i, j: (0, 0)),
            pl.BlockSpec((None, None, d, tf), lambda i, j: (layer, which, 0, j)),
            pl.BlockSpec((None, None, d, tf), lambda i, j: (layer, which, 0, j + n_f)),
            pl.BlockSpec((None, None, tf, d), lambda i, j: (layer, which, j, 0)),
        ],
        out_specs=pl.BlockSpec((tm, d), lambda i, j: (i, 0)),
        out_shape=jax.ShapeDtypeStruct((m, d), F32),
        scratch_shapes=[pltpu.VMEM((tm, d), BF16), pltpu.VMEM((tm, d), F32)],
        compiler_params=_cparams("arbitrary", "arbitrary"),
        name="ffn",
    )(x2, mod_s, g_pre.reshape(1, d), g_post.reshape(1, d), w13_all, w13_all, w2_all)


OUTPROJ_HALVES = 2


def _outproj_kernel(*refs, n_in, res_w):
    a_refs = refs[:n_in]
    w_refs = refs[n_in:2 * n_in]
    x_ref, mod_ref, gpost_ref, o_ref = refs[2 * n_in:]
    part = x_ref.shape[0] // OUTPROJ_HALVES
    for p in range(OUTPROJ_HALVES):
        rs = slice(p * part, (p + 1) * part)
        f = _dot(a_refs[0][rs, :], w_refs[0][...])
        for a_ref, w_ref in zip(a_refs[1:], w_refs[1:]):
            f = f + _dot(a_ref[rs, :], w_ref[...])
        o_ref[rs, :] = _post_residual(x_ref[rs, :], f, gpost_ref[...], mod_ref, res_w)


def _outproj(acts, w_out, x2, mod_s, g_post, seq, res_w):
    m, d = x2.shape
    tm = _pick(seq, 512)
    per_b = seq // tm
    n_in = len(acts)
    in_specs, args = [], []
    for a in acts:
        in_specs.append(pl.BlockSpec((tm, a.shape[1]), lambda i: (i, 0)))
        args.append(a)
    row = 0
    for a in acts:
        wi = a.shape[1]
        assert row % wi == 0
        in_specs.append(pl.BlockSpec((wi, d), lambda i, r=row // wi: (r, 0)))
        args.append(w_out)
        row += wi
    in_specs += [
        pl.BlockSpec((tm, d), lambda i: (i, 0)),
        pl.BlockSpec((1, 3, d), lambda i: (i // per_b, 0, 0)),
        pl.BlockSpec((1, d), lambda i: (0, 0)),
    ]
    args += [x2, mod_s, g_post.reshape(1, d)]
    return pl.pallas_call(
        functools.partial(_outproj_kernel, n_in=n_in, res_w=res_w),
        grid=(m // tm,),
        in_specs=in_specs,
        out_specs=pl.BlockSpec((tm, d), lambda i: (i, 0)),
        out_shape=jax.ShapeDtypeStruct((m, d), F32),
        compiler_params=_cparams("arbitrary"),
        name="mixer_outproj",
    )(*args)


def _conv_from_scratch(cs_ref, w_ref, rows, lo, hi):
    acc = cs_ref[SUBLANES:SUBLANES + rows, lo:hi] * w_ref[CONV_WIDTH - 1:CONV_WIDTH, lo:hi]
    for back in range(1, CONV_WIDTH):
        tap = CONV_WIDTH - 1 - back
        acc = acc + cs_ref[SUBLANES - back:SUBLANES - back + rows, lo:hi] * w_ref[tap:tap + 1, lo:hi]
    return acc


GDN_BLOCK = 256
GDN_PACK = 4


def _tile_rows(x, n):
    return jnp.concatenate([x] * n, axis=0)


def _gdn_block(p_ref, p_prev_ref, seq_start, between, cw_ref, alog_ref, dtb_ref, onorm_ref,
               o_ref, row0, s_scr, qn_scr, kn_scr, kb_scr, vb_scr, kbe_scr, kst_scr, egq_scr,
               a_scr, qk_scr, xo_scr, l_scr, ku_scr, oin_scr, *, n_heads, rows, n_pieces):
    dk = GDN_HEAD_DIM
    gw = n_heads * dk
    n_chunks = rows // CHUNK
    n_groups = n_heads // GDN_PACK
    pw = GDN_PACK * CHUNK
    gk = GDN_PACK * dk

    tail = p_prev_ref[rows:rows + SUBLANES, 0:3 * gw]
    p_ref[0:SUBLANES, 0:3 * gw] = jnp.where(seq_start, 0.0, tail)
    cs_scr = p_ref
    n_between = [0]

    def next_piece():
        between(n_between[0])
        n_between[0] += 1

    sm = p_ref[SUBLANES:SUBLANES + rows, 4 * gw:4 * gw + LANES]
    beta_all = _sigmoid(sm)
    g_all = -jnp.exp(alog_ref[...]) * _softplus(sm + dtb_ref[...])
    in_chunk = lax.broadcasted_iota(jnp.int32, (rows, LANES), 0) & (CHUNK - 1)
    d = 1
    while d < CHUNK:
        g_all = g_all + jnp.where(in_chunk >= d, pltpu.roll(g_all, d, 0), 0.0)
        d *= 2
    eg_all = jnp.exp(g_all)
    glast_all = jnp.concatenate(
        [jnp.broadcast_to(g_all[(c + 1) * CHUNK - 1:(c + 1) * CHUNK, :], (CHUNK, LANES))
         for c in range(n_chunks)], axis=0)
    ekl_all = jnp.exp(glast_all - g_all)

    for h in range(n_heads):
        next_piece()
        lo, hi = h * dk, (h + 1) * dk
        bcast = lambda arr, idx: jnp.broadcast_to(arr[:, idx:idx + 1], (rows, dk))
        beta_b = bcast(beta_all, h)
        eg_b = bcast(eg_all, n_heads + h)
        q = _silu(_conv_from_scratch(cs_scr, cw_ref, rows, lo, hi))
        qn = (q * lax.rsqrt(jnp.sum(q * q, axis=-1, keepdims=True) + EPS)) * dk ** -0.5
        qn_scr[:, lo:hi] = qn.astype(BF16)
        egq_scr[:, lo:hi] = qn * eg_b
        k = _silu(_conv_from_scratch(cs_scr, cw_ref, rows, gw + lo, gw + hi))
        kn = k * lax.rsqrt(jnp.sum(k * k, axis=-1, keepdims=True) + EPS)
        kb = kn * beta_b
        kn_scr[:, lo:hi] = kn.astype(BF16)
        kb_scr[:, lo:hi] = kb.astype(BF16)
        kbe_scr[:, lo:hi] = kb * eg_b
        kst_scr[:, lo:hi] = (kn * bcast(ekl_all, n_heads + h)).astype(BF16)
        v = _silu(_conv_from_scratch(cs_scr, cw_ref, rows, 2 * gw + lo, 2 * gw + hi))
        vb_scr[:, lo:hi] = v * beta_b

    ri = lax.broadcasted_iota(jnp.int32, (CHUNK, pw), 0)
    li = lax.broadcasted_iota(jnp.int32, (CHUNK, pw), 1)
    cj = li & (CHUNK - 1)
    lh = li // CHUNK
    causal = ri >= cj
    strict = ri > cj
    eye = ri == cj
    bd_r = lax.broadcasted_iota(jnp.int32, (pw, pw), 0) // CHUNK
    bd_c = lax.broadcasted_iota(jnp.int32, (pw, pw), 1) // CHUNK
    bd_sq = bd_r == bd_c
    bk_r = lax.broadcasted_iota(jnp.int32, (pw, gk), 0) // CHUNK
    bk_c = lax.broadcasted_iota(jnp.int32, (pw, gk), 1) // dk
    bd_k = bk_r == bk_c
    zero_sq = jnp.zeros((pw, pw), BF16)
    zero_k = jnp.zeros((pw, gk), BF16)

    def block_diag(x_rp):
        return jnp.where(bd_sq, _tile_rows(x_rp.astype(BF16), GDN_PACK), zero_sq)

    def block_diag_k(x_cat):
        return jnp.where(bd_k, _tile_rows(x_cat, GDN_PACK), zero_k)

    probs = [(c, gi) for c in range(n_chunks) for gi in range(n_groups)]
    for p, (c, gi) in enumerate(probs):
        next_piece()
        r0, r1 = c * CHUNK, (c + 1) * CHUNK
        c0 = gi * gk
        gcs = [jnp.broadcast_to(g_all[r0:r1, n_heads + gi * GDN_PACK + s:n_heads + gi * GDN_PACK + s + 1],
                                (CHUNK, pw)) for s in range(GDN_PACK)]
        gc = gcs[GDN_PACK - 1]
        for s in range(GDN_PACK - 2, -1, -1):
            gc = jnp.where(lh == s, gcs[s], gc)
        gr = jnp.sum(jnp.where(eye, gc, 0.0), axis=0, keepdims=True)
        decay = jnp.exp(jnp.where(causal, gc - gr, -jnp.inf))
        lhs = jnp.concatenate([kb_scr[r0:r1, c0:c0 + gk], qn_scr[r0:r1, c0:c0 + gk]], axis=0)
        kq = _dot_nt(lhs, block_diag_k(kn_scr[r0:r1, c0:c0 + gk]))
        a = jnp.where(strict, kq[0:CHUNK] * decay, 0.0)
        a_scr[p] = a
        qk_scr[p] = jnp.where(causal, kq[CHUNK:2 * CHUNK] * decay, 0.0)
        xo_scr[p] = -jnp.where((ri - cj == 1) & ((ri & 1) == 1), a, 0.0)

    next_piece()
    s = 2
    while s < CHUNK:
        rb = ri // s
        off = ((rb & 1) == 1) & ((cj // s) == rb - 1)
        for p in range(len(probs)):
            a_off = jnp.where(off, a_scr[p], 0.0)
            xo = xo_scr[p]
            y = a_off + _dot(xo.astype(BF16), block_diag(a_off))
            xo_scr[p] = xo - (y + _dot(y.astype(BF16), block_diag(xo)))
        s *= 2

    for p, (c, gi) in enumerate(probs):
        r0, r1 = c * CHUNK, (c + 1) * CHUNK
        rhs = jnp.concatenate(
            [jnp.concatenate([vb_scr[r0:r1, h * dk:(h + 1) * dk], kbe_scr[r0:r1, h * dk:(h + 1) * dk]], axis=1)
             for h in range(gi * GDN_PACK, (gi + 1) * GDN_PACK)], axis=0)
        sol = rhs + _dot(block_diag(xo_scr[p]), rhs.astype(BF16))
        sol_b = sol.astype(BF16)
        qks = _dot(block_diag(qk_scr[p]), sol_b)
        ks = _dot_tn(block_diag_k(kst_scr[r0:r1, gi * gk:(gi + 1) * gk]), sol_b)
        for s_ in range(GDN_PACK):
            h = gi * GDN_PACK + s_
            half = (h % 2) * dk
            rr = slice(s_ * CHUNK, (s_ + 1) * CHUNK)
            kr = slice(s_ * dk, (s_ + 1) * dk)
            q_eff = egq_scr[r0:r1, h * dk:(h + 1) * dk] - qks[rr, dk:2 * dk]
            l_scr[c, h // 2, 0:dk, half:half + dk] = ks[kr, dk:2 * dk].astype(BF16)
            l_scr[c, h // 2, dk:dk + CHUNK, half:half + dk] = q_eff.astype(BF16)
            ku_scr[c, h] = ks[kr, 0:dk]
            oin_scr[r0:r1, h * dk:(h + 1) * dk] = qks[rr, 0:dk]

    zero = jnp.zeros((dk, dk), BF16)
    for c in range(n_chunks):
        r0, r1 = c * CHUNK, (c + 1) * CHUNK
        for pr in range(n_heads // 2):
            h0, h1 = 2 * pr, 2 * pr + 1
            s0 = s_scr[h0]
            s1 = s_scr[h1]
            if c == 0:
                s0 = jnp.where(seq_start, 0.0, s0)
                s1 = jnp.where(seq_start, 0.0, s1)
            s_bd = jnp.concatenate(
                [jnp.concatenate([s0.astype(BF16), zero], axis=1),
                 jnp.concatenate([zero, s1.astype(BF16)], axis=1)], axis=0)
            r = _dot(l_scr[c, pr], s_bd)
            for h, st, off_ in ((h0, s0, 0), (h1, s1, dk)):
                cd = eg_all[r1 - 1:r1, n_heads + h:n_heads + h + 1]
                s_scr[h] = st * cd - r[0:dk, off_:off_ + dk] + ku_scr[c, h]
                o = r[dk:dk + CHUNK, off_:off_ + dk] + oin_scr[r0:r1, h * dk:(h + 1) * dk]
                z = p_ref[SUBLANES + r0:SUBLANES + r1, 3 * gw + h * dk:3 * gw + (h + 1) * dk]
                out = _rms(o, onorm_ref[...]) * _silu(z)
                o_ref[row0 + r0:row0 + r1, h * dk:(h + 1) * dk] = out.astype(o_ref.dtype)

    while n_between[0] < n_pieces:
        next_piece()


GDN_PIECES = 17


def _gdn_in_kernel(xa_ref, moda_ref, xb_ref, modb_ref, gpre_ref, win_ref,
                   cw_ref, alog_ref, dtb_ref, onorm_ref, o_ref, p0_scr, p1_scr, *scratch,
                   n_heads, rows, blocks_per_seq):
    s = pl.program_id(0)
    n_cols = win_ref.shape[1]
    piece = -(-n_cols // (GDN_PIECES * LANES)) * LANES
    bounds = [(c, min(c + piece, n_cols)) for c in range(0, n_cols, piece)]
    assert len(bounds) <= GDN_PIECES

    def in_proj_piece(h, p_ref, i):
        if i < len(bounds):
            c0, c1 = bounds[i]
            p_ref[SUBLANES:SUBLANES + rows, c0:c1] = _dot(h, win_ref[:, c0:c1])

    n_pieces = len(bounds)

    def normed(x_ref, mod_ref):
        return _norm_mod(x_ref[...], gpre_ref[...], mod_ref).astype(BF16)

    @pl.when(s == 0)
    def _():
        p0_scr[...] = jnp.zeros_like(p0_scr)
        p1_scr[...] = jnp.zeros_like(p1_scr)
        scratch[0][...] = jnp.zeros_like(scratch[0])

    seq_start = (2 * s - 2) % blocks_per_seq == 0
    ha = normed(xa_ref, moda_ref)
    _gdn_block(p0_scr, p1_scr, seq_start, lambda i: in_proj_piece(ha, p1_scr, i),
               cw_ref, alog_ref, dtb_ref, onorm_ref, o_ref, 0, *scratch, n_heads=n_heads, rows=rows,
               n_pieces=n_pieces)
    hb = normed(xb_ref, modb_ref)
    _gdn_block(p1_scr, p0_scr, False, lambda i: in_proj_piece(hb, p0_scr, i),
               cw_ref, alog_ref, dtb_ref, onorm_ref, o_ref, rows, *scratch, n_heads=n_heads, rows=rows,
               n_pieces=n_pieces)


def _gdn_in(x2, mod_s, g_pre, w_g, conv_w, a_log, dt_bias, o_norm, seq):
    m, d = x2.shape
    n_heads = a_log.shape[0]
    assert n_heads % GDN_PACK == 0 and GDN_HEAD_DIM == LANES
    gw = n_heads * GDN_HEAD_DIM
    n_cols = w_g.shape[1]
    assert n_cols == 4 * gw + LANES
    rows = _pick(seq, GDN_BLOCK)
    bps = seq // rows
    nb = m // rows
    assert bps % 2 == 0 and nb % 2 == 0
    last = nb - 1
    n_chunks = rows // CHUNK
    n_prob = n_chunks * (n_heads // GDN_PACK)
    pad = LANES - 2 * n_heads
    alog_row = jnp.pad(a_log, (n_heads, pad)).reshape(1, LANES)
    dtb_row = jnp.pad(dt_bias, (n_heads, pad)).reshape(1, LANES)
    xa = lambda s: jnp.maximum(2 * s - 1, 0)
    xb = lambda s: jnp.minimum(2 * s, last)
    return pl.pallas_call(
        functools.partial(_gdn_in_kernel, n_heads=n_heads, rows=rows, blocks_per_seq=bps),
        grid=(nb // 2 + 1,),
        in_specs=[
            pl.BlockSpec((rows, d), lambda s: (xa(s), 0)),
            pl.BlockSpec((1, 3, d), lambda s: (xa(s) // bps, 0, 0)),
            pl.BlockSpec((rows, d), lambda s: (xb(s), 0)),
            pl.BlockSpec((1, 3, d), lambda s: (xb(s) // bps, 0, 0)),
            pl.BlockSpec((1, d), lambda s: (0, 0)),
            pl.BlockSpec((d, n_cols), lambda s: (0, 0)),
            pl.BlockSpec((CONV_WIDTH, 3 * gw), lambda s: (0, 0)),
            pl.BlockSpec((1, LANES), lambda s: (0, 0)),
            pl.BlockSpec((1, LANES), lambda s: (0, 0)),
            pl.BlockSpec((1, GDN_HEAD_DIM), lambda s: (0, 0)),
        ],
        out_specs=pl.BlockSpec((2 * rows, gw), lambda s: (jnp.maximum(s - 1, 0), 0)),
        out_shape=jax.ShapeDtypeStruct((m, gw), BF16),
        scratch_shapes=[
            pltpu.VMEM((rows + SUBLANES, n_cols), F32),
            pltpu.VMEM((rows + SUBLANES, n_cols), F32),
            pltpu.VMEM((n_heads, GDN_HEAD_DIM, GDN_HEAD_DIM), F32),
            pltpu.VMEM((rows, gw), BF16),
            pltpu.VMEM((rows, gw), BF16),
            pltpu.VMEM((rows, gw), BF16),
            pltpu.VMEM((rows, gw), F32),
            pltpu.VMEM((rows, gw), F32),
            pltpu.VMEM((rows, gw), BF16),
            pltpu.VMEM((rows, gw), F32),
            pltpu.VMEM((n_prob, CHUNK, GDN_PACK * CHUNK), F32),
            pltpu.VMEM((n_prob, CHUNK, GDN_PACK * CHUNK), F32),
            pltpu.VMEM((n_prob, CHUNK, GDN_PACK * CHUNK), F32),
            pltpu.VMEM((n_chunks, n_heads // 2, GDN_HEAD_DIM + CHUNK, 2 * GDN_HEAD_DIM), BF16),
            pltpu.VMEM((n_chunks, n_heads, GDN_HEAD_DIM, GDN_HEAD_DIM), F32),
            pltpu.VMEM((rows, gw), F32),
        ],
        compiler_params=_cparams("arbitrary"),
        name="gdn_in",
    )(x2, mod_s, x2, mod_s, g_pre.reshape(1, d), w_g, conv_w, alog_row, dtb_row,
      o_norm.reshape(1, GDN_HEAD_DIM))


def _rotary(x, cos, sin):
    half = x.shape[-1] // 2
    x1, x2 = x[:, :half], x[:, half:]
    return jnp.concatenate([x1 * cos - x2 * sin, x2 * cos + x1 * sin], axis=-1)


RET_PIECES = 8


def _ret_block(p_ref, seq_start, between, cos, sin, norm_ref, o_ref, row0, s_scr, *, n_heads, rows,
               n_pieces):
    dk = RET_HEAD_DIM
    rw = n_heads * dk
    ri = lax.broadcasted_iota(jnp.int32, (rows, rows), 0)
    ci = lax.broadcasted_iota(jnp.int32, (rows, rows), 1)
    delta = (ri - ci).astype(F32)
    causal = ri >= ci
    pos = lax.broadcasted_iota(jnp.int32, (rows, 1), 0).astype(F32)
    done = 0

    for h in range(n_heads):
        while done < min(n_pieces, (h + 1) * -(-n_pieces // n_heads)):
            between(done)
            done += 1
        lo, hi = h * dk, (h + 1) * dk
        log_gamma = math.log1p(-(2.0 ** (-5.0 - h)))
        q = _rotary(p_ref[:, lo:hi], cos, sin)
        k = _rotary(p_ref[:, rw + lo:rw + hi], cos, sin) * dk ** -0.5
        v_b = p_ref[:, 2 * rw + lo:2 * rw + hi].astype(BF16)
        intra = jnp.exp(jnp.where(causal, delta * log_gamma, -jnp.inf))
        scores = _dot_nt(q.astype(BF16), k.astype(BF16)) * intra
        q_in = q * jnp.exp((pos + 1.0) * log_gamma)
        k_st = k * jnp.exp((rows - 1.0 - pos) * log_gamma)
        state = jnp.where(seq_start, 0.0, s_scr[h])
        o = _dot(scores.astype(BF16), v_b) + _dot(q_in.astype(BF16), state.astype(BF16))
        s_scr[h] = state * math.exp(rows * log_gamma) + _dot_tn(k_st.astype(BF16), v_b)
        out = _rms(o, norm_ref[:, lo:hi]) * _silu(p_ref[:, 3 * rw + lo:3 * rw + hi])
        o_ref[row0:row0 + rows, lo:hi] = out.astype(o_ref.dtype)

    while done < n_pieces:
        between(done)
        done += 1


def _ret_in_kernel(xa_ref, moda_ref, xb_ref, modb_ref, gpre_ref, win_ref, cos_ref, sin_ref, norm_ref,
                   o_ref, p0_scr, p1_scr, s_scr, *, n_heads, rows, blocks_per_seq):
    s = pl.program_id(0)
    piece = win_ref.shape[1] // RET_PIECES

    def in_proj_piece(h, p_ref, i):
        p_ref[:, i * piece:(i + 1) * piece] = _dot(h, win_ref[:, i * piece:(i + 1) * piece])

    def normed(x_ref, mod_ref):
        return _norm_mod(x_ref[...], gpre_ref[...], mod_ref).astype(BF16)

    @pl.when(s == 0)
    def _():
        p0_scr[...] = jnp.zeros_like(p0_scr)
        p1_scr[...] = jnp.zeros_like(p1_scr)
        s_scr[...] = jnp.zeros_like(s_scr)

    seq_start = (2 * s - 2) % blocks_per_seq == 0
    ha = normed(xa_ref, moda_ref)
    _ret_block(p0_scr, seq_start, lambda i: in_proj_piece(ha, p1_scr, i), cos_ref[0:rows, :],
               sin_ref[0:rows, :], norm_ref, o_ref, 0, s_scr, n_heads=n_heads, rows=rows,
               n_pieces=RET_PIECES)
    hb = normed(xb_ref, modb_ref)
    _ret_block(p1_scr, False, lambda i: in_proj_piece(hb, p0_scr, i), cos_ref[rows:2 * rows, :],
               sin_ref[rows:2 * rows, :], norm_ref, o_ref, rows, s_scr, n_heads=n_heads, rows=rows,
               n_pieces=RET_PIECES)


def _ret_in(x2, mod_s, g_pre, w_r, ret_norm, seq):
    m, d = x2.shape
    rw = ret_norm.shape[0]
    n_heads = rw // RET_HEAD_DIM
    n_cols = w_r.shape[1]
    assert n_cols == 4 * rw and n_cols % (RET_PIECES * LANES) == 0
    rows = _pick(seq, 256)
    bps = seq // rows
    nb = m // rows
    assert bps % 2 == 0 and nb % 2 == 0
    last = nb - 1
    half = RET_HEAD_DIM // 2
    inv_freq = ROPE_BASE ** (-jnp.arange(half, dtype=F32) / half)
    ang = jnp.arange(seq, dtype=F32)[:, None] * inv_freq[None, :]
    cos, sin = jnp.cos(ang), jnp.sin(ang)
    xa = lambda s: jnp.maximum(2 * s - 1, 0)
    xb = lambda s: jnp.minimum(2 * s, last)
    pos_blk = lambda s: (jnp.maximum(2 * s - 2, 0) % bps) // 2
    return pl.pallas_call(
        functools.partial(_ret_in_kernel, n_heads=n_heads, rows=rows, blocks_per_seq=bps),
        grid=(nb // 2 + 1,),
        in_specs=[
            pl.BlockSpec((rows, d), lambda s: (xa(s), 0)),
            pl.BlockSpec((1, 3, d), lambda s: (xa(s) // bps, 0, 0)),
            pl.BlockSpec((rows, d), lambda s: (xb(s), 0)),
            pl.BlockSpec((1, 3, d), lambda s: (xb(s) // bps, 0, 0)),
            pl.BlockSpec((1, d), lambda s: (0, 0)),
            pl.BlockSpec((d, n_cols), lambda s: (0, 0)),
            pl.BlockSpec((2 * rows, half), lambda s: (pos_blk(s), 0)),
            pl.BlockSpec((2 * rows, half), lambda s: (pos_blk(s), 0)),
            pl.BlockSpec((1, rw), lambda s: (0, 0)),
        ],
        out_specs=pl.BlockSpec((2 * rows, rw), lambda s: (jnp.maximum(s - 1, 0), 0)),
        out_shape=jax.ShapeDtypeStruct((m, rw), BF16),
        scratch_shapes=[
            pltpu.VMEM((rows, n_cols), F32),
            pltpu.VMEM((rows, n_cols), F32),
            pltpu.VMEM((n_heads, RET_HEAD_DIM, RET_HEAD_DIM), F32),
        ],
        compiler_params=_cparams("arbitrary"),
        name="retention_in",
    )(x2, mod_s, x2, mod_s, g_pre.reshape(1, d), w_r, cos, sin, ret_norm.reshape(1, rw))


LRU_ROWS = 256


def _lru_block(p_ref, p_prev_ref, seq_start, between, perm_ref, cw_ref, cb_ref, wa_ref, ba_ref, wx_ref,
               bx_ref, lam_ref, o_ref, row0, h_scr, act_scr, *, n_blocks, rows, bw):
    width = n_blocks * bw
    nj = rows // SUBLANES
    sub = lax.broadcasted_iota(jnp.int32, (SUBLANES, bw), 0)
    tails = [jnp.where(seq_start, 0.0, p_prev_ref[rows - SUBLANES * (k - 1) - 1:rows - SUBLANES * (k - 1),
                                                  width:2 * width]) for k in range(1, CONV_WIDTH)]

    for n in range(n_blocks):
        between(2 * n)
        lo, hi = n * bw, (n + 1) * bw
        xp = p_ref[:, width + lo:width + hi]

        def edge(k):
            prev = pltpu.roll(xp[rows - k * SUBLANES:rows - (k - 1) * SUBLANES], 1, 0)
            return jnp.where(sub == 0, tails[k - 1][:, lo:hi], prev)

        edges = [edge(k) for k in range(1, CONV_WIDTH)]
        xc = xp * cw_ref[CONV_WIDTH - 1:CONV_WIDTH, lo:hi]
        for back in range(1, CONV_WIDTH):
            tap = CONV_WIDTH - 1 - back
            shifted = jnp.concatenate(edges[:back][::-1] + [xp[:rows - back * SUBLANES]], axis=0)
            xc = xc + shifted * cw_ref[tap:tap + 1, lo:hi]
        xc = xc + cb_ref[:, lo:hi]
        xc_b = xc.astype(BF16)
        r = _sigmoid(_dot(xc_b, wa_ref[n]) + ba_ref[:, lo:hi])
        i = _sigmoid(_dot(xc_b, wx_ref[n]) + bx_ref[:, lo:hi])
        log_a = (-LRU_C * r) * _softplus(-lam_ref[:, lo:hi])
        a = jnp.exp(log_a)
        mult = jnp.sqrt(-jnp.tanh(log_a) * (a * a + 1.0))
        bv = mult * (i * xc)

        h = bv[0:SUBLANES]
        ac = a[0:SUBLANES]
        hl, al = [h], [ac]
        for j in range(1, nj):
            aj = a[j * SUBLANES:(j + 1) * SUBLANES]
            h = aj * h + bv[j * SUBLANES:(j + 1) * SUBLANES]
            ac = aj * ac
            hl.append(h)
            al.append(ac)
        between(2 * n + 1)
        c = jnp.where(seq_start, 0.0, h_scr[0:1, lo:hi])
        cs = []
        for s in range(SUBLANES):
            cs.append(c)
            c = h[s:s + 1] + ac[s:s + 1] * c
        h_scr[0:1, lo:hi] = c
        c_in = jnp.concatenate(cs, axis=0)
        hs = jnp.concatenate([hl[j] + al[j] * c_in for j in range(nj)], axis=0)
        act_scr[:, lo:hi] = (hs * _gelu_tanh(p_ref[:, lo:hi])).astype(BF16)

    o_ref[row0:row0 + rows, :] = _dot_tn(perm_ref[...], act_scr[...]).astype(o_ref.dtype)


def _lru_in_kernel(x0_ref, mod0_ref, xa_ref, moda_ref, xb_ref, modb_ref, gpre_ref, win_ref, perm_ref,
                   cw_ref, cb_ref, wa_ref, ba_ref, wx_ref, bx_ref, lam_ref, o_ref,
                   p0_scr, p1_scr, h_scr, act_scr, *, n_blocks, rows, bw, blocks_per_seq):
    s = pl.program_id(0)

    def in_proj_cols(h, p_ref, n):
        cw = win_ref.shape[1] // (2 * n_blocks)
        p_ref[:, n * cw:(n + 1) * cw] = _dot(h, win_ref[:, n * cw:(n + 1) * cw])

    def normed(x_ref, mod_ref):
        h = _norm_mod(x_ref[...], gpre_ref[...], mod_ref).astype(BF16)
        return _dot(perm_ref[...], h).astype(BF16)

    @pl.when(s == 0)
    def _():
        h0 = normed(x0_ref, mod0_ref)
        for n in range(2 * n_blocks):
            in_proj_cols(h0, p0_scr, n)
        p1_scr[...] = jnp.zeros_like(p1_scr)
        h_scr[...] = jnp.zeros_like(h_scr)

    lru = functools.partial(_lru_block, perm_ref=perm_ref, cw_ref=cw_ref, cb_ref=cb_ref, wa_ref=wa_ref,
                            ba_ref=ba_ref, wx_ref=wx_ref, bx_ref=bx_ref, lam_ref=lam_ref, o_ref=o_ref,
                            h_scr=h_scr, act_scr=act_scr, n_blocks=n_blocks, rows=rows, bw=bw)
    seq_start = (2 * s) % blocks_per_seq == 0
    ha = normed(xa_ref, moda_ref)
    lru(p0_scr, p1_scr, seq_start, lambda n: in_proj_cols(ha, p1_scr, n), row0=0)
    hb = normed(xb_ref, modb_ref)
    lru(p1_scr, p0_scr, False, lambda n: in_proj_cols(hb, p0_scr, n), row0=rows)


def _lru_in(x2, mod_s, g_pre, w_in, conv_w, conv_b, gate_a_w, gate_a_b, gate_x_w, gate_x_b, lam, seq):
    m, d = x2.shape
    width = w_in.shape[1] // 2
    n_blocks, bw, _ = gate_a_w.shape
    rows = _pick(seq, LRU_ROWS)
    bps = seq // rows
    nb = m // rows
    assert bps % 2 == 0 and nb % 2 == 0
    last = nb - 1
    r_idx = jnp.arange(rows)
    t_of_r = (r_idx % SUBLANES) * (rows // SUBLANES) + r_idx // SUBLANES
    perm = (t_of_r[:, None] == r_idx[None, :]).astype(BF16)
    xa = lambda s: 2 * s + 1
    xb = lambda s: jnp.minimum(2 * s + 2, last)
    vec = lambda: pl.BlockSpec((1, width), lambda s: (0, 0))
    gate = lambda: pl.BlockSpec((n_blocks, bw, bw), lambda s: (0, 0, 0))
    return pl.pallas_call(
        functools.partial(_lru_in_kernel, n_blocks=n_blocks, rows=rows, bw=bw, blocks_per_seq=bps),
        grid=(nb // 2,),
        in_specs=[
            pl.BlockSpec((rows, d), lambda s: (0, 0)),
            pl.BlockSpec((1, 3, d), lambda s: (0, 0, 0)),
            pl.BlockSpec((rows, d), lambda s: (xa(s), 0)),
            pl.BlockSpec((1, 3, d), lambda s: (xa(s) // bps, 0, 0)),
            pl.BlockSpec((rows, d), lambda s: (xb(s), 0)),
            pl.BlockSpec((1, 3, d), lambda s: (xb(s) // bps, 0, 0)),
            pl.BlockSpec((1, d), lambda s: (0, 0)),
            pl.BlockSpec((d, 2 * width), lambda s: (0, 0)),
            pl.BlockSpec((rows, rows), lambda s: (0, 0)),
            pl.BlockSpec((CONV_WIDTH, width), lambda s: (0, 0)),
            vec(), gate(), vec(), gate(), vec(), vec(),
        ],
        out_specs=pl.BlockSpec((2 * rows, width), lambda s: (s, 0)),
        out_shape=jax.ShapeDtypeStruct((m, width), BF16),
        scratch_shapes=[
            pltpu.VMEM((rows, 2 * width), F32),
            pltpu.VMEM((rows, 2 * width), F32),
            pltpu.VMEM((SUBLANES, width), F32),
            pltpu.VMEM((rows, width), BF16),
        ],
        compiler_params=_cparams("arbitrary"),
        name="rglru_in",
    )(x2, mod_s, x2, mod_s, x2, mod_s, g_pre.reshape(1, d), w_in, perm, conv_w, conv_b.reshape(1, width),
      gate_a_w.astype(BF16), gate_a_b.reshape(1, width), gate_x_w.astype(BF16),
      gate_x_b.reshape(1, width), lam.reshape(1, width))


def kernel(x, c, ada_w, ada_b, norm_pre, norm_post, ffn_w13, ffn_w2, ev_w_in, ev_conv_w, ev_a_log,
           ev_dt_bias, ev_o_norm, ev_ret_norm, ev_w_out, od_w_in, od_conv_w, od_conv_b,
           od_gate_a_w, od_gate_a_b, od_gate_x_w, od_gate_x_b, od_lambda, od_w_out):
    b, seq, d = x.shape
    depth = ada_w.shape[0]
    m = b * seq
    mod = _ada(c, ada_w, ada_b).reshape(depth, b, N_SUB, 3, d)
    x2 = x.reshape(m, d)
    w13_b = ffn_w13.astype(BF16)
    w2_b = ffn_w2.astype(BF16)

    for layer in range(depth):
        mod_l = mod[layer]
        x2 = _ffn(x2, mod_l[:, 0], norm_pre[layer, 0], norm_post[layer, 0], w13_b, w2_b, layer, 0, seq, 0.5)
        if layer % 2 == 0:
            e = layer // 2
            n_heads = ev_a_log.shape[1]
            gw = n_heads * GDN_HEAD_DIM
            w_in = ev_w_in[e]
            w_small = jnp.pad(w_in[:, 3 * gw:3 * gw + 2 * n_heads], ((0, 0), (0, LANES - 2 * n_heads)))
            w_g = jnp.concatenate([w_in[:, :3 * gw], w_in[:, 3 * gw + 2 * n_heads:4 * gw + 2 * n_heads],
                                   w_small], axis=1).astype(BF16)
            w_r = w_in[:, 4 * gw + 2 * n_heads:].astype(BF16)
            o_a = _gdn_in(x2, mod_l[:, 1], norm_pre[layer, 1], w_g, ev_conv_w[e], ev_a_log[e],
                          ev_dt_bias[e], ev_o_norm[e], seq)
            o_b = _ret_in(x2, mod_l[:, 1], norm_pre[layer, 1], w_r, ev_ret_norm[e], seq)
            acts = [o_a, o_b]
            w_out = ev_w_out[e]
        else:
            o = layer // 2
            hs = _lru_in(x2, mod_l[:, 1], norm_pre[layer, 1], od_w_in[o].astype(BF16), od_conv_w[o],
                         od_conv_b[o], od_gate_a_w[o], od_gate_a_b[o], od_gate_x_w[o], od_gate_x_b[o],
                         od_lambda[o], seq)
            acts = [hs.reshape(m, -1)]
            w_out = od_w_out[o]
        x2 = _outproj(acts, w_out.astype(BF16), x2, mod_l[:, 1], norm_post[layer, 1], seq, 1.0)
        x2 = _ffn(x2, mod_l[:, 2], norm_pre[layer, 2], norm_post[layer, 2], w13_b, w2_b, layer, 1, seq, 0.5)
    return x2.reshape(b, seq, d)
```

```python
import functools
import math

import jax
import jax.numpy as jnp
from jax import lax
from jax.experimental import pallas as pl
from jax.experimental.pallas import tpu as pltpu

F32 = jnp.float32
BF16 = jnp.bfloat16

EPS = 1e-6
GDN_HEAD_DIM = 128
RET_HEAD_DIM = 256
CHUNK = 64
CONV_WIDTH = 4
ROPE_BASE = 10000.0
LRU_C = 8.0
N_SUB = 3

SUBLANES = 8
LANES = 128
V7X_VMEM_LIMIT_BYTES = 56 * 1024 * 1024


def _cparams(*semantics):
    return pltpu.CompilerParams(dimension_semantics=semantics,
                                vmem_limit_bytes=V7X_VMEM_LIMIT_BYTES)


def _dot(a, b):
    return jnp.dot(a, b, preferred_element_type=F32)


def _dot_nt(a, b):
    return lax.dot_general(a, b, (((1,), (1,)), ((), ())), preferred_element_type=F32)


def _dot_tn(a, b):
    return lax.dot_general(a, b, (((0,), (0,)), ((), ())), preferred_element_type=F32)


def _sigmoid(x):
    return jax.nn.sigmoid(x)


def _silu(x):
    return x * jax.nn.sigmoid(x)


def _softplus(x):
    return jnp.maximum(x, 0.0) + jnp.log1p(jnp.exp(-jnp.abs(x)))


def _gelu_tanh(x):
    c = math.sqrt(2.0 / math.pi)
    return 0.5 * x * (1.0 + jnp.tanh(c * (x + 0.044715 * (x * x * x))))


def _rms(x, gain):
    return x * lax.rsqrt(jnp.mean(x * x, axis=-1, keepdims=True) + EPS) * gain


def _inv_rms(x):
    return lax.rsqrt(jnp.mean(x * x, axis=-1, keepdims=True) + EPS)


def _norm_mod(x, g_pre, mod_ref):
    return (x * _inv_rms(x)) * (g_pre * (1.0 + mod_ref[0, 1:2, :])) + mod_ref[0, 0:1, :]


def _post_residual(x, f, g_post, mod_ref, res_w):
    return x + (f * _inv_rms(f)) * (g_post * (res_w * (1.0 + mod_ref[0, 2:3, :])))


def _pick(n, pref):
    if n <= pref:
        return n
    for t in range(pref - pref % LANES, 0, -LANES):
        if n % t == 0:
            return t
    raise ValueError(f"no lane-aligned tile of {n} at or below {pref}")


def _ada_kernel(c_ref, w_ref, b_ref, o_ref):
    a = _silu(c_ref[...]).astype(BF16)
    o_ref[0] = _dot(a, w_ref[0].astype(BF16)) + b_ref[0]


def _ada(c, ada_w, ada_b):
    n_layers, d, n = ada_w.shape
    b = c.shape[0]
    tn = _pick(n, 1024)
    return pl.pallas_call(
        _ada_kernel,
        grid=(n_layers, n // tn),
        in_specs=[
            pl.BlockSpec((b, d), lambda l, j: (0, 0)),
            pl.BlockSpec((1, d, tn), lambda l, j: (l, 0, j)),
            pl.BlockSpec((1, 1, tn), lambda l, j: (l, 0, j)),
        ],
        out_specs=pl.BlockSpec((1, b, tn), lambda l, j: (l, 0, j)),
        out_shape=jax.ShapeDtypeStruct((n_layers, b, n), F32),
        compiler_params=_cparams("arbitrary", "arbitrary"),
        name="ada_mod",
    )(c, ada_w, ada_b.reshape(n_layers, 1, n))


FFN_HALVES = 2


def _ffn_kernel(x_ref, mod_ref, gpre_ref, gpost_ref, w1_ref, w3_ref, w2_ref, o_ref, h_scr,
                *, res_w, n_f):
    j = pl.program_id(1)
    tm = x_ref.shape[0]
    part = tm // FFN_HALVES

    def down(rs):
        h = h_scr[rs, :]
        a = (_silu(_dot(h, w1_ref[...])) * _dot(h, w3_ref[...])).astype(BF16)
        return _dot(a, w2_ref[...])

    @pl.when(j == 0)
    def _():
        for p in range(FFN_HALVES):
            rs = slice(p * part, (p + 1) * part)
            h_scr[rs, :] = _norm_mod(x_ref[rs, :], gpre_ref[...], mod_ref).astype(BF16)
            o_ref[rs, :] = down(rs)

    @pl.when((j > 0) & (j < n_f - 1))
    def _():
        for p in range(FFN_HALVES):
            rs = slice(p * part, (p + 1) * part)
            o_ref[rs, :] += down(rs)

    @pl.when(j == n_f - 1)
    def _():
        for p in range(FFN_HALVES):
            rs = slice(p * part, (p + 1) * part)
            f = o_ref[rs, :] + down(rs)
            o_ref[rs, :] = _post_residual(x_ref[rs, :], f, gpost_ref[...], mod_ref, res_w)


def _ffn(x2, mod_s, g_pre, g_post, w13_all, w2_all, layer, which, seq, res_w):
    m, d = x2.shape
    f = w2_all.shape[2]
    tm = _pick(seq, 1024)
    tf = _pick(f, 512)
    n_f = f // tf
    per_b = seq // tm
    assert n_f >= 2 and tm % (FFN_HALVES * 2 * SUBLANES) == 0
    return pl.pallas_call(
        functools.partial(_ffn_kernel, res_w=res_w, n_f=n_f),
        grid=(m // tm, n_f),
        in_specs=[
            pl.BlockSpec((tm, d), lambda i, j: (i, 0)),
            pl.BlockSpec((1, 3, d), lambda i, j: (i // per_b, 0, 0)),
            pl.BlockSpec((1, d), lambda i, j: (0, 0)),
            pl.BlockSpec((1, d), lambda i, j: (0, 0)),
            pl.BlockSpec((None, None, d, tf), lambda i, j: (layer, which, 0, j)),
            pl.BlockSpec((None, None, d, tf), lambda i, j: (layer, which, 0, j + n_f)),
            pl.BlockSpec((None, None, tf, d), lambda i, j: (layer, which, j, 0)),
        ],
        out_specs=pl.BlockSpec((tm, d), lambda i, j: (i, 0)),
        out_shape=jax.ShapeDtypeStruct((m, d), F32),
        scratch_shapes=[pltpu.VMEM((tm, d), BF16)],
        compiler_params=_cparams("arbitrary", "arbitrary"),
        name="ffn",
    )(x2, mod_s, g_pre.reshape(1, d), g_post.reshape(1, d), w13_all, w13_all, w2_all)


def _outproj_kernel(*refs, n_in, res_w):
    a_refs = refs[:n_in]
    w_refs = refs[n_in:2 * n_in]
    x_ref, mod_ref, gpost_ref, o_ref = refs[2 * n_in:]
    f = _dot(a_refs[0][...], w_refs[0][...])
    for a_ref, w_ref in zip(a_refs[1:], w_refs[1:]):
        f = f + _dot(a_ref[...], w_ref[...])
    o_ref[...] = _post_residual(x_ref[...], f, gpost_ref[...], mod_ref, res_w)


def _outproj(acts, w_out, x2, mod_s, g_post, seq, res_w):
    m, d = x2.shape
    tm = _pick(seq, 512)
    per_b = seq // tm
    n_in = len(acts)
    in_specs, args = [], []
    for a in acts:
        in_specs.append(pl.BlockSpec((tm, a.shape[1]), lambda i: (i, 0)))
        args.append(a)
    row = 0
    for a in acts:
        wi = a.shape[1]
        assert row % wi == 0
        in_specs.append(pl.BlockSpec((wi, d), lambda i, r=row // wi: (r, 0)))
        args.append(w_out)
        row += wi
    in_specs += [
        pl.BlockSpec((tm, d), lambda i: (i, 0)),
        pl.BlockSpec((1, 3, d), lambda i: (i // per_b, 0, 0)),
        pl.BlockSpec((1, d), lambda i: (0, 0)),
    ]
    args += [x2, mod_s, g_post.reshape(1, d)]
    return pl.pallas_call(
        functools.partial(_outproj_kernel, n_in=n_in, res_w=res_w),
        grid=(m // tm,),
        in_specs=in_specs,
        out_specs=pl.BlockSpec((tm, d), lambda i: (i, 0)),
        out_shape=jax.ShapeDtypeStruct((m, d), F32),
        compiler_params=_cparams("arbitrary"),
        name="mixer_outproj",
    )(*args)


def _conv_from_scratch(cs_ref, w_ref, rows, lo, hi):
    acc = cs_ref[SUBLANES:SUBLANES + rows, lo:hi] * w_ref[CONV_WIDTH - 1:CONV_WIDTH, lo:hi]
    for back in range(1, CONV_WIDTH):
        tap = CONV_WIDTH - 1 - back
        acc = acc + cs_ref[SUBLANES - back:SUBLANES - back + rows, lo:hi] * w_ref[tap:tap + 1, lo:hi]
    return acc


GDN_BLOCK = 256
GDN_PACK = 4


def _tile_rows(x, n):
    return jnp.concatenate([x] * n, axis=0)


def _gdn_block(p_ref, p_prev_ref, seq_start, between, cw_ref, alog_ref, dtb_ref, onorm_ref,
               o_ref, row0, s_scr, qn_scr, kn_scr, kb_scr, vb_scr, kbe_scr, kst_scr, egq_scr,
               a_scr, qk_scr, xo_scr, l_scr, ku_scr, oin_scr, *, n_heads, rows, n_pieces):
    dk = GDN_HEAD_DIM
    gw = n_heads * dk
    n_chunks = rows // CHUNK
    n_groups = n_heads // GDN_PACK
    pw = GDN_PACK * CHUNK
    gk = GDN_PACK * dk

    tail = p_prev_ref[rows:rows + SUBLANES, 0:3 * gw]
    p_ref[0:SUBLANES, 0:3 * gw] = jnp.where(seq_start, 0.0, tail)
    cs_scr = p_ref
    n_between = [0]

    def next_piece():
        between(n_between[0])
        n_between[0] += 1

    sm = p_ref[SUBLANES:SUBLANES + rows, 4 * gw:4 * gw + LANES]
    beta_all = _sigmoid(sm)
    g_all = -jnp.exp(alog_ref[...]) * _softplus(sm + dtb_ref[...])
    in_chunk = lax.broadcasted_iota(jnp.int32, (rows, LANES), 0) & (CHUNK - 1)
    d = 1
    while d < CHUNK:
        g_all = g_all + jnp.where(in_chunk >= d, pltpu.roll(g_all, d, 0), 0.0)
        d *= 2
    eg_all = jnp.exp(g_all)
    glast_all = jnp.concatenate(
        [jnp.broadcast_to(g_all[(c + 1) * CHUNK - 1:(c + 1) * CHUNK, :], (CHUNK, LANES))
         for c in range(n_chunks)], axis=0)
    ekl_all = jnp.exp(glast_all - g_all)

    for h in range(n_heads):
        next_piece()
        lo, hi = h * dk, (h + 1) * dk
        bcast = lambda arr, idx: jnp.broadcast_to(arr[:, idx:idx + 1], (rows, dk))
        beta_b = bcast(beta_all, h)
        eg_b = bcast(eg_all, n_heads + h)
        q = _silu(_conv_from_scratch(cs_scr, cw_ref, rows, lo, hi))
        qn = (q * lax.rsqrt(jnp.sum(q * q, axis=-1, keepdims=True) + EPS)) * dk ** -0.5
        qn_scr[:, lo:hi] = qn.astype(BF16)
        egq_scr[:, lo:hi] = qn * eg_b
        k = _silu(_conv_from_scratch(cs_scr, cw_ref, rows, gw + lo, gw + hi))
        kn = k * lax.rsqrt(jnp.sum(k * k, axis=-1, keepdims=True) + EPS)
        kb = kn * beta_b
        kn_scr[:, lo:hi] = kn.astype(BF16)
        kb_scr[:, lo:hi] = kb.astype(BF16)
        kbe_scr[:, lo:hi] = kb * eg_b
        kst_scr[:, lo:hi] = (kn * bcast(ekl_all, n_heads + h)).astype(BF16)
        v = _silu(_conv_from_scratch(cs_scr, cw_ref, rows, 2 * gw + lo, 2 * gw + hi))
        vb_scr[:, lo:hi] = v * beta_b

    ri = lax.broadcasted_iota(jnp.int32, (CHUNK, pw), 0)
    li = lax.broadcasted_iota(jnp.int32, (CHUNK, pw), 1)
    cj = li & (CHUNK - 1)
    lh = li // CHUNK
    causal = ri >= cj
    strict = ri > cj
    eye = ri == cj
    bd_r = lax.broadcasted_iota(jnp.int32, (pw, pw), 0) // CHUNK
    bd_c = lax.broadcasted_iota(jnp.int32, (pw, pw), 1) // CHUNK
    bd_sq = bd_r == bd_c
    bk_r = lax.broadcasted_iota(jnp.int32, (pw, gk), 0) // CHUNK
    bk_c = lax.broadcasted_iota(jnp.int32, (pw, gk), 1) // dk
    bd_k = bk_r == bk_c
    zero_sq = jnp.zeros((pw, pw), BF16)
    zero_k = jnp.zeros((pw, gk), BF16)

    def block_diag(x_rp):
        return jnp.where(bd_sq, _tile_rows(x_rp.astype(BF16), GDN_PACK), zero_sq)

    def block_diag_k(x_cat):
        return jnp.where(bd_k, _tile_rows(x_cat, GDN_PACK), zero_k)

    probs = [(c, gi) for c in range(n_chunks) for gi in range(n_groups)]
    for p, (c, gi) in enumerate(probs):
        next_piece()
        r0, r1 = c * CHUNK, (c + 1) * CHUNK
        c0 = gi * gk
        gcs = [jnp.broadcast_to(g_all[r0:r1, n_heads + gi * GDN_PACK + s:n_heads + gi * GDN_PACK + s + 1],
                                (CHUNK, pw)) for s in range(GDN_PACK)]
        gc = gcs[GDN_PACK - 1]
        for s in range(GDN_PACK - 2, -1, -1):
            gc = jnp.where(lh == s, gcs[s], gc)
        gr = jnp.sum(jnp.where(eye, gc, 0.0), axis=0, keepdims=True)
        decay = jnp.exp(jnp.where(causal, gc - gr, -jnp.inf))
        lhs = jnp.concatenate([kb_scr[r0:r1, c0:c0 + gk], qn_scr[r0:r1, c0:c0 + gk]], axis=0)
        kq = _dot_nt(lhs, block_diag_k(kn_scr[r0:r1, c0:c0 + gk]))
        a = jnp.where(strict, kq[0:CHUNK] * decay, 0.0)
        a_scr[p] = a
        qk_scr[p] = jnp.where(causal, kq[CHUNK:2 * CHUNK] * decay, 0.0)
        xo_scr[p] = -jnp.where((ri - cj == 1) & ((ri & 1) == 1), a, 0.0)

    next_piece()
    s = 2
    while s < CHUNK:
        rb = ri // s
        off = ((rb & 1) == 1) & ((cj // s) == rb - 1)
        for p in range(len(probs)):
            a_off = jnp.where(off, a_scr[p], 0.0)
            xo = xo_scr[p]
            y = a_off + _dot(xo.astype(BF16), block_diag(a_off))
            xo_scr[p] = xo - (y + _dot(y.astype(BF16), block_diag(xo)))
        s *= 2

    for p, (c, gi) in enumerate(probs):
        r0, r1 = c * CHUNK, (c + 1) * CHUNK
        rhs = jnp.concatenate(
            [jnp.concatenate([vb_scr[r0:r1, h * dk:(h + 1) * dk], kbe_scr[r0:r1, h * dk:(h + 1) * dk]], axis=1)
             for h in range(gi * GDN_PACK, (gi + 1) * GDN_PACK)], axis=0)
        sol = rhs + _dot(block_diag(xo_scr[p]), rhs.astype(BF16))
        sol_b = sol.astype(BF16)
        qks = _dot(block_diag(qk_scr[p]), sol_b)
        ks = _dot_tn(block_diag_k(kst_scr[r0:r1, gi * gk:(gi + 1) * gk]), sol_b)
        for s_ in range(GDN_PACK):
            h = gi * GDN_PACK + s_
            half = (h % 2) * dk
            rr = slice(s_ * CHUNK, (s_ + 1) * CHUNK)
            kr = slice(s_ * dk, (s_ + 1) * dk)
            q_eff = egq_scr[r0:r1, h * dk:(h + 1) * dk] - qks[rr, dk:2 * dk]
            l_scr[c, h // 2, 0:dk, half:half + dk] = ks[kr, dk:2 * dk].astype(BF16)
            l_scr[c, h // 2, dk:dk + CHUNK, half:half + dk] = q_eff.astype(BF16)
            ku_scr[c, h] = ks[kr, 0:dk]
            oin_scr[r0:r1, h * dk:(h + 1) * dk] = qks[rr, 0:dk]

    zero = jnp.zeros((dk, dk), BF16)
    for c in range(n_chunks):
        r0, r1 = c * CHUNK, (c + 1) * CHUNK
        for pr in range(n_heads // 2):
            h0, h1 = 2 * pr, 2 * pr + 1
            s0 = s_scr[h0]
            s1 = s_scr[h1]
            if c == 0:
                s0 = jnp.where(seq_start, 0.0, s0)
                s1 = jnp.where(seq_start, 0.0, s1)
            s_bd = jnp.concatenate(
                [jnp.concatenate([s0.astype(BF16), zero], axis=1),
                 jnp.concatenate([zero, s1.astype(BF16)], axis=1)], axis=0)
            r = _dot(l_scr[c, pr], s_bd)
            for h, st, off_ in ((h0, s0, 0), (h1, s1, dk)):
                cd = eg_all[r1 - 1:r1, n_heads + h:n_heads + h + 1]
                s_scr[h] = st * cd - r[0:dk, off_:off_ + dk] + ku_scr[c, h]
                o = r[dk:dk + CHUNK, off_:off_ + dk] + oin_scr[r0:r1, h * dk:(h + 1) * dk]
                z = p_ref[SUBLANES + r0:SUBLANES + r1, 3 * gw + h * dk:3 * gw + (h + 1) * dk]
                out = _rms(o, onorm_ref[...]) * _silu(z)
                o_ref[row0 + r0:row0 + r1, h * dk:(h + 1) * dk] = out.astype(o_ref.dtype)

    while n_between[0] < n_pieces:
        next_piece()


GDN_PIECES = 17


def _gdn_in_kernel(xa_ref, moda_ref, xb_ref, modb_ref, gpre_ref, win_ref,
                   cw_ref, alog_ref, dtb_ref, onorm_ref, o_ref, p0_scr, p1_scr, *scratch,
                   n_heads, rows, blocks_per_seq):
    s = pl.program_id(0)
    n_cols = win_ref.shape[1]
    piece = -(-n_cols // (GDN_PIECES * LANES)) * LANES
    bounds = [(c, min(c + piece, n_cols)) for c in range(0, n_cols, piece)]
    assert len(bounds) <= GDN_PIECES

    def in_proj_piece(h, p_ref, i):
        if i < len(bounds):
            c0, c1 = bounds[i]
            p_ref[SUBLANES:SUBLANES + rows, c0:c1] = _dot(h, win_ref[:, c0:c1])

    n_pieces = len(bounds)

    def normed(x_ref, mod_ref):
        return _norm_mod(x_ref[...], gpre_ref[...], mod_ref).astype(BF16)

    @pl.when(s == 0)
    def _():
        p0_scr[...] = jnp.zeros_like(p0_scr)
        p1_scr[...] = jnp.zeros_like(p1_scr)
        scratch[0][...] = jnp.zeros_like(scratch[0])

    seq_start = (2 * s - 2) % blocks_per_seq == 0
    ha = normed(xa_ref, moda_ref)
    _gdn_block(p0_scr, p1_scr, seq_start, lambda i: in_proj_piece(ha, p1_scr, i),
               cw_ref, alog_ref, dtb_ref, onorm_ref, o_ref, 0, *scratch, n_heads=n_heads, rows=rows,
               n_pieces=n_pieces)
    hb = normed(xb_ref, modb_ref)
    _gdn_block(p1_scr, p0_scr, False, lambda i: in_proj_piece(hb, p0_scr, i),
               cw_ref, alog_ref, dtb_ref, onorm_ref, o_ref, rows, *scratch, n_heads=n_heads, rows=rows,
               n_pieces=n_pieces)


def _gdn_in(x2, mod_s, g_pre, w_g, conv_w, a_log, dt_bias, o_norm, seq):
    m, d = x2.shape
    n_heads = a_log.shape[0]
    assert n_heads % GDN_PACK == 0 and GDN_HEAD_DIM == LANES
    gw = n_heads * GDN_HEAD_DIM
    n_cols = w_g.shape[1]
    assert n_cols == 4 * gw + LANES
    rows = _pick(seq, GDN_BLOCK)
    bps = seq // rows
    nb = m // rows
    assert bps % 2 == 0 and nb % 2 == 0
    last = nb - 1
    n_chunks = rows // CHUNK
    n_prob = n_chunks * (n_heads // GDN_PACK)
    pad = LANES - 2 * n_heads
    alog_row = jnp.pad(a_log, (n_heads, pad)).reshape(1, LANES)
    dtb_row = jnp.pad(dt_bias, (n_heads, pad)).reshape(1, LANES)
    xa = lambda s: jnp.maximum(2 * s - 1, 0)
    xb = lambda s: jnp.minimum(2 * s, last)
    return pl.pallas_call(
        functools.partial(_gdn_in_kernel, n_heads=n_heads, rows=rows, blocks_per_seq=bps),
        grid=(nb // 2 + 1,),
        in_specs=[
            pl.BlockSpec((rows, d), lambda s: (xa(s), 0)),
            pl.BlockSpec((1, 3, d), lambda s: (xa(s) // bps, 0, 0)),
            pl.BlockSpec((rows, d), lambda s: (xb(s), 0)),
            pl.BlockSpec((1, 3, d), lambda s: (xb(s) // bps, 0, 0)),
            pl.BlockSpec((1, d), lambda s: (0, 0)),
            pl.BlockSpec((d, n_cols), lambda s: (0, 0)),
            pl.BlockSpec((CONV_WIDTH, 3 * gw), lambda s: (0, 0)),
            pl.BlockSpec((1, LANES), lambda s: (0, 0)),
            pl.BlockSpec((1, LANES), lambda s: (0, 0)),
            pl.BlockSpec((1, GDN_HEAD_DIM), lambda s: (0, 0)),
        ],
        out_specs=pl.BlockSpec((2 * rows, gw), lambda s: (jnp.maximum(s - 1, 0), 0)),
        out_shape=jax.ShapeDtypeStruct((m, gw), BF16),
        scratch_shapes=[
            pltpu.VMEM((rows + SUBLANES, n_cols), F32),
            pltpu.VMEM((rows + SUBLANES, n_cols), F32),
            pltpu.VMEM((n_heads, GDN_HEAD_DIM, GDN_HEAD_DIM), F32),
            pltpu.VMEM((rows, gw), BF16),
            pltpu.VMEM((rows, gw), BF16),
            pltpu.VMEM((rows, gw), BF16),
            pltpu.VMEM((rows, gw), F32),
            pltpu.VMEM((rows, gw), F32),
            pltpu.VMEM((rows, gw), BF16),
            pltpu.VMEM((rows, gw), F32),
            pltpu.VMEM((n_prob, CHUNK, GDN_PACK * CHUNK), F32),
            pltpu.VMEM((n_prob, CHUNK, GDN_PACK * CHUNK), F32),
            pltpu.VMEM((n_prob, CHUNK, GDN_PACK * CHUNK), F32),
            pltpu.VMEM((n_chunks, n_heads // 2, GDN_HEAD_DIM + CHUNK, 2 * GDN_HEAD_DIM), BF16),
            pltpu.VMEM((n_chunks, n_heads, GDN_HEAD_DIM, GDN_HEAD_DIM), F32),
            pltpu.VMEM((rows, gw), F32),
        ],
        compiler_params=_cparams("arbitrary"),
        name="gdn_in",
    )(x2, mod_s, x2, mod_s, g_pre.reshape(1, d), w_g, conv_w, alog_row, dtb_row,
      o_norm.reshape(1, GDN_HEAD_DIM))


def _rotary(x, cos, sin):
    half = x.shape[-1] // 2
    x1, x2 = x[:, :half], x[:, half:]
    return jnp.concatenate([x1 * cos - x2 * sin, x2 * cos + x1 * sin], axis=-1)


RET_PIECES = 8


def _ret_block(p_ref, seq_start, between, cos, sin, norm_ref, o_ref, row0, s_scr, *, n_heads, rows,
               n_pieces):
    dk = RET_HEAD_DIM
    rw = n_heads * dk
    ri = lax.broadcasted_iota(jnp.int32, (rows, rows), 0)
    ci = lax.broadcasted_iota(jnp.int32, (rows, rows), 1)
    delta = (ri - ci).astype(F32)
    causal = ri >= ci
    pos = lax.broadcasted_iota(jnp.int32, (rows, 1), 0).astype(F32)
    done = 0

    for h in range(n_heads):
        while done < min(n_pieces, (h + 1) * -(-n_pieces // n_heads)):
            between(done)
            done += 1
        lo, hi = h * dk, (h + 1) * dk
        log_gamma = math.log1p(-(2.0 ** (-5.0 - h)))
        q = _rotary(p_ref[:, lo:hi], cos, sin)
        k = _rotary(p_ref[:, rw + lo:rw + hi], cos, sin) * dk ** -0.5
        v_b = p_ref[:, 2 * rw + lo:2 * rw + hi].astype(BF16)
        intra = jnp.exp(jnp.where(causal, delta * log_gamma, -jnp.inf))
        scores = _dot_nt(q.astype(BF16), k.astype(BF16)) * intra
        q_in = q * jnp.exp((pos + 1.0) * log_gamma)
        k_st = k * jnp.exp((rows - 1.0 - pos) * log_gamma)
        state = jnp.where(seq_start, 0.0, s_scr[h])
        o = _dot(scores.astype(BF16), v_b) + _dot(q_in.astype(BF16), state.astype(BF16))
        s_scr[h] = state * math.exp(rows * log_gamma) + _dot_tn(k_st.astype(BF16), v_b)
        out = _rms(o, norm_ref[:, lo:hi]) * _silu(p_ref[:, 3 * rw + lo:3 * rw + hi])
        o_ref[row0:row0 + rows, lo:hi] = out.astype(o_ref.dtype)

    while done < n_pieces:
        between(done)
        done += 1


def _ret_in_kernel(xa_ref, moda_ref, xb_ref, modb_ref, gpre_ref, win_ref, cos_ref, sin_ref, norm_ref,
                   o_ref, p0_scr, p1_scr, s_scr, *, n_heads, rows, blocks_per_seq):
    s = pl.program_id(0)
    piece = win_ref.shape[1] // RET_PIECES

    def in_proj_piece(h, p_ref, i):
        p_ref[:, i * piece:(i + 1) * piece] = _dot(h, win_ref[:, i * piece:(i + 1) * piece])

    def normed(x_ref, mod_ref):
        return _norm_mod(x_ref[...], gpre_ref[...], mod_ref).astype(BF16)

    @pl.when(s == 0)
    def _():
        p0_scr[...] = jnp.zeros_like(p0_scr)
        p1_scr[...] = jnp.zeros_like(p1_scr)
        s_scr[...] = jnp.zeros_like(s_scr)

    seq_start = (2 * s - 2) % blocks_per_seq == 0
    ha = normed(xa_ref, moda_ref)
    _ret_block(p0_scr, seq_start, lambda i: in_proj_piece(ha, p1_scr, i), cos_ref[0:rows, :],
               sin_ref[0:rows, :], norm_ref, o_ref, 0, s_scr, n_heads=n_heads, rows=rows,
               n_pieces=RET_PIECES)
    hb = normed(xb_ref, modb_ref)
    _ret_block(p1_scr, False, lambda i: in_proj_piece(hb, p0_scr, i), cos_ref[rows:2 * rows, :],
               sin_ref[rows:2 * rows, :], norm_ref, o_ref, rows, s_scr, n_heads=n_heads, rows=rows,
               n_pieces=RET_PIECES)


def _ret_in(x2, mod_s, g_pre, w_r, ret_norm, seq):
    m, d = x2.shape
    rw = ret_norm.shape[0]
    n_heads = rw // RET_HEAD_DIM
    n_cols = w_r.shape[1]
    assert n_cols == 4 * rw and n_cols % (RET_PIECES * LANES) == 0
    rows = _pick(seq, 256)
    bps = seq // rows
    nb = m // rows
    assert bps % 2 == 0 and nb % 2 == 0
    last = nb - 1
    half = RET_HEAD_DIM // 2
    inv_freq = ROPE_BASE ** (-jnp.arange(half, dtype=F32) / half)
    ang = jnp.arange(seq, dtype=F32)[:, None] * inv_freq[None, :]
    cos, sin = jnp.cos(ang), jnp.sin(ang)
    xa = lambda s: jnp.maximum(2 * s - 1, 0)
    xb = lambda s: jnp.minimum(2 * s, last)
    pos_blk = lambda s: (jnp.maximum(2 * s - 2, 0) % bps) // 2
    return pl.pallas_call(
        functools.partial(_ret_in_kernel, n_heads=n_heads, rows=rows, blocks_per_seq=bps),
        grid=(nb // 2 + 1,),
        in_specs=[
            pl.BlockSpec((rows, d), lambda s: (xa(s), 0)),
            pl.BlockSpec((1, 3, d), lambda s: (xa(s) // bps, 0, 0)),
            pl.BlockSpec((rows, d), lambda s: (xb(s), 0)),
            pl.BlockSpec((1, 3, d), lambda s: (xb(s) // bps, 0, 0)),
            pl.BlockSpec((1, d), lambda s: (0, 0)),
            pl.BlockSpec((d, n_cols), lambda s: (0, 0)),
            pl.BlockSpec((2 * rows, half), lambda s: (pos_blk(s), 0)),
            pl.BlockSpec((2 * rows, half), lambda s: (pos_blk(s), 0)),
            pl.BlockSpec((1, rw), lambda s: (0, 0)),
        ],
        out_specs=pl.BlockSpec((2 * rows, rw), lambda s: (jnp.maximum(s - 1, 0), 0)),
        out_shape=jax.ShapeDtypeStruct((m, rw), BF16),
        scratch_shapes=[
            pltpu.VMEM((rows, n_cols), F32),
            pltpu.VMEM((rows, n_cols), F32),
            pltpu.VMEM((n_heads, RET_HEAD_DIM, RET_HEAD_DIM), F32),
        ],
        compiler_params=_cparams("arbitrary"),
        name="retention_in",
    )(x2, mod_s, x2, mod_s, g_pre.reshape(1, d), w_r, cos, sin, ret_norm.reshape(1, rw))


LRU_ROWS = 256


def _lru_block(p_ref, p_prev_ref, seq_start, between, perm_ref, cw_ref, cb_ref, wa_ref, ba_ref, wx_ref,
               bx_ref, lam_ref, o_ref, row0, h_scr, act_scr, *, n_blocks, rows, bw):
    width = n_blocks * bw
    nj = rows // SUBLANES
    sub = lax.broadcasted_iota(jnp.int32, (SUBLANES, bw), 0)
    tails = [jnp.where(seq_start, 0.0, p_prev_ref[rows - SUBLANES * (k - 1) - 1:rows - SUBLANES * (k - 1),
                                                  width:2 * width]) for k in range(1, CONV_WIDTH)]

    for n in range(n_blocks):
        between(2 * n)
        lo, hi = n * bw, (n + 1) * bw
        xp = p_ref[:, width + lo:width + hi]

        def edge(k):
            prev = pltpu.roll(xp[rows - k * SUBLANES:rows - (k - 1) * SUBLANES], 1, 0)
            return jnp.where(sub == 0, tails[k - 1][:, lo:hi], prev)

        edges = [edge(k) for k in range(1, CONV_WIDTH)]
        xc = xp * cw_ref[CONV_WIDTH - 1:CONV_WIDTH, lo:hi]
        for back in range(1, CONV_WIDTH):
            tap = CONV_WIDTH - 1 - back
            shifted = jnp.concatenate(edges[:back][::-1] + [xp[:rows - back * SUBLANES]], axis=0)
            xc = xc + shifted * cw_ref[tap:tap + 1, lo:hi]
        xc = xc + cb_ref[:, lo:hi]
        xc_b = xc.astype(BF16)
        r = _sigmoid(_dot(xc_b, wa_ref[n]) + ba_ref[:, lo:hi])
        i = _sigmoid(_dot(xc_b, wx_ref[n]) + bx_ref[:, lo:hi])
        log_a = (-LRU_C * r) * _softplus(-lam_ref[:, lo:hi])
        a = jnp.exp(log_a)
        mult = jnp.sqrt(-jnp.tanh(log_a) * (a * a + 1.0))
        bv = mult * (i * xc)

        h = bv[0:SUBLANES]
        ac = a[0:SUBLANES]
        hl, al = [h], [ac]
        for j in range(1, nj):
            aj = a[j * SUBLANES:(j + 1) * SUBLANES]
            h = aj * h + bv[j * SUBLANES:(j + 1) * SUBLANES]
            ac = aj * ac
            hl.append(h)
            al.append(ac)
        between(2 * n + 1)
        c = jnp.where(seq_start, 0.0, h_scr[0:1, lo:hi])
        cs = []
        for s in range(SUBLANES):
            cs.append(c)
            c = h[s:s + 1] + ac[s:s + 1] * c
        h_scr[0:1, lo:hi] = c
        c_in = jnp.concatenate(cs, axis=0)
        hs = jnp.concatenate([hl[j] + al[j] * c_in for j in range(nj)], axis=0)
        act_scr[:, lo:hi] = (hs * _gelu_tanh(p_ref[:, lo:hi])).astype(BF16)

    o_ref[row0:row0 + rows, :] = _dot_tn(perm_ref[...], act_scr[...]).astype(o_ref.dtype)


def _lru_in_kernel(x0_ref, mod0_ref, xa_ref, moda_ref, xb_ref, modb_ref, gpre_ref, win_ref, perm_ref,
                   cw_ref, cb_ref, wa_ref, ba_ref, wx_ref, bx_ref, lam_ref, o_ref,
                   p0_scr, p1_scr, h_scr, act_scr, *, n_blocks, rows, bw, blocks_per_seq):
    s = pl.program_id(0)

    def in_proj_cols(h, p_ref, n):
        cw = win_ref.shape[1] // (2 * n_blocks)
        p_ref[:, n * cw:(n + 1) * cw] = _dot(h, win_ref[:, n * cw:(n + 1) * cw])

    def normed(x_ref, mod_ref):
        h = _norm_mod(x_ref[...], gpre_ref[...], mod_ref).astype(BF16)
        return _dot(perm_ref[...], h).astype(BF16)

    @pl.when(s == 0)
    def _():
        h0 = normed(x0_ref, mod0_ref)
        for n in range(2 * n_blocks):
            in_proj_cols(h0, p0_scr, n)
        p1_scr[...] = jnp.zeros_like(p1_scr)
        h_scr[...] = jnp.zeros_like(h_scr)

    lru = functools.partial(_lru_block, perm_ref=perm_ref, cw_ref=cw_ref, cb_ref=cb_ref, wa_ref=wa_ref,
                            ba_ref=ba_ref, wx_ref=wx_ref, bx_ref=bx_ref, lam_ref=lam_ref, o_ref=o_ref,
                            h_scr=h_scr, act_scr=act_scr, n_blocks=n_blocks, rows=rows, bw=bw)
    seq_start = (2 * s) % blocks_per_seq == 0
    ha = normed(xa_ref, moda_ref)
    lru(p0_scr, p1_scr, seq_start, lambda n: in_proj_cols(ha, p1_scr, n), row0=0)
    hb = normed(xb_ref, modb_ref)
    lru(p1_scr, p0_scr, False, lambda n: in_proj_cols(hb, p0_scr, n), row0=rows)


def _lru_in(x2, mod_s, g_pre, w_in, conv_w, conv_b, gate_a_w, gate_a_b, gate_x_w, gate_x_b, lam, seq):
    m, d = x2.shape
    width = w_in.shape[1] // 2
    n_blocks, bw, _ = gate_a_w.shape
    rows = _pick(seq, LRU_ROWS)
    bps = seq // rows
    nb = m // rows
    assert bps % 2 == 0 and nb % 2 == 0
    last = nb - 1
    r_idx = jnp.arange(rows)
    t_of_r = (r_idx % SUBLANES) * (rows // SUBLANES) + r_idx // SUBLANES
    perm = (t_of_r[:, None] == r_idx[None, :]).astype(BF16)
    xa = lambda s: 2 * s + 1
    xb = lambda s: jnp.minimum(2 * s + 2, last)
    vec = lambda: pl.BlockSpec((1, width), lambda s: (0, 0))
    gate = lambda: pl.BlockSpec((n_blocks, bw, bw), lambda s: (0, 0, 0))
    return pl.pallas_call(
        functools.partial(_lru_in_kernel, n_blocks=n_blocks, rows=rows, bw=bw, blocks_per_seq=bps),
        grid=(nb // 2,),
        in_specs=[
            pl.BlockSpec((rows, d), lambda s: (0, 0)),
            pl.BlockSpec((1, 3, d), lambda s: (0, 0, 0)),
            pl.BlockSpec((rows, d), lambda s: (xa(s), 0)),
            pl.BlockSpec((1, 3, d), lambda s: (xa(s) // bps, 0, 0)),
            pl.BlockSpec((rows, d), lambda s: (xb(s), 0)),
            pl.BlockSpec((1, 3, d), lambda s: (xb(s) // bps, 0, 0)),
            pl.BlockSpec((1, d), lambda s: (0, 0)),
            pl.BlockSpec((d, 2 * width), lambda s: (0, 0)),
            pl.BlockSpec((rows, rows), lambda s: (0, 0)),
            pl.BlockSpec((CONV_WIDTH, width), lambda s: (0, 0)),
            vec(), gate(), vec(), gate(), vec(), vec(),
        ],
        out_specs=pl.BlockSpec((2 * rows, width), lambda s: (s, 0)),
        out_shape=jax.ShapeDtypeStruct((m, width), BF16),
        scratch_shapes=[
            pltpu.VMEM((rows, 2 * width), F32),
            pltpu.VMEM((rows, 2 * width), F32),
            pltpu.VMEM((SUBLANES, width), F32),
            pltpu.VMEM((rows, width), BF16),
        ],
        compiler_params=_cparams("arbitrary"),
        name="rglru_in",
    )(x2, mod_s, x2, mod_s, x2, mod_s, g_pre.reshape(1, d), w_in, perm, conv_w, conv_b.reshape(1, width),
      gate_a_w.astype(BF16), gate_a_b.reshape(1, width), gate_x_w.astype(BF16),
      gate_x_b.reshape(1, width), lam.reshape(1, width))


def kernel(x, c, ada_w, ada_b, norm_pre, norm_post, ffn_w13, ffn_w2, ev_w_in, ev_conv_w, ev_a_log,
           ev_dt_bias, ev_o_norm, ev_ret_norm, ev_w_out, od_w_in, od_conv_w, od_conv_b,
           od_gate_a_w, od_gate_a_b, od_gate_x_w, od_gate_x_b, od_lambda, od_w_out):
    b, seq, d = x.shape
    depth = ada_w.shape[0]
    m = b * seq
    mod = _ada(c, ada_w, ada_b).reshape(depth, b, N_SUB, 3, d)
    x2 = x.reshape(m, d)
    w13_b = ffn_w13.astype(BF16)
    w2_b = ffn_w2.astype(BF16)

    for layer in range(depth):
        mod_l = mod[layer]
        x2 = _ffn(x2, mod_l[:, 0], norm_pre[layer, 0], norm_post[layer, 0], w13_b, w2_b, layer, 0, seq, 0.5)
        if layer % 2 == 0:
            e = layer // 2
            n_heads = ev_a_log.shape[1]
            gw = n_heads * GDN_HEAD_DIM
            w_in = ev_w_in[e]
            w_small = jnp.pad(w_in[:, 3 * gw:3 * gw + 2 * n_heads], ((0, 0), (0, LANES - 2 * n_heads)))
            w_g = jnp.concatenate([w_in[:, :3 * gw], w_in[:, 3 * gw + 2 * n_heads:4 * gw + 2 * n_heads],
                                   w_small], axis=1).astype(BF16)
            w_r = w_in[:, 4 * gw + 2 * n_heads:].astype(BF16)
            o_a = _gdn_in(x2, mod_l[:, 1], norm_pre[layer, 1], w_g, ev_conv_w[e], ev_a_log[e],
                          ev_dt_bias[e], ev_o_norm[e], seq)
            o_b = _ret_in(x2, mod_l[:, 1], norm_pre[layer, 1], w_r, ev_ret_norm[e], seq)
            acts = [o_a, o_b]
            w_out = ev_w_out[e]
        else:
            o = layer // 2
            hs = _lru_in(x2, mod_l[:, 1], norm_pre[layer, 1], od_w_in[o].astype(BF16), od_conv_w[o],
                         od_conv_b[o], od_gate_a_w[o], od_gate_a_b[o], od_gate_x_w[o], od_gate_x_b[o],
                         od_lambda[o], seq)
            acts = [hs.reshape(m, -1)]
            w_out = od_w_out[o]
        x2 = _outproj(acts, w_out.astype(BF16), x2, mod_l[:, 1], norm_post[layer, 1], seq, 1.0)
        x2 = _ffn(x2, mod_l[:, 2], norm_pre[layer, 2], norm_post[layer, 2], w13_b, w2_b, layer, 1, seq, 0.5)
    return x2.reshape(b, seq, d)
```

```python
import functools
import math

import jax
import jax.numpy as jnp
from jax import lax
from jax.experimental import pallas as pl
from jax.experimental.pallas import tpu as pltpu

F32 = jnp.float32
BF16 = jnp.bfloat16

EPS = 1e-6
GDN_HEAD_DIM = 128
RET_HEAD_DIM = 256
CHUNK = 64
CONV_WIDTH = 4
ROPE_BASE = 10000.0
LRU_C = 8.0
N_SUB = 3

SUBLANES = 8
LANES = 128
V7X_VMEM_LIMIT_BYTES = 56 * 1024 * 1024


def _cparams(*semantics):
    return pltpu.CompilerParams(dimension_semantics=semantics,
                                vmem_limit_bytes=V7X_VMEM_LIMIT_BYTES)


def _dot(a, b):
    return jnp.dot(a, b, preferred_element_type=F32)


def _dot_nt(a, b):
    return lax.dot_general(a, b, (((1,), (1,)), ((), ())), preferred_element_type=F32)


def _dot_tn(a, b):
    return lax.dot_general(a, b, (((0,), (0,)), ((), ())), preferred_element_type=F32)


def _sigmoid(x):
    return jax.nn.sigmoid(x)


def _silu(x):
    return x * jax.nn.sigmoid(x)


def _softplus(x):
    return jnp.maximum(x, 0.0) + jnp.log1p(jnp.exp(-jnp.abs(x)))


def _gelu_tanh(x):
    c = math.sqrt(2.0 / math.pi)
    return 0.5 * x * (1.0 + jnp.tanh(c * (x + 0.044715 * (x * x * x))))


def _rms(x, gain):
    return x * lax.rsqrt(jnp.mean(x * x, axis=-1, keepdims=True) + EPS) * gain


def _inv_rms(x):
    return lax.rsqrt(jnp.mean(x * x, axis=-1, keepdims=True) + EPS)


def _norm_mod(x, g_pre, mod_ref):
    return (x * _inv_rms(x)) * (g_pre * (1.0 + mod_ref[0, 1:2, :])) + mod_ref[0, 0:1, :]


def _post_residual(x, f, g_post, mod_ref, res_w):
    return x + (f * _inv_rms(f)) * (g_post * (res_w * (1.0 + mod_ref[0, 2:3, :])))


def _pick(n, pref):
    if n <= pref:
        return n
    for t in range(pref - pref % LANES, 0, -LANES):
        if n % t == 0:
            return t
    raise ValueError(f"no lane-aligned tile of {n} at or below {pref}")


def _ada_kernel(c_ref, w_ref, b_ref, o_ref):
    a = _silu(c_ref[...]).astype(BF16)
    o_ref[0] = _dot(a, w_ref[0].astype(BF16)) + b_ref[0]


def _ada(c, ada_w, ada_b):
    n_layers, d, n = ada_w.shape
    b = c.shape[0]
    tn = _pick(n, 1024)
    return pl.pallas_call(
        _ada_kernel,
        grid=(n_layers, n // tn),
        in_specs=[
            pl.BlockSpec((b, d), lambda l, j: (0, 0)),
            pl.BlockSpec((1, d, tn), lambda l, j: (l, 0, j)),
            pl.BlockSpec((1, 1, tn), lambda l, j: (l, 0, j)),
        ],
        out_specs=pl.BlockSpec((1, b, tn), lambda l, j: (l, 0, j)),
        out_shape=jax.ShapeDtypeStruct((n_layers, b, n), F32),
        compiler_params=_cparams("arbitrary", "arbitrary"),
        name="ada_mod",
    )(c, ada_w, ada_b.reshape(n_layers, 1, n))


FFN_HALVES = 2


def _ffn_kernel(x_ref, mod_ref, gpre_ref, gpost_ref, w1_ref, w3_ref, w2_ref, o_ref, h_scr,
                *, res_w, n_f):
    j = pl.program_id(1)
    tm = x_ref.shape[0]
    part = tm // FFN_HALVES

    def down(rs):
        h = h_scr[rs, :]
        a = (_silu(_dot(h, w1_ref[...])) * _dot(h, w3_ref[...])).astype(BF16)
        return _dot(a, w2_ref[...])

    @pl.when(j == 0)
    def _():
        for p in range(FFN_HALVES):
            rs = slice(p * part, (p + 1) * part)
            h_scr[rs, :] = _norm_mod(x_ref[rs, :], gpre_ref[...], mod_ref).astype(BF16)
            o_ref[rs, :] = down(rs)

    @pl.when((j > 0) & (j < n_f - 1))
    def _():
        for p in range(FFN_HALVES):
            rs = slice(p * part, (p + 1) * part)
            o_ref[rs, :] += down(rs)

    @pl.when(j == n_f - 1)
    def _():
        for p in range(FFN_HALVES):
            rs = slice(p * part, (p + 1) * part)
            f = o_ref[rs, :] + down(rs)
            o_ref[rs, :] = _post_residual(x_ref[rs, :], f, gpost_ref[...], mod_ref, res_w)


def _ffn(x2, mod_s, g_pre, g_post, w13_all, w2_all, layer, which, seq, res_w):
    m, d = x2.shape
    f = w2_all.shape[2]
    tm = _pick(seq, 1024)
    tf = _pick(f, 512)
    n_f = f // tf
    per_b = seq // tm
    assert n_f >= 2 and tm % (FFN_HALVES * 2 * SUBLANES) == 0
    return pl.pallas_call(
        functools.partial(_ffn_kernel, res_w=res_w, n_f=n_f),
        grid=(m // tm, n_f),
        in_specs=[
            pl.BlockSpec((tm, d), lambda i, j: (i, 0)),
            pl.BlockSpec((1, 3, d), lambda i, j: (i // per_b, 0, 0)),
            pl.BlockSpec((1, d), lambda i, j: (0, 0)),
            pl.BlockSpec((1, d), lambda i, j: (0, 0)),
            pl.BlockSpec((None, None, d, tf), lambda i, j: (layer, which, 0, j)),
            pl.BlockSpec((None, None, d, tf), lambda i, j: (layer, which, 0, j + n_f)),
            pl.BlockSpec((None, None, tf, d), lambda i, j: (layer, which, j, 0)),
        ],
        out_specs=pl.BlockSpec((tm, d), lambda i, j: (i, 0)),
        out_shape=jax.ShapeDtypeStruct((m, d), F32),
        scratch_shapes=[pltpu.VMEM((tm, d), BF16)],
        compiler_params=_cparams("arbitrary", "arbitrary"),
        name="ffn",
    )(x2, mod_s, g_pre.reshape(1, d), g_post.reshape(1, d), w13_all, w13_all, w2_all)


def _outproj_kernel(*refs, n_in, res_w):
    a_refs = refs[:n_in]
    w_refs = refs[n_in:2 * n_in]
    x_ref, mod_ref, gpost_ref, o_ref = refs[2 * n_in:]
    f = _dot(a_refs[0][...], w_refs[0][...])
    for a_ref, w_ref in zip(a_refs[1:], w_refs[1:]):
        f = f + _dot(a_ref[...], w_ref[...])
    o_ref[...] = _post_residual(x_ref[...], f, gpost_ref[...], mod_ref, res_w)


def _outproj(acts, w_out, x2, mod_s, g_post, seq, res_w):
    m, d = x2.shape
    tm = _pick(seq, 512)
    per_b = seq // tm
    n_in = len(acts)
    in_specs, args = [], []
    for a in acts:
        in_specs.append(pl.BlockSpec((tm, a.shape[1]), lambda i: (i, 0)))
        args.append(a)
    row = 0
    for a in acts:
        wi = a.shape[1]
        assert row % wi == 0
        in_specs.append(pl.BlockSpec((wi, d), lambda i, r=row // wi: (r, 0)))
        args.append(w_out)
        row += wi
    in_specs += [
        pl.BlockSpec((tm, d), lambda i: (i, 0)),
        pl.BlockSpec((1, 3, d), lambda i: (i // per_b, 0, 0)),
        pl.BlockSpec((1, d), lambda i: (0, 0)),
    ]
    args += [x2, mod_s, g_post.reshape(1, d)]
    return pl.pallas_call(
        functools.partial(_outproj_kernel, n_in=n_in, res_w=res_w),
        grid=(m // tm,),
        in_specs=in_specs,
        out_specs=pl.BlockSpec((tm, d), lambda i: (i, 0)),
        out_shape=jax.ShapeDtypeStruct((m, d), F32),
        compiler_params=_cparams("arbitrary"),
        name="mixer_outproj",
    )(*args)


def _conv_from_scratch(cs_ref, w_ref, rows, lo, hi):
    acc = cs_ref[SUBLANES:SUBLANES + rows, lo:hi] * w_ref[CONV_WIDTH - 1:CONV_WIDTH, lo:hi]
    for back in range(1, CONV_WIDTH):
        tap = CONV_WIDTH - 1 - back
        acc = acc + cs_ref[SUBLANES - back:SUBLANES - back + rows, lo:hi] * w_ref[tap:tap + 1, lo:hi]
    return acc


GDN_BLOCK = 256
GDN_PACK = 4


def _tile_rows(x, n):
    return jnp.concatenate([x] * n, axis=0)


def _gdn_block(p_ref, p_prev_ref, seq_start, between, cw_ref, alog_ref, dtb_ref, onorm_ref,
               o_ref, row0, s_scr, qn_scr, kn_scr, kb_scr, vb_scr, kbe_scr, kst_scr, egq_scr,
               a_scr, qk_scr, xo_scr, l_scr, ku_scr, oin_scr, *, n_heads, rows, n_pieces):
    dk = GDN_HEAD_DIM
    gw = n_heads * dk
    n_chunks = rows // CHUNK
    n_groups = n_heads // GDN_PACK
    pw = GDN_PACK * CHUNK
    gk = GDN_PACK * dk

    tail = p_prev_ref[rows:rows + SUBLANES, 0:3 * gw]
    p_ref[0:SUBLANES, 0:3 * gw] = jnp.where(seq_start, 0.0, tail)
    cs_scr = p_ref
    n_between = [0]

    def next_piece():
        between(n_between[0])
        n_between[0] += 1

    sm = p_ref[SUBLANES:SUBLANES + rows, 4 * gw:4 * gw + LANES]
    beta_all = _sigmoid(sm)
    g_all = -jnp.exp(alog_ref[...]) * _softplus(sm + dtb_ref[...])
    in_chunk = lax.broadcasted_iota(jnp.int32, (rows, LANES), 0) & (CHUNK - 1)
    d = 1
    while d < CHUNK:
        g_all = g_all + jnp.where(in_chunk >= d, pltpu.roll(g_all, d, 0), 0.0)
        d *= 2
    eg_all = jnp.exp(g_all)
    glast_all = jnp.concatenate(
        [jnp.broadcast_to(g_all[(c + 1) * CHUNK - 1:(c + 1) * CHUNK, :], (CHUNK, LANES))
         for c in range(n_chunks)], axis=0)
    ekl_all = jnp.exp(glast_all - g_all)

    for h in range(n_heads):
        next_piece()
        lo, hi = h * dk, (h + 1) * dk
        bcast = lambda arr, idx: jnp.broadcast_to(arr[:, idx:idx + 1], (rows, dk))
        beta_b = bcast(beta_all, h)
        eg_b = bcast(eg_all, n_heads + h)
        q = _silu(_conv_from_scratch(cs_scr, cw_ref, rows, lo, hi))
        qn = (q * lax.rsqrt(jnp.sum(q * q, axis=-1, keepdims=True) + EPS)) * dk ** -0.5
        qn_scr[:, lo:hi] = qn.astype(BF16)
        egq_scr[:, lo:hi] = qn * eg_b
        k = _silu(_conv_from_scratch(cs_scr, cw_ref, rows, gw + lo, gw + hi))
        kn = k * lax.rsqrt(jnp.sum(k * k, axis=-1, keepdims=True) + EPS)
        kb = kn * beta_b
        kn_scr[:, lo:hi] = kn.astype(BF16)
        kb_scr[:, lo:hi] = kb.astype(BF16)
        kbe_scr[:, lo:hi] = kb * eg_b
        kst_scr[:, lo:hi] = (kn * bcast(ekl_all, n_heads + h)).astype(BF16)
        v = _silu(_conv_from_scratch(cs_scr, cw_ref, rows, 2 * gw + lo, 2 * gw + hi))
        vb_scr[:, lo:hi] = v * beta_b

    ri = lax.broadcasted_iota(jnp.int32, (CHUNK, pw), 0)
    li = lax.broadcasted_iota(jnp.int32, (CHUNK, pw), 1)
    cj = li & (CHUNK - 1)
    lh = li // CHUNK
    causal = ri >= cj
    strict = ri > cj
    eye = ri == cj
    bd_r = lax.broadcasted_iota(jnp.int32, (pw, pw), 0) // CHUNK
    bd_c = lax.broadcasted_iota(jnp.int32, (pw, pw), 1) // CHUNK
    bd_sq = bd_r == bd_c
    bk_r = lax.broadcasted_iota(jnp.int32, (pw, gk), 0) // CHUNK
    bk_c = lax.broadcasted_iota(jnp.int32, (pw, gk), 1) // dk
    bd_k = bk_r == bk_c
    zero_sq = jnp.zeros((pw, pw), BF16)
    zero_k = jnp.zeros((pw, gk), BF16)

    def block_diag(x_rp):
        return jnp.where(bd_sq, _tile_rows(x_rp.astype(BF16), GDN_PACK), zero_sq)

    def block_diag_k(x_cat):
        return jnp.where(bd_k, _tile_rows(x_cat, GDN_PACK), zero_k)

    probs = [(c, gi) for c in range(n_chunks) for gi in range(n_groups)]
    for p, (c, gi) in enumerate(probs):
        next_piece()
        r0, r1 = c * CHUNK, (c + 1) * CHUNK
        c0 = gi * gk
        gcs = [jnp.broadcast_to(g_all[r0:r1, n_heads + gi * GDN_PACK + s:n_heads + gi * GDN_PACK + s + 1],
                                (CHUNK, pw)) for s in range(GDN_PACK)]
        gc = gcs[GDN_PACK - 1]
        for s in range(GDN_PACK - 2, -1, -1):
            gc = jnp.where(lh == s, gcs[s], gc)
        gr = jnp.sum(jnp.where(eye, gc, 0.0), axis=0, keepdims=True)
        decay = jnp.exp(jnp.where(causal, gc - gr, -jnp.inf))
        lhs = jnp.concatenate([kb_scr[r0:r1, c0:c0 + gk], qn_scr[r0:r1, c0:c0 + gk]], axis=0)
        kq = _dot_nt(lhs, block_diag_k(kn_scr[r0:r1, c0:c0 + gk]))
        a = jnp.where(strict, kq[0:CHUNK] * decay, 0.0)
        a_scr[p] = a
        qk_scr[p] = jnp.where(causal, kq[CHUNK:2 * CHUNK] * decay, 0.0)
        xo_scr[p] = -jnp.where((ri - cj == 1) & ((ri & 1) == 1), a, 0.0)

    next_piece()
    s = 2
    while s < CHUNK:
        rb = ri // s
        off = ((rb & 1) == 1) & ((cj // s) == rb - 1)
        for p in range(len(probs)):
            a_off = jnp.where(off, a_scr[p], 0.0)
            xo = xo_scr[p]
            y = a_off + _dot(xo.astype(BF16), block_diag(a_off))
            xo_scr[p] = xo - (y + _dot(y.astype(BF16), block_diag(xo)))
        s *= 2

    for p, (c, gi) in enumerate(probs):
        r0, r1 = c * CHUNK, (c + 1) * CHUNK
        rhs = jnp.concatenate(
            [jnp.concatenate([vb_scr[r0:r1, h * dk:(h + 1) * dk], kbe_scr[r0:r1, h * dk:(h + 1) * dk]], axis=1)
             for h in range(gi * GDN_PACK, (gi + 1) * GDN_PACK)], axis=0)
        sol = rhs + _dot(block_diag(xo_scr[p]), rhs.astype(BF16))
        sol_b = sol.astype(BF16)
        qks = _dot(block_diag(qk_scr[p]), sol_b)
        ks = _dot_tn(block_diag_k(kst_scr[r0:r1, gi * gk:(gi + 1) * gk]), sol_b)
        for s_ in range(GDN_PACK):
            h = gi * GDN_PACK + s_
            half = (h % 2) * dk
            rr = slice(s_ * CHUNK, (s_ + 1) * CHUNK)
            kr = slice(s_ * dk, (s_ + 1) * dk)
            q_eff = egq_scr[r0:r1, h * dk:(h + 1) * dk] - qks[rr, dk:2 * dk]
            l_scr[c, h // 2, 0:dk, half:half + dk] = ks[kr, dk:2 * dk].astype(BF16)
            l_scr[c, h // 2, dk:dk + CHUNK, half:half + dk] = q_eff.astype(BF16)
            ku_scr[c, h] = ks[kr, 0:dk]
            oin_scr[r0:r1, h * dk:(h + 1) * dk] = qks[rr, 0:dk]

    zero = jnp.zeros((dk, dk), BF16)
    for c in range(n_chunks):
        r0, r1 = c * CHUNK, (c + 1) * CHUNK
        for pr in range(n_heads // 2):
            h0, h1 = 2 * pr, 2 * pr + 1
            s0 = s_scr[h0]
            s1 = s_scr[h1]
            if c == 0:
                s0 = jnp.where(seq_start, 0.0, s0)
                s1 = jnp.where(seq_start, 0.0, s1)
            s_bd = jnp.concatenate(
                [jnp.concatenate([s0.astype(BF16), zero], axis=1),
                 jnp.concatenate([zero, s1.astype(BF16)], axis=1)], axis=0)
            r = _dot(l_scr[c, pr], s_bd)
            for h, st, off_ in ((h0, s0, 0), (h1, s1, dk)):
                cd = eg_all[r1 - 1:r1, n_heads + h:n_heads + h + 1]
                s_scr[h] = st * cd - r[0:dk, off_:off_ + dk] + ku_scr[c, h]
                o = r[dk:dk + CHUNK, off_:off_ + dk] + oin_scr[r0:r1, h * dk:(h + 1) * dk]
                z = p_ref[SUBLANES + r0:SUBLANES + r1, 3 * gw + h * dk:3 * gw + (h + 1) * dk]
                out = _rms(o, onorm_ref[...]) * _silu(z)
                o_ref[row0 + r0:row0 + r1, h * dk:(h + 1) * dk] = out.astype(o_ref.dtype)

    while n_between[0] < n_pieces:
        next_piece()


GDN_PIECES = 17


def _gdn_in_kernel(xa_ref, moda_ref, xb_ref, modb_ref, gpre_ref, win_ref,
                   cw_ref, alog_ref, dtb_ref, onorm_ref, o_ref, p0_scr, p1_scr, *scratch,
                   n_heads, rows, blocks_per_seq):
    s = pl.program_id(0)
    n_cols = win_ref.shape[1]
    piece = -(-n_cols // (GDN_PIECES * LANES)) * LANES
    bounds = [(c, min(c + piece, n_cols)) for c in range(0, n_cols, piece)]
    assert len(bounds) <= GDN_PIECES

    def in_proj_piece(h, p_ref, i):
        if i < len(bounds):
            c0, c1 = bounds[i]
            p_ref[SUBLANES:SUBLANES + rows, c0:c1] = _dot(h, win_ref[:, c0:c1])

    n_pieces = len(bounds)

    def normed(x_ref, mod_ref):
        return _norm_mod(x_ref[...], gpre_ref[...], mod_ref).astype(BF16)

    @pl.when(s == 0)
    def _():
        p0_scr[...] = jnp.zeros_like(p0_scr)
        p1_scr[...] = jnp.zeros_like(p1_scr)
        scratch[0][...] = jnp.zeros_like(scratch[0])

    seq_start = (2 * s - 2) % blocks_per_seq == 0
    ha = normed(xa_ref, moda_ref)
    _gdn_block(p0_scr, p1_scr, seq_start, lambda i: in_proj_piece(ha, p1_scr, i),
               cw_ref, alog_ref, dtb_ref, onorm_ref, o_ref, 0, *scratch, n_heads=n_heads, rows=rows,
               n_pieces=n_pieces)
    hb = normed(xb_ref, modb_ref)
    _gdn_block(p1_scr, p0_scr, False, lambda i: in_proj_piece(hb, p0_scr, i),
               cw_ref, alog_ref, dtb_ref, onorm_ref, o_ref, rows, *scratch, n_heads=n_heads, rows=rows,
               n_pieces=n_pieces)


def _gdn_in(x2, mod_s, g_pre, w_g, conv_w, a_log, dt_bias, o_norm, seq):
    m, d = x2.shape
    n_heads = a_log.shape[0]
    assert n_heads % GDN_PACK == 0 and GDN_HEAD_DIM == LANES
    gw = n_heads * GDN_HEAD_DIM
    n_cols = w_g.shape[1]
    assert n_cols == 4 * gw + LANES
    rows = _pick(seq, GDN_BLOCK)
    bps = seq // rows
    nb = m // rows
    assert bps % 2 == 0 and nb % 2 == 0
    last = nb - 1
    n_chunks = rows // CHUNK
    n_prob = n_chunks * (n_heads // GDN_PACK)
    pad = LANES - 2 * n_heads
    alog_row = jnp.pad(a_log, (n_heads, pad)).reshape(1, LANES)
    dtb_row = jnp.pad(dt_bias, (n_heads, pad)).reshape(1, LANES)
    xa = lambda s: jnp.maximum(2 * s - 1, 0)
    xb = lambda s: jnp.minimum(2 * s, last)
    return pl.pallas_call(
        functools.partial(_gdn_in_kernel, n_heads=n_heads, rows=rows, blocks_per_seq=bps),
        grid=(nb // 2 + 1,),
        in_specs=[
            pl.BlockSpec((rows, d), lambda s: (xa(s), 0)),
            pl.BlockSpec((1, 3, d), lambda s: (xa(s) // bps, 0, 0)),
            pl.BlockSpec((rows, d), lambda s: (xb(s), 0)),
            pl.BlockSpec((1, 3, d), lambda s: (xb(s) // bps, 0, 0)),
            pl.BlockSpec((1, d), lambda s: (0, 0)),
            pl.BlockSpec((d, n_cols), lambda s: (0, 0)),
            pl.BlockSpec((CONV_WIDTH, 3 * gw), lambda s: (0, 0)),
            pl.BlockSpec((1, LANES), lambda s: (0, 0)),
            pl.BlockSpec((1, LANES), lambda s: (0, 0)),
            pl.BlockSpec((1, GDN_HEAD_DIM), lambda s: (0, 0)),
        ],
        out_specs=pl.BlockSpec((2 * rows, gw), lambda s: (jnp.maximum(s - 1, 0), 0)),
        out_shape=jax.ShapeDtypeStruct((m, gw), BF16),
        scratch_shapes=[
            pltpu.VMEM((rows + SUBLANES, n_cols), F32),
            pltpu.VMEM((rows + SUBLANES, n_cols), F32),
            pltpu.VMEM((n_heads, GDN_HEAD_DIM, GDN_HEAD_DIM), F32),
            pltpu.VMEM((rows, gw), BF16),
            pltpu.VMEM((rows, gw), BF16),
            pltpu.VMEM((rows, gw), BF16),
            pltpu.VMEM((rows, gw), F32),
            pltpu.VMEM((rows, gw), F32),
            pltpu.VMEM((rows, gw), BF16),
            pltpu.VMEM((rows, gw), F32),
            pltpu.VMEM((n_prob, CHUNK, GDN_PACK * CHUNK), F32),
            pltpu.VMEM((n_prob, CHUNK, GDN_PACK * CHUNK), F32),
            pltpu.VMEM((n_prob, CHUNK, GDN_PACK * CHUNK), F32),
            pltpu.VMEM((n_chunks, n_heads // 2, GDN_HEAD_DIM + CHUNK, 2 * GDN_HEAD_DIM), BF16),
            pltpu.VMEM((n_chunks, n_heads, GDN_HEAD_DIM, GDN_HEAD_DIM), F32),
            pltpu.VMEM((rows, gw), F32),
        ],
        compiler_params=_cparams("arbitrary"),
        name="gdn_in",
    )(x2, mod_s, x2, mod_s, g_pre.reshape(1, d), w_g, conv_w, alog_row, dtb_row,
      o_norm.reshape(1, GDN_HEAD_DIM))


def _rotary(x, cos, sin):
    half = x.shape[-1] // 2
    x1, x2 = x[:, :half], x[:, half:]
    return jnp.concatenate([x1 * cos - x2 * sin, x2 * cos + x1 * sin], axis=-1)


RET_PIECES = 8


def _ret_block(p_ref, seq_start, between, cos, sin, norm_ref, o_ref, row0, s_scr, *, n_heads, rows,
               n_pieces):
    dk = RET_HEAD_DIM
    rw = n_heads * dk
    ri = lax.broadcasted_iota(jnp.int32, (rows, rows), 0)
    ci = lax.broadcasted_iota(jnp.int32, (rows, rows), 1)
    delta = (ri - ci).astype(F32)
    causal = ri >= ci
    pos = lax.broadcasted_iota(jnp.int32, (rows, 1), 0).astype(F32)
    done = 0

    for h in range(n_heads):
        while done < min(n_pieces, (h + 1) * -(-n_pieces // n_heads)):
            between(done)
            done += 1
        lo, hi = h * dk, (h + 1) * dk
        log_gamma = math.log1p(-(2.0 ** (-5.0 - h)))
        q = _rotary(p_ref[:, lo:hi], cos, sin)
        k = _rotary(p_ref[:, rw + lo:rw + hi], cos, sin) * dk ** -0.5
        v_b = p_ref[:, 2 * rw + lo:2 * rw + hi].astype(BF16)
        intra = jnp.exp(jnp.where(causal, delta * log_gamma, -jnp.inf))
        scores = _dot_nt(q.astype(BF16), k.astype(BF16)) * intra
        q_in = q * jnp.exp((pos + 1.0) * log_gamma)
        k_st = k * jnp.exp((rows - 1.0 - pos) * log_gamma)
        state = jnp.where(seq_start, 0.0, s_scr[h])
        o = _dot(scores.astype(BF16), v_b) + _dot(q_in.astype(BF16), state.astype(BF16))
        s_scr[h] = state * math.exp(rows * log_gamma) + _dot_tn(k_st.astype(BF16), v_b)
        out = _rms(o, norm_ref[:, lo:hi]) * _silu(p_ref[:, 3 * rw + lo:3 * rw + hi])
        o_ref[row0:row0 + rows, lo:hi] = out.astype(o_ref.dtype)

    while done < n_pieces:
        between(done)
        done += 1


def _ret_in_kernel(xa_ref, moda_ref, xb_ref, modb_ref, gpre_ref, win_ref, cos_ref, sin_ref, norm_ref,
                   o_ref, p0_scr, p1_scr, s_scr, *, n_heads, rows, blocks_per_seq):
    s = pl.program_id(0)
    piece = win_ref.shape[1] // RET_PIECES

    def in_proj_piece(h, p_ref, i):
        p_ref[:, i * piece:(i + 1) * piece] = _dot(h, win_ref[:, i * piece:(i + 1) * piece])

    def normed(x_ref, mod_ref):
        return _norm_mod(x_ref[...], gpre_ref[...], mod_ref).astype(BF16)

    @pl.when(s == 0)
    def _():
        p0_scr[...] = jnp.zeros_like(p0_scr)
        p1_scr[...] = jnp.zeros_like(p1_scr)
        s_scr[...] = jnp.zeros_like(s_scr)

    seq_start = (2 * s - 2) % blocks_per_seq == 0
    ha = normed(xa_ref, moda_ref)
    _ret_block(p0_scr, seq_start, lambda i: in_proj_piece(ha, p1_scr, i), cos_ref[0:rows, :],
               sin_ref[0:rows, :], norm_ref, o_ref, 0, s_scr, n_heads=n_heads, rows=rows,
               n_pieces=RET_PIECES)
    hb = normed(xb_ref, modb_ref)
    _ret_block(p1_scr, False, lambda i: in_proj_piece(hb, p0_scr, i), cos_ref[rows:2 * rows, :],
               sin_ref[rows:2 * rows, :], norm_ref, o_ref, rows, s_scr, n_heads=n_heads, rows=rows,
               n_pieces=RET_PIECES)


def _ret_in(x2, mod_s, g_pre, w_r, ret_norm, seq):
    m, d = x2.shape
    rw = ret_norm.shape[0]
    n_heads = rw // RET_HEAD_DIM
    n_cols = w_r.shape[1]
    assert n_cols == 4 * rw and n_cols % (RET_PIECES * LANES) == 0
    rows = _pick(seq, 256)
    bps = seq // rows
    nb = m // rows
    assert bps % 2 == 0 and nb % 2 == 0
    last = nb - 1
    half = RET_HEAD_DIM // 2
    inv_freq = ROPE_BASE ** (-jnp.arange(half, dtype=F32) / half)
    ang = jnp.arange(seq, dtype=F32)[:, None] * inv_freq[None, :]
    cos, sin = jnp.cos(ang), jnp.sin(ang)
    xa = lambda s: jnp.maximum(2 * s - 1, 0)
    xb = lambda s: jnp.minimum(2 * s, last)
    pos_blk = lambda s: (jnp.maximum(2 * s - 2, 0) % bps) // 2
    return pl.pallas_call(
        functools.partial(_ret_in_kernel, n_heads=n_heads, rows=rows, blocks_per_seq=bps),
        grid=(nb // 2 + 1,),
        in_specs=[
            pl.BlockSpec((rows, d), lambda s: (xa(s), 0)),
            pl.BlockSpec((1, 3, d), lambda s: (xa(s) // bps, 0, 0)),
            pl.BlockSpec((rows, d), lambda s: (xb(s), 0)),
            pl.BlockSpec((1, 3, d), lambda s: (xb(s) // bps, 0, 0)),
            pl.BlockSpec((1, d), lambda s: (0, 0)),
            pl.BlockSpec((d, n_cols), lambda s: (0, 0)),
            pl.BlockSpec((2 * rows, half), lambda s: (pos_blk(s), 0)),
            pl.BlockSpec((2 * rows, half), lambda s: (pos_blk(s), 0)),
            pl.BlockSpec((1, rw), lambda s: (0, 0)),
        ],
        out_specs=pl.BlockSpec((2 * rows, rw), lambda s: (jnp.maximum(s - 1, 0), 0)),
        out_shape=jax.ShapeDtypeStruct((m, rw), BF16),
        scratch_shapes=[
            pltpu.VMEM((rows, n_cols), F32),
            pltpu.VMEM((rows, n_cols), F32),
            pltpu.VMEM((n_heads, RET_HEAD_DIM, RET_HEAD_DIM), F32),
        ],
        compiler_params=_cparams("arbitrary"),
        name="retention_in",
    )(x2, mod_s, x2, mod_s, g_pre.reshape(1, d), w_r, cos, sin, ret_norm.reshape(1, rw))


LRU_ROWS = 256


def _lru_block(p_ref, p_prev_ref, seq_start, between, perm_ref, cw_ref, cb_ref, wa_ref, ba_ref, wx_ref,
               bx_ref, lam_ref, o_ref, row0, h_scr, act_scr, *, n_blocks, rows, bw):
    width = n_blocks * bw
    nj = rows // SUBLANES
    sub = lax.broadcasted_iota(jnp.int32, (SUBLANES, bw), 0)
    tails = [jnp.where(seq_start, 0.0, p_prev_ref[rows - SUBLANES * (k - 1) - 1:rows - SUBLANES * (k - 1),
                                                  width:2 * width]) for k in range(1, CONV_WIDTH)]

    for n in range(n_blocks):
        between(2 * n)
        lo, hi = n * bw, (n + 1) * bw
        xp = p_ref[:, width + lo:width + hi]

        def edge(k):
            prev = pltpu.roll(xp[rows - k * SUBLANES:rows - (k - 1) * SUBLANES], 1, 0)
            return jnp.where(sub == 0, tails[k - 1][:, lo:hi], prev)

        edges = [edge(k) for k in range(1, CONV_WIDTH)]
        xc = xp * cw_ref[CONV_WIDTH - 1:CONV_WIDTH, lo:hi]
        for back in range(1, CONV_WIDTH):
            tap = CONV_WIDTH - 1 - back
            shifted = jnp.concatenate(edges[:back][::-1] + [xp[:rows - back * SUBLANES]], axis=0)
            xc = xc + shifted * cw_ref[tap:tap + 1, lo:hi]
        xc = xc + cb_ref[:, lo:hi]
        xc_b = xc.astype(BF16)
        r = _sigmoid(_dot(xc_b, wa_ref[n]) + ba_ref[:, lo:hi])
        i = _sigmoid(_dot(xc_b, wx_ref[n]) + bx_ref[:, lo:hi])
        log_a = (-LRU_C * r) * _softplus(-lam_ref[:, lo:hi])
        a = jnp.exp(log_a)
        mult = jnp.sqrt(-jnp.tanh(log_a) * (a * a + 1.0))
        bv = mult * (i * xc)

        h = bv[0:SUBLANES]
        ac = a[0:SUBLANES]
        hl, al = [h], [ac]
        for j in range(1, nj):
            aj = a[j * SUBLANES:(j + 1) * SUBLANES]
            h = aj * h + bv[j * SUBLANES:(j + 1) * SUBLANES]
            ac = aj * ac
            hl.append(h)
            al.append(ac)
        between(2 * n + 1)
        c = jnp.where(seq_start, 0.0, h_scr[0:1, lo:hi])
        cs = []
        for s in range(SUBLANES):
            cs.append(c)
            c = h[s:s + 1] + ac[s:s + 1] * c
        h_scr[0:1, lo:hi] = c
        c_in = jnp.concatenate(cs, axis=0)
        hs = jnp.concatenate([hl[j] + al[j] * c_in for j in range(nj)], axis=0)
        act_scr[:, lo:hi] = (hs * _gelu_tanh(p_ref[:, lo:hi])).astype(BF16)

    o_ref[row0:row0 + rows, :] = _dot_tn(perm_ref[...], act_scr[...]).astype(o_ref.dtype)


def _lru_in_kernel(x0_ref, mod0_ref, xa_ref, moda_ref, xb_ref, modb_ref, gpre_ref, win_ref, perm_ref,
                   cw_ref, cb_ref, wa_ref, ba_ref, wx_ref, bx_ref, lam_ref, o_ref,
                   p0_scr, p1_scr, h_scr, act_scr, *, n_blocks, rows, bw, blocks_per_seq):
    s = pl.program_id(0)

    def in_proj_cols(h, p_ref, n):
        cw = win_ref.shape[1] // (2 * n_blocks)
        p_ref[:, n * cw:(n + 1) * cw] = _dot(h, win_ref[:, n * cw:(n + 1) * cw])

    def normed(x_ref, mod_ref):
        h = _norm_mod(x_ref[...], gpre_ref[...], mod_ref).astype(BF16)
        return _dot(perm_ref[...], h).astype(BF16)

    @pl.when(s == 0)
    def _():
        h0 = normed(x0_ref, mod0_ref)
        for n in range(2 * n_blocks):
            in_proj_cols(h0, p0_scr, n)
        p1_scr[...] = jnp.zeros_like(p1_scr)
        h_scr[...] = jnp.zeros_like(h_scr)

    lru = functools.partial(_lru_block, perm_ref=perm_ref, cw_ref=cw_ref, cb_ref=cb_ref, wa_ref=wa_ref,
                            ba_ref=ba_ref, wx_ref=wx_ref, bx_ref=bx_ref, lam_ref=lam_ref, o_ref=o_ref,
                            h_scr=h_scr, act_scr=act_scr, n_blocks=n_blocks, rows=rows, bw=bw)
    seq_start = (2 * s) % blocks_per_seq == 0
    ha = normed(xa_ref, moda_ref)
    lru(p0_scr, p1_scr, seq_start, lambda n: in_proj_cols(ha, p1_scr, n), row0=0)
    hb = normed(xb_ref, modb_ref)
    lru(p1_scr, p0_scr, False, lambda n: in_proj_cols(hb, p0_scr, n), row0=rows)


def _lru_in(x2, mod_s, g_pre, w_in, conv_w, conv_b, gate_a_w, gate_a_b, gate_x_w, gate_x_b, lam, seq):
    m, d = x2.shape
    width = w_in.shape[1] // 2
    n_blocks, bw, _ = gate_a_w.shape
    rows = _pick(seq, LRU_ROWS)
    bps = seq // rows
    nb = m // rows
    assert bps % 2 == 0 and nb % 2 == 0
    last = nb - 1
    r_idx = jnp.arange(rows)
    t_of_r = (r_idx % SUBLANES) * (rows // SUBLANES) + r_idx // SUBLANES
    perm = (t_of_r[:, None] == r_idx[None, :]).astype(BF16)
    xa = lambda s: 2 * s + 1
    xb = lambda s: jnp.minimum(2 * s + 2, last)
    vec = lambda: pl.BlockSpec((1, width), lambda s: (0, 0))
    gate = lambda: pl.BlockSpec((n_blocks, bw, bw), lambda s: (0, 0, 0))
    return pl.pallas_call(
        functools.partial(_lru_in_kernel, n_blocks=n_blocks, rows=rows, bw=bw, blocks_per_seq=bps),
        grid=(nb // 2,),
        in_specs=[
            pl.BlockSpec((rows, d), lambda s: (0, 0)),
            pl.BlockSpec((1, 3, d), lambda s: (0, 0, 0)),
            pl.BlockSpec((rows, d), lambda s: (xa(s), 0)),
            pl.BlockSpec((1, 3, d), lambda s: (xa(s) // bps, 0, 0)),
            pl.BlockSpec((rows, d), lambda s: (xb(s), 0)),
            pl.BlockSpec((1, 3, d), lambda s: (xb(s) // bps, 0, 0)),
            pl.BlockSpec((1, d), lambda s: (0, 0)),
            pl.BlockSpec((d, 2 * width), lambda s: (0, 0)),
            pl.BlockSpec((rows, rows), lambda s: (0, 0)),
            pl.BlockSpec((CONV_WIDTH, width), lambda s: (0, 0)),
            vec(), gate(), vec(), gate(), vec(), vec(),
        ],
        out_specs=pl.BlockSpec((2 * rows, width), lambda s: (s, 0)),
        out_shape=jax.ShapeDtypeStruct((m, width), BF16),
        scratch_shapes=[
            pltpu.VMEM((rows, 2 * width), F32),
            pltpu.VMEM((rows, 2 * width), F32),
            pltpu.VMEM((SUBLANES, width), F32),
            pltpu.VMEM((rows, width), BF16),
        ],
        compiler_params=_cparams("arbitrary"),
        name="rglru_in",
    )(x2, mod_s, x2, mod_s, x2, mod_s, g_pre.reshape(1, d), w_in, perm, conv_w, conv_b.reshape(1, width),
      gate_a_w.astype(BF16), gate_a_b.reshape(1, width), gate_x_w.astype(BF16),
      gate_x_b.reshape(1, width), lam.reshape(1, width))


def kernel(x, c, ada_w, ada_b, norm_pre, norm_post, ffn_w13, ffn_w2, ev_w_in, ev_conv_w, ev_a_log,
           ev_dt_bias, ev_o_norm, ev_ret_norm, ev_w_out, od_w_in, od_conv_w, od_conv_b,
           od_gate_a_w, od_gate_a_b, od_gate_x_w, od_gate_x_b, od_lambda, od_w_out):
    b, seq, d = x.shape
    depth = ada_w.shape[0]
    m = b * seq
    mod = _ada(c, ada_w, ada_b).reshape(depth, b, N_SUB, 3, d)
    x2 = x.reshape(m, d)
    w13_b = ffn_w13.astype(BF16)
    w2_b = ffn_w2.astype(BF16)

    for layer in range(depth):
        mod_l = mod[layer]
        x2 = _ffn(x2, mod_l[:, 0], norm_pre[layer, 0], norm_post[layer, 0], w13_b, w2_b, layer, 0, seq, 0.5)
        if layer % 2 == 0:
            e = layer // 2
            n_heads = ev_a_log.shape[1]
            gw = n_heads * GDN_HEAD_DIM
            w_in = ev_w_in[e].astype(BF16)
            w_small = jnp.pad(w_in[:, 3 * gw:3 * gw + 2 * n_heads], ((0, 0), (0, LANES - 2 * n_heads)))
            w_g = jnp.concatenate([w_in[:, :3 * gw], w_in[:, 3 * gw + 2 * n_heads:4 * gw + 2 * n_heads],
                                   w_small], axis=1)
            w_r = w_in[:, 4 * gw + 2 * n_heads:]
            o_a = _gdn_in(x2, mod_l[:, 1], norm_pre[layer, 1], w_g, ev_conv_w[e], ev_a_log[e],
                          ev_dt_bias[e], ev_o_norm[e], seq)
            o_b = _ret_in(x2, mod_l[:, 1], norm_pre[layer, 1], w_r, ev_ret_norm[e], seq)
            acts = [o_a, o_b]
            w_out = ev_w_out[e]
        else:
            o = layer // 2
            hs = _lru_in(x2, mod_l[:, 1], norm_pre[layer, 1], od_w_in[o].astype(BF16), od_conv_w[o],
                         od_conv_b[o], od_gate_a_w[o], od_gate_a_b[o], od_gate_x_w[o], od_gate_x_b[o],
                         od_lambda[o], seq)
            acts = [hs.reshape(m, -1)]
            w_out = od_w_out[o]
        x2 = _outproj(acts, w_out.astype(BF16), x2, mod_l[:, 1], norm_post[layer, 1], seq, 1.0)
        x2 = _ffn(x2, mod_l[:, 2], norm_pre[layer, 2], norm_post[layer, 2], w13_b, w2_b, layer, 1, seq, 0.5)
    return x2.reshape(b, seq, d)
```

```python
import functools
import math

import jax
import jax.numpy as jnp
from jax import lax
from jax.experimental import pallas as pl
from jax.experimental.pallas import tpu as pltpu

F32 = jnp.float32
BF16 = jnp.bfloat16

EPS = 1e-6
GDN_HEAD_DIM = 128
RET_HEAD_DIM = 256
CHUNK = 64
CONV_WIDTH = 4
ROPE_BASE = 10000.0
LRU_C = 8.0
N_SUB = 3

SUBLANES = 8
LANES = 128
V7X_VMEM_LIMIT_BYTES = 56 * 1024 * 1024


def _cparams(*semantics):
    return pltpu.CompilerParams(dimension_semantics=semantics,
                                vmem_limit_bytes=V7X_VMEM_LIMIT_BYTES)


def _dot(a, b):
    return jnp.dot(a, b, preferred_element_type=F32)


def _dot_nt(a, b):
    return lax.dot_general(a, b, (((1,), (1,)), ((), ())), preferred_element_type=F32)


def _dot_tn(a, b):
    return lax.dot_general(a, b, (((0,), (0,)), ((), ())), preferred_element_type=F32)


def _sigmoid(x):
    return jax.nn.sigmoid(x)


def _silu(x):
    return x * jax.nn.sigmoid(x)


def _softplus(x):
    return jnp.maximum(x, 0.0) + jnp.log1p(jnp.exp(-jnp.abs(x)))


def _gelu_tanh(x):
    c = math.sqrt(2.0 / math.pi)
    return 0.5 * x * (1.0 + jnp.tanh(c * (x + 0.044715 * (x * x * x))))


def _rms(x, gain):
    return x * lax.rsqrt(jnp.mean(x * x, axis=-1, keepdims=True) + EPS) * gain


def _inv_rms(x):
    return lax.rsqrt(jnp.mean(x * x, axis=-1, keepdims=True) + EPS)


def _norm_mod(x, g_pre, mod_ref):
    return (x * _inv_rms(x)) * (g_pre * (1.0 + mod_ref[0, 1:2, :])) + mod_ref[0, 0:1, :]


def _post_residual(x, f, g_post, mod_ref, res_w):
    return x + (f * _inv_rms(f)) * (g_post * (res_w * (1.0 + mod_ref[0, 2:3, :])))


def _pick(n, pref):
    if n <= pref:
        return n
    for t in range(pref - pref % LANES, 0, -LANES):
        if n % t == 0:
            return t
    raise ValueError(f"no lane-aligned tile of {n} at or below {pref}")


def _ada_kernel(c_ref, w_ref, b_ref, o_ref):
    a = _silu(c_ref[...]).astype(BF16)
    o_ref[0] = _dot(a, w_ref[0].astype(BF16)) + b_ref[0]


def _ada(c, ada_w, ada_b):
    n_layers, d, n = ada_w.shape
    b = c.shape[0]
    tn = _pick(n, 1024)
    return pl.pallas_call(
        _ada_kernel,
        grid=(n_layers, n // tn),
        in_specs=[
            pl.BlockSpec((b, d), lambda l, j: (0, 0)),
            pl.BlockSpec((1, d, tn), lambda l, j: (l, 0, j)),
            pl.BlockSpec((1, 1, tn), lambda l, j: (l, 0, j)),
        ],
        out_specs=pl.BlockSpec((1, b, tn), lambda l, j: (l, 0, j)),
        out_shape=jax.ShapeDtypeStruct((n_layers, b, n), F32),
        compiler_params=_cparams("arbitrary", "arbitrary"),
        name="ada_mod",
    )(c, ada_w, ada_b.reshape(n_layers, 1, n))


FFN_HALVES = 2


def _ffn_kernel(x_ref, mod_ref, gpre_ref, gpost_ref, w1_ref, w3_ref, w2_ref, o_ref, h_scr,
                *, res_w, n_f):
    j = pl.program_id(1)
    tm = x_ref.shape[0]
    part = tm // FFN_HALVES

    def down(rs):
        h = h_scr[rs, :]
        a = (_silu(_dot(h, w1_ref[...])) * _dot(h, w3_ref[...])).astype(BF16)
        return _dot(a, w2_ref[...])

    @pl.when(j == 0)
    def _():
        for p in range(FFN_HALVES):
            rs = slice(p * part, (p + 1) * part)
            h_scr[rs, :] = _norm_mod(x_ref[rs, :], gpre_ref[...], mod_ref).astype(BF16)
            o_ref[rs, :] = down(rs)

    @pl.when((j > 0) & (j < n_f - 1))
    def _():
        for p in range(FFN_HALVES):
            rs = slice(p * part, (p + 1) * part)
            o_ref[rs, :] += down(rs)

    @pl.when(j == n_f - 1)
    def _():
        for p in range(FFN_HALVES):
            rs = slice(p * part, (p + 1) * part)
            f = o_ref[rs, :] + down(rs)
            o_ref[rs, :] = _post_residual(x_ref[rs, :], f, gpost_ref[...], mod_ref, res_w)


FFN_HIDDEN_TILE = 512


def _ffn_tile_w13(w13):
    n_l, two, d, f2 = w13.shape
    tf = _pick(f2 // 2, FFN_HIDDEN_TILE)
    return w13.reshape(n_l, two, d, f2 // tf, tf).transpose(0, 1, 3, 2, 4).astype(BF16)


def _ffn(x2, mod_s, g_pre, g_post, w13_all, w2_all, layer, which, seq, res_w):
    m, d = x2.shape
    f = w2_all.shape[2]
    tm = _pick(seq, 1024)
    tf = w13_all.shape[-1]
    n_f = f // tf
    per_b = seq // tm
    assert n_f >= 2 and w13_all.shape[2] == 2 * n_f and tm % (FFN_HALVES * 2 * SUBLANES) == 0
    return pl.pallas_call(
        functools.partial(_ffn_kernel, res_w=res_w, n_f=n_f),
        grid=(m // tm, n_f),
        in_specs=[
            pl.BlockSpec((tm, d), lambda i, j: (i, 0)),
            pl.BlockSpec((1, 3, d), lambda i, j: (i // per_b, 0, 0)),
            pl.BlockSpec((1, d), lambda i, j: (0, 0)),
            pl.BlockSpec((1, d), lambda i, j: (0, 0)),
            pl.BlockSpec((None, None, None, d, tf), lambda i, j: (layer, which, j, 0, 0)),
            pl.BlockSpec((None, None, None, d, tf), lambda i, j: (layer, which, j + n_f, 0, 0)),
            pl.BlockSpec((None, None, tf, d), lambda i, j: (layer, which, j, 0)),
        ],
        out_specs=pl.BlockSpec((tm, d), lambda i, j: (i, 0)),
        out_shape=jax.ShapeDtypeStruct((m, d), F32),
        scratch_shapes=[pltpu.VMEM((tm, d), BF16)],
        compiler_params=_cparams("arbitrary", "arbitrary"),
        name="ffn",
    )(x2, mod_s, g_pre.reshape(1, d), g_post.reshape(1, d), w13_all, w13_all, w2_all)


def _outproj_kernel(*refs, n_in, res_w):
    a_refs = refs[:n_in]
    w_refs = refs[n_in:2 * n_in]
    x_ref, mod_ref, gpost_ref, o_ref = refs[2 * n_in:]
    f = _dot(a_refs[0][...], w_refs[0][...])
    for a_ref, w_ref in zip(a_refs[1:], w_refs[1:]):
        f = f + _dot(a_ref[...], w_ref[...])
    o_ref[...] = _post_residual(x_ref[...], f, gpost_ref[...], mod_ref, res_w)


def _outproj(acts, w_out, x2, mod_s, g_post, seq, res_w):
    m, d = x2.shape
    tm = _pick(seq, 512)
    per_b = seq // tm
    n_in = len(acts)
    in_specs, args = [], []
    for a in acts:
        in_specs.append(pl.BlockSpec((tm, a.shape[1]), lambda i: (i, 0)))
        args.append(a)
    row = 0
    for a in acts:
        wi = a.shape[1]
        assert row % wi == 0
        in_specs.append(pl.BlockSpec((wi, d), lambda i, r=row // wi: (r, 0)))
        args.append(w_out)
        row += wi
    in_specs += [
        pl.BlockSpec((tm, d), lambda i: (i, 0)),
        pl.BlockSpec((1, 3, d), lambda i: (i // per_b, 0, 0)),
        pl.BlockSpec((1, d), lambda i: (0, 0)),
    ]
    args += [x2, mod_s, g_post.reshape(1, d)]
    return pl.pallas_call(
        functools.partial(_outproj_kernel, n_in=n_in, res_w=res_w),
        grid=(m // tm,),
        in_specs=in_specs,
        out_specs=pl.BlockSpec((tm, d), lambda i: (i, 0)),
        out_shape=jax.ShapeDtypeStruct((m, d), F32),
        compiler_params=_cparams("arbitrary"),
        name="mixer_outproj",
    )(*args)


def _conv_from_scratch(cs_ref, w_ref, rows, lo, hi):
    acc = cs_ref[SUBLANES:SUBLANES + rows, lo:hi] * w_ref[CONV_WIDTH - 1:CONV_WIDTH, lo:hi]
    for back in range(1, CONV_WIDTH):
        tap = CONV_WIDTH - 1 - back
        acc = acc + cs_ref[SUBLANES - back:SUBLANES - back + rows, lo:hi] * w_ref[tap:tap + 1, lo:hi]
    return acc


GDN_BLOCK = 256
GDN_PACK = 4


def _tile_rows(x, n):
    return jnp.concatenate([x] * n, axis=0)


def _gdn_block(p_ref, p_prev_ref, seq_start, between, cw_ref, alog_ref, dtb_ref, onorm_ref,
               o_ref, row0, s_scr, qn_scr, kn_scr, kb_scr, vb_scr, kbe_scr, kst_scr, egq_scr,
               a_scr, qk_scr, xo_scr, l_scr, ku_scr, oin_scr, *, n_heads, rows, n_pieces):
    dk = GDN_HEAD_DIM
    gw = n_heads * dk
    n_chunks = rows // CHUNK
    n_groups = n_heads // GDN_PACK
    pw = GDN_PACK * CHUNK
    gk = GDN_PACK * dk

    tail = p_prev_ref[rows:rows + SUBLANES, 0:3 * gw]
    p_ref[0:SUBLANES, 0:3 * gw] = jnp.where(seq_start, 0.0, tail)
    cs_scr = p_ref
    n_between = [0]

    def next_piece():
        between(n_between[0])
        n_between[0] += 1

    sm = p_ref[SUBLANES:SUBLANES + rows, 4 * gw:4 * gw + LANES]
    beta_all = _sigmoid(sm)
    g_all = -jnp.exp(alog_ref[...]) * _softplus(sm + dtb_ref[...])
    in_chunk = lax.broadcasted_iota(jnp.int32, (rows, LANES), 0) & (CHUNK - 1)
    d = 1
    while d < CHUNK:
        g_all = g_all + jnp.where(in_chunk >= d, pltpu.roll(g_all, d, 0), 0.0)
        d *= 2
    eg_all = jnp.exp(g_all)
    glast_all = jnp.concatenate(
        [jnp.broadcast_to(g_all[(c + 1) * CHUNK - 1:(c + 1) * CHUNK, :], (CHUNK, LANES))
         for c in range(n_chunks)], axis=0)
    ekl_all = jnp.exp(glast_all - g_all)

    for h in range(n_heads):
        next_piece()
        lo, hi = h * dk, (h + 1) * dk
        bcast = lambda arr, idx: jnp.broadcast_to(arr[:, idx:idx + 1], (rows, dk))
        beta_b = bcast(beta_all, h)
        eg_b = bcast(eg_all, n_heads + h)
        q = _silu(_conv_from_scratch(cs_scr, cw_ref, rows, lo, hi))
        qn = (q * lax.rsqrt(jnp.sum(q * q, axis=-1, keepdims=True) + EPS)) * dk ** -0.5
        qn_scr[:, lo:hi] = qn.astype(BF16)
        egq_scr[:, lo:hi] = qn * eg_b
        k = _silu(_conv_from_scratch(cs_scr, cw_ref, rows, gw + lo, gw + hi))
        kn = k * lax.rsqrt(jnp.sum(k * k, axis=-1, keepdims=True) + EPS)
        kb = kn * beta_b
        kn_scr[:, lo:hi] = kn.astype(BF16)
        kb_scr[:, lo:hi] = kb.astype(BF16)
        kbe_scr[:, lo:hi] = kb * eg_b
        kst_scr[:, lo:hi] = (kn * bcast(ekl_all, n_heads + h)).astype(BF16)
        v = _silu(_conv_from_scratch(cs_scr, cw_ref, rows, 2 * gw + lo, 2 * gw + hi))
        vb_scr[:, lo:hi] = v * beta_b

    ri = lax.broadcasted_iota(jnp.int32, (CHUNK, pw), 0)
    li = lax.broadcasted_iota(jnp.int32, (CHUNK, pw), 1)
    cj = li & (CHUNK - 1)
    lh = li // CHUNK
    causal = ri >= cj
    strict = ri > cj
    eye = ri == cj
    bd_r = lax.broadcasted_iota(jnp.int32, (pw, pw), 0) // CHUNK
    bd_c = lax.broadcasted_iota(jnp.int32, (pw, pw), 1) // CHUNK
    bd_sq = bd_r == bd_c
    bk_r = lax.broadcasted_iota(jnp.int32, (pw, gk), 0) // CHUNK
    bk_c = lax.broadcasted_iota(jnp.int32, (pw, gk), 1) // dk
    bd_k = bk_r == bk_c
    zero_sq = jnp.zeros((pw, pw), BF16)
    zero_k = jnp.zeros((pw, gk), BF16)

    def block_diag(x_rp):
        return jnp.where(bd_sq, _tile_rows(x_rp.astype(BF16), GDN_PACK), zero_sq)

    def block_diag_k(x_cat):
        return jnp.where(bd_k, _tile_rows(x_cat, GDN_PACK), zero_k)

    probs = [(c, gi) for c in range(n_chunks) for gi in range(n_groups)]
    for p, (c, gi) in enumerate(probs):
        next_piece()
        r0, r1 = c * CHUNK, (c + 1) * CHUNK
        c0 = gi * gk
        gcs = [jnp.broadcast_to(g_all[r0:r1, n_heads + gi * GDN_PACK + s:n_heads + gi * GDN_PACK + s + 1],
                                (CHUNK, pw)) for s in range(GDN_PACK)]
        gc = gcs[GDN_PACK - 1]
        for s in range(GDN_PACK - 2, -1, -1):
            gc = jnp.where(lh == s, gcs[s], gc)
        gr = jnp.sum(jnp.where(eye, gc, 0.0), axis=0, keepdims=True)
        decay = jnp.exp(jnp.where(causal, gc - gr, -jnp.inf))
        lhs = jnp.concatenate([kb_scr[r0:r1, c0:c0 + gk], qn_scr[r0:r1, c0:c0 + gk]], axis=0)
        kq = _dot_nt(lhs, block_diag_k(kn_scr[r0:r1, c0:c0 + gk]))
        a = jnp.where(strict, kq[0:CHUNK] * decay, 0.0)
        a_scr[p] = a
        qk_scr[p] = jnp.where(causal, kq[CHUNK:2 * CHUNK] * decay, 0.0)
        xo_scr[p] = -jnp.where((ri - cj == 1) & ((ri & 1) == 1), a, 0.0)

    next_piece()
    s = 2
    while s < CHUNK:
        rb = ri // s
        off = ((rb & 1) == 1) & ((cj // s) == rb - 1)
        for p in range(len(probs)):
            a_off = jnp.where(off, a_scr[p], 0.0)
            xo = xo_scr[p]
            y = a_off + _dot(xo.astype(BF16), block_diag(a_off))
            xo_scr[p] = xo - (y + _dot(y.astype(BF16), block_diag(xo)))
        s *= 2

    for p, (c, gi) in enumerate(probs):
        r0, r1 = c * CHUNK, (c + 1) * CHUNK
        rhs = jnp.concatenate(
            [jnp.concatenate([vb_scr[r0:r1, h * dk:(h + 1) * dk], kbe_scr[r0:r1, h * dk:(h + 1) * dk]], axis=1)
             for h in range(gi * GDN_PACK, (gi + 1) * GDN_PACK)], axis=0)
        sol = rhs + _dot(block_diag(xo_scr[p]), rhs.astype(BF16))
        sol_b = sol.astype(BF16)
        qks = _dot(block_diag(qk_scr[p]), sol_b)
        ks = _dot_tn(block_diag_k(kst_scr[r0:r1, gi * gk:(gi + 1) * gk]), sol_b)
        for s_ in range(GDN_PACK):
            h = gi * GDN_PACK + s_
            half = (h % 2) * dk
            rr = slice(s_ * CHUNK, (s_ + 1) * CHUNK)
            kr = slice(s_ * dk, (s_ + 1) * dk)
            q_eff = egq_scr[r0:r1, h * dk:(h + 1) * dk] - qks[rr, dk:2 * dk]
            l_scr[c, h // 2, 0:dk, half:half + dk] = ks[kr, dk:2 * dk].astype(BF16)
            l_scr[c, h // 2, dk:dk + CHUNK, half:half + dk] = q_eff.astype(BF16)
            ku_scr[c, h] = ks[kr, 0:dk]
            oin_scr[r0:r1, h * dk:(h + 1) * dk] = qks[rr, 0:dk]

    zero = jnp.zeros((dk, dk), BF16)
    for c in range(n_chunks):
        r0, r1 = c * CHUNK, (c + 1) * CHUNK
        for pr in range(n_heads // 2):
            h0, h1 = 2 * pr, 2 * pr + 1
            s0 = s_scr[h0]
            s1 = s_scr[h1]
            if c == 0:
                s0 = jnp.where(seq_start, 0.0, s0)
                s1 = jnp.where(seq_start, 0.0, s1)
            s_bd = jnp.concatenate(
                [jnp.concatenate([s0.astype(BF16), zero], axis=1),
                 jnp.concatenate([zero, s1.astype(BF16)], axis=1)], axis=0)
            r = _dot(l_scr[c, pr], s_bd)
            for h, st, off_ in ((h0, s0, 0), (h1, s1, dk)):
                cd = eg_all[r1 - 1:r1, n_heads + h:n_heads + h + 1]
                s_scr[h] = st * cd - r[0:dk, off_:off_ + dk] + ku_scr[c, h]
                o = r[dk:dk + CHUNK, off_:off_ + dk] + oin_scr[r0:r1, h * dk:(h + 1) * dk]
                z = p_ref[SUBLANES + r0:SUBLANES + r1, 3 * gw + h * dk:3 * gw + (h + 1) * dk]
                out = _rms(o, onorm_ref[...]) * _silu(z)
                o_ref[row0 + r0:row0 + r1, h * dk:(h + 1) * dk] = out.astype(o_ref.dtype)

    while n_between[0] < n_pieces:
        next_piece()


GDN_PIECES = 17


def _gdn_in_kernel(xa_ref, moda_ref, xb_ref, modb_ref, gpre_ref, win_ref,
                   cw_ref, alog_ref, dtb_ref, onorm_ref, o_ref, p0_scr, p1_scr, *scratch,
                   n_heads, rows, blocks_per_seq):
    s = pl.program_id(0)
    n_cols = win_ref.shape[1]
    piece = -(-n_cols // (GDN_PIECES * LANES)) * LANES
    bounds = [(c, min(c + piece, n_cols)) for c in range(0, n_cols, piece)]
    assert len(bounds) <= GDN_PIECES

    def in_proj_piece(h, p_ref, i):
        if i < len(bounds):
            c0, c1 = bounds[i]
            p_ref[SUBLANES:SUBLANES + rows, c0:c1] = _dot(h, win_ref[:, c0:c1])

    n_pieces = len(bounds)

    def normed(x_ref, mod_ref):
        return _norm_mod(x_ref[...], gpre_ref[...], mod_ref).astype(BF16)

    @pl.when(s == 0)
    def _():
        p0_scr[...] = jnp.zeros_like(p0_scr)
        p1_scr[...] = jnp.zeros_like(p1_scr)
        scratch[0][...] = jnp.zeros_like(scratch[0])

    seq_start = (2 * s - 2) % blocks_per_seq == 0
    ha = normed(xa_ref, moda_ref)
    _gdn_block(p0_scr, p1_scr, seq_start, lambda i: in_proj_piece(ha, p1_scr, i),
               cw_ref, alog_ref, dtb_ref, onorm_ref, o_ref, 0, *scratch, n_heads=n_heads, rows=rows,
               n_pieces=n_pieces)
    hb = normed(xb_ref, modb_ref)
    _gdn_block(p1_scr, p0_scr, False, lambda i: in_proj_piece(hb, p0_scr, i),
               cw_ref, alog_ref, dtb_ref, onorm_ref, o_ref, rows, *scratch, n_heads=n_heads, rows=rows,
               n_pieces=n_pieces)


def _gdn_in(x2, mod_s, g_pre, w_g, conv_w, a_log, dt_bias, o_norm, seq):
    m, d = x2.shape
    n_heads = a_log.shape[0]
    assert n_heads % GDN_PACK == 0 and GDN_HEAD_DIM == LANES
    gw = n_heads * GDN_HEAD_DIM
    n_cols = w_g.shape[1]
    assert n_cols == 4 * gw + LANES
    rows = _pick(seq, GDN_BLOCK)
    bps = seq // rows
    nb = m // rows
    assert bps % 2 == 0 and nb % 2 == 0
    last = nb - 1
    n_chunks = rows // CHUNK
    n_prob = n_chunks * (n_heads // GDN_PACK)
    pad = LANES - 2 * n_heads
    alog_row = jnp.pad(a_log, (n_heads, pad)).reshape(1, LANES)
    dtb_row = jnp.pad(dt_bias, (n_heads, pad)).reshape(1, LANES)
    xa = lambda s: jnp.maximum(2 * s - 1, 0)
    xb = lambda s: jnp.minimum(2 * s, last)
    return pl.pallas_call(
        functools.partial(_gdn_in_kernel, n_heads=n_heads, rows=rows, blocks_per_seq=bps),
        grid=(nb // 2 + 1,),
        in_specs=[
            pl.BlockSpec((rows, d), lambda s: (xa(s), 0)),
            pl.BlockSpec((1, 3, d), lambda s: (xa(s) // bps, 0, 0)),
            pl.BlockSpec((rows, d), lambda s: (xb(s), 0)),
            pl.BlockSpec((1, 3, d), lambda s: (xb(s) // bps, 0, 0)),
            pl.BlockSpec((1, d), lambda s: (0, 0)),
            pl.BlockSpec((d, n_cols), lambda s: (0, 0)),
            pl.BlockSpec((CONV_WIDTH, 3 * gw), lambda s: (0, 0)),
            pl.BlockSpec((1, LANES), lambda s: (0, 0)),
            pl.BlockSpec((1, LANES), lambda s: (0, 0)),
            pl.BlockSpec((1, GDN_HEAD_DIM), lambda s: (0, 0)),
        ],
        out_specs=pl.BlockSpec((2 * rows, gw), lambda s: (jnp.maximum(s - 1, 0), 0)),
        out_shape=jax.ShapeDtypeStruct((m, gw), BF16),
        scratch_shapes=[
            pltpu.VMEM((rows + SUBLANES, n_cols), F32),
            pltpu.VMEM((rows + SUBLANES, n_cols), F32),
            pltpu.VMEM((n_heads, GDN_HEAD_DIM, GDN_HEAD_DIM), F32),
            pltpu.VMEM((rows, gw), BF16),
            pltpu.VMEM((rows, gw), BF16),
            pltpu.VMEM((rows, gw), BF16),
            pltpu.VMEM((rows, gw), F32),
            pltpu.VMEM((rows, gw), F32),
            pltpu.VMEM((rows, gw), BF16),
            pltpu.VMEM((rows, gw), F32),
            pltpu.VMEM((n_prob, CHUNK, GDN_PACK * CHUNK), F32),
            pltpu.VMEM((n_prob, CHUNK, GDN_PACK * CHUNK), F32),
            pltpu.VMEM((n_prob, CHUNK, GDN_PACK * CHUNK), F32),
            pltpu.VMEM((n_chunks, n_heads // 2, GDN_HEAD_DIM + CHUNK, 2 * GDN_HEAD_DIM), BF16),
            pltpu.VMEM((n_chunks, n_heads, GDN_HEAD_DIM, GDN_HEAD_DIM), F32),
            pltpu.VMEM((rows, gw), F32),
        ],
        compiler_params=_cparams("arbitrary"),
        name="gdn_in",
    )(x2, mod_s, x2, mod_s, g_pre.reshape(1, d), w_g, conv_w, alog_row, dtb_row,
      o_norm.reshape(1, GDN_HEAD_DIM))


def _rotary(x, cos, sin):
    half = x.shape[-1] // 2
    x1, x2 = x[:, :half], x[:, half:]
    return jnp.concatenate([x1 * cos - x2 * sin, x2 * cos + x1 * sin], axis=-1)


RET_PIECES = 8


def _ret_block(p_ref, seq_start, between, cos, sin, norm_ref, o_ref, row0, s_scr, *, n_heads, rows,
               n_pieces):
    dk = RET_HEAD_DIM
    rw = n_heads * dk
    ri = lax.broadcasted_iota(jnp.int32, (rows, rows), 0)
    ci = lax.broadcasted_iota(jnp.int32, (rows, rows), 1)
    delta = (ri - ci).astype(F32)
    causal = ri >= ci
    pos = lax.broadcasted_iota(jnp.int32, (rows, 1), 0).astype(F32)
    done = 0

    for h in range(n_heads):
        while done < min(n_pieces, (h + 1) * -(-n_pieces // n_heads)):
            between(done)
            done += 1
        lo, hi = h * dk, (h + 1) * dk
        log_gamma = math.log1p(-(2.0 ** (-5.0 - h)))
        q = _rotary(p_ref[:, lo:hi], cos, sin)
        k = _rotary(p_ref[:, rw + lo:rw + hi], cos, sin) * dk ** -0.5
        v_b = p_ref[:, 2 * rw + lo:2 * rw + hi].astype(BF16)
        intra = jnp.exp(jnp.where(causal, delta * log_gamma, -jnp.inf))
        scores = _dot_nt(q.astype(BF16), k.astype(BF16)) * intra
        q_in = q * jnp.exp((pos + 1.0) * log_gamma)
        k_st = k * jnp.exp((rows - 1.0 - pos) * log_gamma)
        state = jnp.where(seq_start, 0.0, s_scr[h])
        o = _dot(scores.astype(BF16), v_b) + _dot(q_in.astype(BF16), state.astype(BF16))
        s_scr[h] = state * math.exp(rows * log_gamma) + _dot_tn(k_st.astype(BF16), v_b)
        out = _rms(o, norm_ref[:, lo:hi]) * _silu(p_ref[:, 3 * rw + lo:3 * rw + hi])
        o_ref[row0:row0 + rows, lo:hi] = out.astype(o_ref.dtype)

    while done < n_pieces:
        between(done)
        done += 1


def _ret_in_kernel(xa_ref, moda_ref, xb_ref, modb_ref, gpre_ref, win_ref, cos_ref, sin_ref, norm_ref,
                   o_ref, p0_scr, p1_scr, s_scr, *, n_heads, rows, blocks_per_seq):
    s = pl.program_id(0)
    piece = win_ref.shape[1] // RET_PIECES

    def in_proj_piece(h, p_ref, i):
        p_ref[:, i * piece:(i + 1) * piece] = _dot(h, win_ref[:, i * piece:(i + 1) * piece])

    def normed(x_ref, mod_ref):
        return _norm_mod(x_ref[...], gpre_ref[...], mod_ref).astype(BF16)

    @pl.when(s == 0)
    def _():
        p0_scr[...] = jnp.zeros_like(p0_scr)
        p1_scr[...] = jnp.zeros_like(p1_scr)
        s_scr[...] = jnp.zeros_like(s_scr)

    seq_start = (2 * s - 2) % blocks_per_seq == 0
    ha = normed(xa_ref, moda_ref)
    _ret_block(p0_scr, seq_start, lambda i: in_proj_piece(ha, p1_scr, i), cos_ref[0:rows, :],
               sin_ref[0:rows, :], norm_ref, o_ref, 0, s_scr, n_heads=n_heads, rows=rows,
               n_pieces=RET_PIECES)
    hb = normed(xb_ref, modb_ref)
    _ret_block(p1_scr, False, lambda i: in_proj_piece(hb, p0_scr, i), cos_ref[rows:2 * rows, :],
               sin_ref[rows:2 * rows, :], norm_ref, o_ref, rows, s_scr, n_heads=n_heads, rows=rows,
               n_pieces=RET_PIECES)


def _ret_in(x2, mod_s, g_pre, w_r, ret_norm, seq):
    m, d = x2.shape
    rw = ret_norm.shape[0]
    n_heads = rw // RET_HEAD_DIM
    n_cols = w_r.shape[1]
    assert n_cols == 4 * rw and n_cols % (RET_PIECES * LANES) == 0
    rows = _pick(seq, 256)
    bps = seq // rows
    nb = m // rows
    assert bps % 2 == 0 and nb % 2 == 0
    last = nb - 1
    half = RET_HEAD_DIM // 2
    inv_freq = ROPE_BASE ** (-jnp.arange(half, dtype=F32) / half)
    ang = jnp.arange(seq, dtype=F32)[:, None] * inv_freq[None, :]
    cos, sin = jnp.cos(ang), jnp.sin(ang)
    xa = lambda s: jnp.maximum(2 * s - 1, 0)
    xb = lambda s: jnp.minimum(2 * s, last)
    pos_blk = lambda s: (jnp.maximum(2 * s - 2, 0) % bps) // 2
    return pl.pallas_call(
        functools.partial(_ret_in_kernel, n_heads=n_heads, rows=rows, blocks_per_seq=bps),
        grid=(nb // 2 + 1,),
        in_specs=[
            pl.BlockSpec((rows, d), lambda s: (xa(s), 0)),
            pl.BlockSpec((1, 3, d), lambda s: (xa(s) // bps, 0, 0)),
            pl.BlockSpec((rows, d), lambda s: (xb(s), 0)),
            pl.BlockSpec((1, 3, d), lambda s: (xb(s) // bps, 0, 0)),
            pl.BlockSpec((1, d), lambda s: (0, 0)),
            pl.BlockSpec((d, n_cols), lambda s: (0, 0)),
            pl.BlockSpec((2 * rows, half), lambda s: (pos_blk(s), 0)),
            pl.BlockSpec((2 * rows, half), lambda s: (pos_blk(s), 0)),
            pl.BlockSpec((1, rw), lambda s: (0, 0)),
        ],
        out_specs=pl.BlockSpec((2 * rows, rw), lambda s: (jnp.maximum(s - 1, 0), 0)),
        out_shape=jax.ShapeDtypeStruct((m, rw), BF16),
        scratch_shapes=[
            pltpu.VMEM((rows, n_cols), F32),
            pltpu.VMEM((rows, n_cols), F32),
            pltpu.VMEM((n_heads, RET_HEAD_DIM, RET_HEAD_DIM), F32),
        ],
        compiler_params=_cparams("arbitrary"),
        name="retention_in",
    )(x2, mod_s, x2, mod_s, g_pre.reshape(1, d), w_r, cos, sin, ret_norm.reshape(1, rw))


LRU_ROWS = 256


def _lru_block(p_ref, p_prev_ref, seq_start, between, perm_ref, cw_ref, cb_ref, wa_ref, ba_ref, wx_ref,
               bx_ref, lam_ref, o_ref, row0, h_scr, act_scr, *, n_blocks, rows, bw):
    width = n_blocks * bw
    nj = rows // SUBLANES
    sub = lax.broadcasted_iota(jnp.int32, (SUBLANES, bw), 0)
    tails = [jnp.where(seq_start, 0.0, p_prev_ref[rows - SUBLANES * (k - 1) - 1:rows - SUBLANES * (k - 1),
                                                  width:2 * width]) for k in range(1, CONV_WIDTH)]

    for n in range(n_blocks):
        between(2 * n)
        lo, hi = n * bw, (n + 1) * bw
        xp = p_ref[:, width + lo:width + hi]

        def edge(k):
            prev = pltpu.roll(xp[rows - k * SUBLANES:rows - (k - 1) * SUBLANES], 1, 0)
            return jnp.where(sub == 0, tails[k - 1][:, lo:hi], prev)

        edges = [edge(k) for k in range(1, CONV_WIDTH)]
        xc = xp * cw_ref[CONV_WIDTH - 1:CONV_WIDTH, lo:hi]
        for back in range(1, CONV_WIDTH):
            tap = CONV_WIDTH - 1 - back
            shifted = jnp.concatenate(edges[:back][::-1] + [xp[:rows - back * SUBLANES]], axis=0)
            xc = xc + shifted * cw_ref[tap:tap + 1, lo:hi]
        xc = xc + cb_ref[:, lo:hi]
        xc_b = xc.astype(BF16)
        r = _sigmoid(_dot(xc_b, wa_ref[n]) + ba_ref[:, lo:hi])
        i = _sigmoid(_dot(xc_b, wx_ref[n]) + bx_ref[:, lo:hi])
        log_a = (-LRU_C * r) * _softplus(-lam_ref[:, lo:hi])
        a = jnp.exp(log_a)
        mult = jnp.sqrt(-jnp.tanh(log_a) * (a * a + 1.0))
        bv = mult * (i * xc)

        h = bv[0:SUBLANES]
        ac = a[0:SUBLANES]
        hl, al = [h], [ac]
        for j in range(1, nj):
            aj = a[j * SUBLANES:(j + 1) * SUBLANES]
            h = aj * h + bv[j * SUBLANES:(j + 1) * SUBLANES]
            ac = aj * ac
            hl.append(h)
            al.append(ac)
        between(2 * n + 1)
        c = jnp.where(seq_start, 0.0, h_scr[0:1, lo:hi])
        cs = []
        for s in range(SUBLANES):
            cs.append(c)
            c = h[s:s + 1] + ac[s:s + 1] * c
        h_scr[0:1, lo:hi] = c
        c_in = jnp.concatenate(cs, axis=0)
        hs = jnp.concatenate([hl[j] + al[j] * c_in for j in range(nj)], axis=0)
        act_scr[:, lo:hi] = (hs * _gelu_tanh(p_ref[:, lo:hi])).astype(BF16)

    o_ref[row0:row0 + rows, :] = _dot_tn(perm_ref[...], act_scr[...]).astype(o_ref.dtype)


def _lru_in_kernel(x0_ref, mod0_ref, xa_ref, moda_ref, xb_ref, modb_ref, gpre_ref, win_ref, perm_ref,
                   cw_ref, cb_ref, wa_ref, ba_ref, wx_ref, bx_ref, lam_ref, o_ref,
                   p0_scr, p1_scr, h_scr, act_scr, *, n_blocks, rows, bw, blocks_per_seq):
    s = pl.program_id(0)

    def in_proj_cols(h, p_ref, n):
        cw = win_ref.shape[1] // (2 * n_blocks)
        p_ref[:, n * cw:(n + 1) * cw] = _dot(h, win_ref[:, n * cw:(n + 1) * cw])

    def normed(x_ref, mod_ref):
        h = _norm_mod(x_ref[...], gpre_ref[...], mod_ref).astype(BF16)
        return _dot(perm_ref[...], h).astype(BF16)

    @pl.when(s == 0)
    def _():
        h0 = normed(x0_ref, mod0_ref)
        for n in range(2 * n_blocks):
            in_proj_cols(h0, p0_scr, n)
        p1_scr[...] = jnp.zeros_like(p1_scr)
        h_scr[...] = jnp.zeros_like(h_scr)

    lru = functools.partial(_lru_block, perm_ref=perm_ref, cw_ref=cw_ref, cb_ref=cb_ref, wa_ref=wa_ref,
                            ba_ref=ba_ref, wx_ref=wx_ref, bx_ref=bx_ref, lam_ref=lam_ref, o_ref=o_ref,
                            h_scr=h_scr, act_scr=act_scr, n_blocks=n_blocks, rows=rows, bw=bw)
    seq_start = (2 * s) % blocks_per_seq == 0
    ha = normed(xa_ref, moda_ref)
    lru(p0_scr, p1_scr, seq_start, lambda n: in_proj_cols(ha, p1_scr, n), row0=0)
    hb = normed(xb_ref, modb_ref)
    lru(p1_scr, p0_scr, False, lambda n: in_proj_cols(hb, p0_scr, n), row0=rows)


def _lru_in(x2, mod_s, g_pre, w_in, conv_w, conv_b, gate_a_w, gate_a_b, gate_x_w, gate_x_b, lam, seq):
    m, d = x2.shape
    width = w_in.shape[1] // 2
    n_blocks, bw, _ = gate_a_w.shape
    rows = _pick(seq, LRU_ROWS)
    bps = seq // rows
    nb = m // rows
    assert bps % 2 == 0 and nb % 2 == 0
    last = nb - 1
    r_idx = jnp.arange(rows)
    t_of_r = (r_idx % SUBLANES) * (rows // SUBLANES) + r_idx // SUBLANES
    perm = (t_of_r[:, None] == r_idx[None, :]).astype(BF16)
    xa = lambda s: 2 * s + 1
    xb = lambda s: jnp.minimum(2 * s + 2, last)
    vec = lambda: pl.BlockSpec((1, width), lambda s: (0, 0))
    gate = lambda: pl.BlockSpec((n_blocks, bw, bw), lambda s: (0, 0, 0))
    return pl.pallas_call(
        functools.partial(_lru_in_kernel, n_blocks=n_blocks, rows=rows, bw=bw, blocks_per_seq=bps),
        grid=(nb // 2,),
        in_specs=[
            pl.BlockSpec((rows, d), lambda s: (0, 0)),
            pl.BlockSpec((1, 3, d), lambda s: (0, 0, 0)),
            pl.BlockSpec((rows, d), lambda s: (xa(s), 0)),
            pl.BlockSpec((1, 3, d), lambda s: (xa(s) // bps, 0, 0)),
            pl.BlockSpec((rows, d), lambda s: (xb(s), 0)),
            pl.BlockSpec((1, 3, d), lambda s: (xb(s) // bps, 0, 0)),
            pl.BlockSpec((1, d), lambda s: (0, 0)),
            pl.BlockSpec((d, 2 * width), lambda s: (0, 0)),
            pl.BlockSpec((rows, rows), lambda s: (0, 0)),
            pl.BlockSpec((CONV_WIDTH, width), lambda s: (0, 0)),
            vec(), gate(), vec(), gate(), vec(), vec(),
        ],
        out_specs=pl.BlockSpec((2 * rows, width), lambda s: (s, 0)),
        out_shape=jax.ShapeDtypeStruct((m, width), BF16),
        scratch_shapes=[
            pltpu.VMEM((rows, 2 * width), F32),
            pltpu.VMEM((rows, 2 * width), F32),
            pltpu.VMEM((SUBLANES, width), F32),
            pltpu.VMEM((rows, width), BF16),
        ],
        compiler_params=_cparams("arbitrary"),
        name="rglru_in",
    )(x2, mod_s, x2, mod_s, x2, mod_s, g_pre.reshape(1, d), w_in, perm, conv_w, conv_b.reshape(1, width),
      gate_a_w.astype(BF16), gate_a_b.reshape(1, width), gate_x_w.astype(BF16),
      gate_x_b.reshape(1, width), lam.reshape(1, width))


def kernel(x, c, ada_w, ada_b, norm_pre, norm_post, ffn_w13, ffn_w2, ev_w_in, ev_conv_w, ev_a_log,
           ev_dt_bias, ev_o_norm, ev_ret_norm, ev_w_out, od_w_in, od_conv_w, od_conv_b,
           od_gate_a_w, od_gate_a_b, od_gate_x_w, od_gate_x_b, od_lambda, od_w_out):
    b, seq, d = x.shape
    depth = ada_w.shape[0]
    m = b * seq
    mod = _ada(c, ada_w, ada_b).reshape(depth, b, N_SUB, 3, d)
    x2 = x.reshape(m, d)
    w13_b = _ffn_tile_w13(ffn_w13)
    w2_b = ffn_w2.astype(BF16)

    for layer in range(depth):
        mod_l = mod[layer]
        x2 = _ffn(x2, mod_l[:, 0], norm_pre[layer, 0], norm_post[layer, 0], w13_b, w2_b, layer, 0, seq, 0.5)
        if layer % 2 == 0:
            e = layer // 2
            n_heads = ev_a_log.shape[1]
            gw = n_heads * GDN_HEAD_DIM
            w_in = ev_w_in[e]
            w_small = jnp.pad(w_in[:, 3 * gw:3 * gw + 2 * n_heads], ((0, 0), (0, LANES - 2 * n_heads)))
            w_g = jnp.concatenate([w_in[:, :3 * gw], w_in[:, 3 * gw + 2 * n_heads:4 * gw + 2 * n_heads],
                                   w_small], axis=1).astype(BF16)
            w_r = w_in[:, 4 * gw + 2 * n_heads:].astype(BF16)
            o_a = _gdn_in(x2, mod_l[:, 1], norm_pre[layer, 1], w_g, ev_conv_w[e], ev_a_log[e],
                          ev_dt_bias[e], ev_o_norm[e], seq)
            o_b = _ret_in(x2, mod_l[:, 1], norm_pre[layer, 1], w_r, ev_ret_norm[e], seq)
            acts = [o_a, o_b]
            w_out = ev_w_out[e]
        else:
            o = layer // 2
            hs = _lru_in(x2, mod_l[:, 1], norm_pre[layer, 1], od_w_in[o].astype(BF16), od_conv_w[o],
                         od_conv_b[o], od_gate_a_w[o], od_gate_a_b[o], od_gate_x_w[o], od_gate_x_b[o],
                         od_lambda[o], seq)
            acts = [hs.reshape(m, -1)]
            w_out = od_w_out[o]
        x2 = _outproj(acts, w_out.astype(BF16), x2, mod_l[:, 1], norm_post[layer, 1], seq, 1.0)
        x2 = _ffn(x2, mod_l[:, 2], norm_pre[layer, 2], norm_post[layer, 2], w13_b, w2_b, layer, 1, seq, 0.5)
    return x2.reshape(b, seq, d)
```

```python
import functools
import math

import jax
import jax.numpy as jnp
from jax import lax
from jax.experimental import pallas as pl
from jax.experimental.pallas import tpu as pltpu

F32 = jnp.float32
BF16 = jnp.bfloat16

EPS = 1e-6
GDN_HEAD_DIM = 128
RET_HEAD_DIM = 256
CHUNK = 64
CONV_WIDTH = 4
ROPE_BASE = 10000.0
LRU_C = 8.0
N_SUB = 3

SUBLANES = 8
LANES = 128
V7X_VMEM_LIMIT_BYTES = 56 * 1024 * 1024


def _cparams(*semantics):
    return pltpu.CompilerParams(dimension_semantics=semantics,
                                vmem_limit_bytes=V7X_VMEM_LIMIT_BYTES)


def _dot(a, b):
    return jnp.dot(a, b, preferred_element_type=F32)


def _dot_nt(a, b):
    return lax.dot_general(a, b, (((1,), (1,)), ((), ())), preferred_element_type=F32)


def _dot_tn(a, b):
    return lax.dot_general(a, b, (((0,), (0,)), ((), ())), preferred_element_type=F32)


def _sigmoid(x):
    return jax.nn.sigmoid(x)


def _silu(x):
    return x * jax.nn.sigmoid(x)


def _softplus(x):
    return jnp.maximum(x, 0.0) + jnp.log1p(jnp.exp(-jnp.abs(x)))


def _gelu_tanh(x):
    c = math.sqrt(2.0 / math.pi)
    return 0.5 * x * (1.0 + jnp.tanh(c * (x + 0.044715 * (x * x * x))))


def _rms(x, gain):
    return x * lax.rsqrt(jnp.mean(x * x, axis=-1, keepdims=True) + EPS) * gain


def _inv_rms(x):
    return lax.rsqrt(jnp.mean(x * x, axis=-1, keepdims=True) + EPS)


def _norm_mod(x, g_pre, mod_ref):
    return (x * _inv_rms(x)) * (g_pre * (1.0 + mod_ref[0, 1:2, :])) + mod_ref[0, 0:1, :]


def _post_residual(x, f, g_post, mod_ref, res_w):
    return x + (f * _inv_rms(f)) * (g_post * (res_w * (1.0 + mod_ref[0, 2:3, :])))


def _pick(n, pref):
    if n <= pref:
        return n
    for t in range(pref - pref % LANES, 0, -LANES):
        if n % t == 0:
            return t
    raise ValueError(f"no lane-aligned tile of {n} at or below {pref}")


def _ada_kernel(c_ref, w_ref, b_ref, o_ref):
    a = _silu(c_ref[...]).astype(BF16)
    o_ref[0] = _dot(a, w_ref[0].astype(BF16)) + b_ref[0]


def _ada(c, ada_w, ada_b):
    n_layers, d, n = ada_w.shape
    b = c.shape[0]
    tn = _pick(n, 2048)
    return pl.pallas_call(
        _ada_kernel,
        grid=(n_layers, n // tn),
        in_specs=[
            pl.BlockSpec((b, d), lambda l, j: (0, 0)),
            pl.BlockSpec((1, d, tn), lambda l, j: (l, 0, j)),
            pl.BlockSpec((1, 1, tn), lambda l, j: (l, 0, j)),
        ],
        out_specs=pl.BlockSpec((1, b, tn), lambda l, j: (l, 0, j)),
        out_shape=jax.ShapeDtypeStruct((n_layers, b, n), F32),
        compiler_params=_cparams("arbitrary", "arbitrary"),
        name="ada_mod",
    )(c, ada_w, ada_b.reshape(n_layers, 1, n))


FFN_HALVES = 2


def _ffn_kernel(x_ref, mod_ref, gpre_ref, gpost_ref, w1_ref, w3_ref, w2_ref, o_ref, h_scr,
                *, res_w, n_f):
    j = pl.program_id(1)
    tm = x_ref.shape[0]
    part = tm // FFN_HALVES

    def down(rs):
        h = h_scr[rs, :]
        a = (_silu(_dot(h, w1_ref[...])) * _dot(h, w3_ref[...])).astype(BF16)
        return _dot(a, w2_ref[...])

    @pl.when(j == 0)
    def _():
        for p in range(FFN_HALVES):
            rs = slice(p * part, (p + 1) * part)
            h_scr[rs, :] = _norm_mod(x_ref[rs, :], gpre_ref[...], mod_ref).astype(BF16)
            o_ref[rs, :] = down(rs)

    @pl.when((j > 0) & (j < n_f - 1))
    def _():
        for p in range(FFN_HALVES):
            rs = slice(p * part, (p + 1) * part)
            o_ref[rs, :] += down(rs)

    @pl.when(j == n_f - 1)
    def _():
        for p in range(FFN_HALVES):
            rs = slice(p * part, (p + 1) * part)
            f = o_ref[rs, :] + down(rs)
            o_ref[rs, :] = _post_residual(x_ref[rs, :], f, gpost_ref[...], mod_ref, res_w)


def _ffn(x2, mod_s, g_pre, g_post, w13_all, w2_all, layer, which, seq, res_w):
    m, d = x2.shape
    f = w2_all.shape[2]
    tm = _pick(seq, 1024)
    tf = _pick(f, 512)
    n_f = f // tf
    per_b = seq // tm
    assert n_f >= 2 and tm % (FFN_HALVES * 2 * SUBLANES) == 0
    return pl.pallas_call(
        functools.partial(_ffn_kernel, res_w=res_w, n_f=n_f),
        grid=(m // tm, n_f),
        in_specs=[
            pl.BlockSpec((tm, d), lambda i, j: (i, 0)),
            pl.BlockSpec((1, 3, d), lambda i, j: (i // per_b, 0, 0)),
            pl.BlockSpec((1, d), lambda i, j: (0, 0)),
            pl.BlockSpec((1, d), lambda i, j: (0, 0)),
            pl.BlockSpec((None, None, d, tf), lambda i, j: (layer, which, 0, j)),
            pl.BlockSpec((None, None, d, tf), lambda i, j: (layer, which, 0, j + n_f)),
            pl.BlockSpec((None, None, tf, d), lambda i, j: (layer, which, j, 0)),
        ],
        out_specs=pl.BlockSpec((tm, d), lambda i, j: (i, 0)),
        out_shape=jax.ShapeDtypeStruct((m, d), F32),
        scratch_shapes=[pltpu.VMEM((tm, d), BF16)],
        compiler_params=_cparams("arbitrary", "arbitrary"),
        name="ffn",
    )(x2, mod_s, g_pre.reshape(1, d), g_post.reshape(1, d), w13_all, w13_all, w2_all)


def _outproj_kernel(*refs, n_in, res_w):
    a_refs = refs[:n_in]
    w_refs = refs[n_in:2 * n_in]
    x_ref, mod_ref, gpost_ref, o_ref = refs[2 * n_in:]
    f = _dot(a_refs[0][...], w_refs[0][...])
    for a_ref, w_ref in zip(a_refs[1:], w_refs[1:]):
        f = f + _dot(a_ref[...], w_ref[...])
    o_ref[...] = _post_residual(x_ref[...], f, gpost_ref[...], mod_ref, res_w)


def _outproj(acts, w_out, x2, mod_s, g_post, seq, res_w):
    m, d = x2.shape
    tm = _pick(seq, 512)
    per_b = seq // tm
    n_in = len(acts)
    in_specs, args = [], []
    for a in acts:
        in_specs.append(pl.BlockSpec((tm, a.shape[1]), lambda i: (i, 0)))
        args.append(a)
    row = 0
    for a in acts:
        wi = a.shape[1]
        assert row % wi == 0
        in_specs.append(pl.BlockSpec((wi, d), lambda i, r=row // wi: (r, 0)))
        args.append(w_out)
        row += wi
    in_specs += [
        pl.BlockSpec((tm, d), lambda i: (i, 0)),
        pl.BlockSpec((1, 3, d), lambda i: (i // per_b, 0, 0)),
        pl.BlockSpec((1, d), lambda i: (0, 0)),
    ]
    args += [x2, mod_s, g_post.reshape(1, d)]
    return pl.pallas_call(
        functools.partial(_outproj_kernel, n_in=n_in, res_w=res_w),
        grid=(m // tm,),
        in_specs=in_specs,
        out_specs=pl.BlockSpec((tm, d), lambda i: (i, 0)),
        out_shape=jax.ShapeDtypeStruct((m, d), F32),
        compiler_params=_cparams("arbitrary"),
        name="mixer_outproj",
    )(*args)


def _conv_from_scratch(cs_ref, w_ref, rows, lo, hi):
    acc = cs_ref[SUBLANES:SUBLANES + rows, lo:hi] * w_ref[CONV_WIDTH - 1:CONV_WIDTH, lo:hi]
    for back in range(1, CONV_WIDTH):
        tap = CONV_WIDTH - 1 - back
        acc = acc + cs_ref[SUBLANES - back:SUBLANES - back + rows, lo:hi] * w_ref[tap:tap + 1, lo:hi]
    return acc


GDN_BLOCK = 256
GDN_PACK = 4


def _tile_rows(x, n):
    return jnp.concatenate([x] * n, axis=0)


def _gdn_block(p_ref, p_prev_ref, seq_start, between, cw_ref, alog_ref, dtb_ref, onorm_ref,
               o_ref, row0, s_scr, qn_scr, kn_scr, kb_scr, vb_scr, kbe_scr, kst_scr, egq_scr,
               a_scr, qk_scr, xo_scr, l_scr, ku_scr, oin_scr, *, n_heads, rows, n_pieces):
    dk = GDN_HEAD_DIM
    gw = n_heads * dk
    n_chunks = rows // CHUNK
    n_groups = n_heads // GDN_PACK
    pw = GDN_PACK * CHUNK
    gk = GDN_PACK * dk

    tail = p_prev_ref[rows:rows + SUBLANES, 0:3 * gw]
    p_ref[0:SUBLANES, 0:3 * gw] = jnp.where(seq_start, 0.0, tail)
    cs_scr = p_ref
    n_between = [0]

    def next_piece():
        between(n_between[0])
        n_between[0] += 1

    sm = p_ref[SUBLANES:SUBLANES + rows, 4 * gw:4 * gw + LANES]
    beta_all = _sigmoid(sm)
    g_all = -jnp.exp(alog_ref[...]) * _softplus(sm + dtb_ref[...])
    in_chunk = lax.broadcasted_iota(jnp.int32, (rows, LANES), 0) & (CHUNK - 1)
    d = 1
    while d < CHUNK:
        g_all = g_all + jnp.where(in_chunk >= d, pltpu.roll(g_all, d, 0), 0.0)
        d *= 2
    eg_all = jnp.exp(g_all)
    glast_all = jnp.concatenate(
        [jnp.broadcast_to(g_all[(c + 1) * CHUNK - 1:(c + 1) * CHUNK, :], (CHUNK, LANES))
         for c in range(n_chunks)], axis=0)
    ekl_all = jnp.exp(glast_all - g_all)

    for h in range(n_heads):
        next_piece()
        lo, hi = h * dk, (h + 1) * dk
        bcast = lambda arr, idx: jnp.broadcast_to(arr[:, idx:idx + 1], (rows, dk))
        beta_b = bcast(beta_all, h)
        eg_b = bcast(eg_all, n_heads + h)
        q = _silu(_conv_from_scratch(cs_scr, cw_ref, rows, lo, hi))
        qn = (q * lax.rsqrt(jnp.sum(q * q, axis=-1, keepdims=True) + EPS)) * dk ** -0.5
        qn_scr[:, lo:hi] = qn.astype(BF16)
        egq_scr[:, lo:hi] = qn * eg_b
        k = _silu(_conv_from_scratch(cs_scr, cw_ref, rows, gw + lo, gw + hi))
        kn = k * lax.rsqrt(jnp.sum(k * k, axis=-1, keepdims=True) + EPS)
        kb = kn * beta_b
        kn_scr[:, lo:hi] = kn.astype(BF16)
        kb_scr[:, lo:hi] = kb.astype(BF16)
        kbe_scr[:, lo:hi] = kb * eg_b
        kst_scr[:, lo:hi] = (kn * bcast(ekl_all, n_heads + h)).astype(BF16)
        v = _silu(_conv_from_scratch(cs_scr, cw_ref, rows, 2 * gw + lo, 2 * gw + hi))
        vb_scr[:, lo:hi] = v * beta_b

    ri = lax.broadcasted_iota(jnp.int32, (CHUNK, pw), 0)
    li = lax.broadcasted_iota(jnp.int32, (CHUNK, pw), 1)
    cj = li & (CHUNK - 1)
    lh = li // CHUNK
    causal = ri >= cj
    strict = ri > cj
    eye = ri == cj
    bd_r = lax.broadcasted_iota(jnp.int32, (pw, pw), 0) // CHUNK
    bd_c = lax.broadcasted_iota(jnp.int32, (pw, pw), 1) // CHUNK
    bd_sq = bd_r == bd_c
    bk_r = lax.broadcasted_iota(jnp.int32, (pw, gk), 0) // CHUNK
    bk_c = lax.broadcasted_iota(jnp.int32, (pw, gk), 1) // dk
    bd_k = bk_r == bk_c
    zero_sq = jnp.zeros((pw, pw), BF16)
    zero_k = jnp.zeros((pw, gk), BF16)

    def block_diag(x_rp):
        return jnp.where(bd_sq, _tile_rows(x_rp.astype(BF16), GDN_PACK), zero_sq)

    def block_diag_k(x_cat):
        return jnp.where(bd_k, _tile_rows(x_cat, GDN_PACK), zero_k)

    probs = [(c, gi) for c in range(n_chunks) for gi in range(n_groups)]
    for p, (c, gi) in enumerate(probs):
        next_piece()
        r0, r1 = c * CHUNK, (c + 1) * CHUNK
        c0 = gi * gk
        gcs = [jnp.broadcast_to(g_all[r0:r1, n_heads + gi * GDN_PACK + s:n_heads + gi * GDN_PACK + s + 1],
                                (CHUNK, pw)) for s in range(GDN_PACK)]
        gc = gcs[GDN_PACK - 1]
        for s in range(GDN_PACK - 2, -1, -1):
            gc = jnp.where(lh == s, gcs[s], gc)
        gr = jnp.sum(jnp.where(eye, gc, 0.0), axis=0, keepdims=True)
        decay = jnp.exp(jnp.where(causal, gc - gr, -jnp.inf))
        lhs = jnp.concatenate([kb_scr[r0:r1, c0:c0 + gk], qn_scr[r0:r1, c0:c0 + gk]], axis=0)
        kq = _dot_nt(lhs, block_diag_k(kn_scr[r0:r1, c0:c0 + gk]))
        a = jnp.where(strict, kq[0:CHUNK] * decay, 0.0)
        a_scr[p] = a
        qk_scr[p] = jnp.where(causal, kq[CHUNK:2 * CHUNK] * decay, 0.0)
        xo_scr[p] = -jnp.where((ri - cj == 1) & ((ri & 1) == 1), a, 0.0)

    next_piece()
    s = 2
    while s < CHUNK:
        rb = ri // s
        off = ((rb & 1) == 1) & ((cj // s) == rb - 1)
        for p in range(len(probs)):
            a_off = jnp.where(off, a_scr[p], 0.0)
            xo = xo_scr[p]
            y = a_off + _dot(xo.astype(BF16), block_diag(a_off))
            xo_scr[p] = xo - (y + _dot(y.astype(BF16), block_diag(xo)))
        s *= 2

    for p, (c, gi) in enumerate(probs):
        r0, r1 = c * CHUNK, (c + 1) * CHUNK
        rhs = jnp.concatenate(
            [jnp.concatenate([vb_scr[r0:r1, h * dk:(h + 1) * dk], kbe_scr[r0:r1, h * dk:(h + 1) * dk]], axis=1)
             for h in range(gi * GDN_PACK, (gi + 1) * GDN_PACK)], axis=0)
        sol = rhs + _dot(block_diag(xo_scr[p]), rhs.astype(BF16))
        sol_b = sol.astype(BF16)
        qks = _dot(block_diag(qk_scr[p]), sol_b)
        ks = _dot_tn(block_diag_k(kst_scr[r0:r1, gi * gk:(gi + 1) * gk]), sol_b)
        for s_ in range(GDN_PACK):
            h = gi * GDN_PACK + s_
            half = (h % 2) * dk
            rr = slice(s_ * CHUNK, (s_ + 1) * CHUNK)
            kr = slice(s_ * dk, (s_ + 1) * dk)
            q_eff = egq_scr[r0:r1, h * dk:(h + 1) * dk] - qks[rr, dk:2 * dk]
            l_scr[c, h // 2, 0:dk, half:half + dk] = ks[kr, dk:2 * dk].astype(BF16)
            l_scr[c, h // 2, dk:dk + CHUNK, half:half + dk] = q_eff.astype(BF16)
            ku_scr[c, h] = ks[kr, 0:dk]
            oin_scr[r0:r1, h * dk:(h + 1) * dk] = qks[rr, 0:dk]

    zero = jnp.zeros((dk, dk), BF16)
    for c in range(n_chunks):
        r0, r1 = c * CHUNK, (c + 1) * CHUNK
        for pr in range(n_heads // 2):
            h0, h1 = 2 * pr, 2 * pr + 1
            s0 = s_scr[h0]
            s1 = s_scr[h1]
            if c == 0:
                s0 = jnp.where(seq_start, 0.0, s0)
                s1 = jnp.where(seq_start, 0.0, s1)
            s_bd = jnp.concatenate(
                [jnp.concatenate([s0.astype(BF16), zero], axis=1),
                 jnp.concatenate([zero, s1.astype(BF16)], axis=1)], axis=0)
            r = _dot(l_scr[c, pr], s_bd)
            for h, st, off_ in ((h0, s0, 0), (h1, s1, dk)):
                cd = eg_all[r1 - 1:r1, n_heads + h:n_heads + h + 1]
                s_scr[h] = st * cd - r[0:dk, off_:off_ + dk] + ku_scr[c, h]
                o = r[dk:dk + CHUNK, off_:off_ + dk] + oin_scr[r0:r1, h * dk:(h + 1) * dk]
                z = p_ref[SUBLANES + r0:SUBLANES + r1, 3 * gw + h * dk:3 * gw + (h + 1) * dk]
                out = _rms(o, onorm_ref[...]) * _silu(z)
                o_ref[row0 + r0:row0 + r1, h * dk:(h + 1) * dk] = out.astype(o_ref.dtype)

    while n_between[0] < n_pieces:
        next_piece()


GDN_PIECES = 17


def _gdn_in_kernel(xa_ref, moda_ref, xb_ref, modb_ref, gpre_ref, win_ref,
                   cw_ref, alog_ref, dtb_ref, onorm_ref, o_ref, p0_scr, p1_scr, *scratch,
                   n_heads, rows, blocks_per_seq):
    s = pl.program_id(0)
    n_cols = win_ref.shape[1]
    piece = -(-n_cols // (GDN_PIECES * LANES)) * LANES
    bounds = [(c, min(c + piece, n_cols)) for c in range(0, n_cols, piece)]
    assert len(bounds) <= GDN_PIECES

    def in_proj_piece(h, p_ref, i):
        if i < len(bounds):
            c0, c1 = bounds[i]
            p_ref[SUBLANES:SUBLANES + rows, c0:c1] = _dot(h, win_ref[:, c0:c1])

    n_pieces = len(bounds)

    def normed(x_ref, mod_ref):
        return _norm_mod(x_ref[...], gpre_ref[...], mod_ref).astype(BF16)

    @pl.when(s == 0)
    def _():
        p0_scr[...] = jnp.zeros_like(p0_scr)
        p1_scr[...] = jnp.zeros_like(p1_scr)
        scratch[0][...] = jnp.zeros_like(scratch[0])

    seq_start = (2 * s - 2) % blocks_per_seq == 0
    ha = normed(xa_ref, moda_ref)
    _gdn_block(p0_scr, p1_scr, seq_start, lambda i: in_proj_piece(ha, p1_scr, i),
               cw_ref, alog_ref, dtb_ref, onorm_ref, o_ref, 0, *scratch, n_heads=n_heads, rows=rows,
               n_pieces=n_pieces)
    hb = normed(xb_ref, modb_ref)
    _gdn_block(p1_scr, p0_scr, False, lambda i: in_proj_piece(hb, p0_scr, i),
               cw_ref, alog_ref, dtb_ref, onorm_ref, o_ref, rows, *scratch, n_heads=n_heads, rows=rows,
               n_pieces=n_pieces)


def _gdn_in(x2, mod_s, g_pre, w_g, conv_w, a_log, dt_bias, o_norm, seq):
    m, d = x2.shape
    n_heads = a_log.shape[0]
    assert n_heads % GDN_PACK == 0 and GDN_HEAD_DIM == LANES
    gw = n_heads * GDN_HEAD_DIM
    n_cols = w_g.shape[1]
    assert n_cols == 4 * gw + LANES
    rows = _pick(seq, GDN_BLOCK)
    bps = seq // rows
    nb = m // rows
    assert bps % 2 == 0 and nb % 2 == 0
    last = nb - 1
    n_chunks = rows // CHUNK
    n_prob = n_chunks * (n_heads // GDN_PACK)
    pad = LANES - 2 * n_heads
    alog_row = jnp.pad(a_log, (n_heads, pad)).reshape(1, LANES)
    dtb_row = jnp.pad(dt_bias, (n_heads, pad)).reshape(1, LANES)
    xa = lambda s: jnp.maximum(2 * s - 1, 0)
    xb = lambda s: jnp.minimum(2 * s, last)
    return pl.pallas_call(
        functools.partial(_gdn_in_kernel, n_heads=n_heads, rows=rows, blocks_per_seq=bps),
        grid=(nb // 2 + 1,),
        in_specs=[
            pl.BlockSpec((rows, d), lambda s: (xa(s), 0)),
            pl.BlockSpec((1, 3, d), lambda s: (xa(s) // bps, 0, 0)),
            pl.BlockSpec((rows, d), lambda s: (xb(s), 0)),
            pl.BlockSpec((1, 3, d), lambda s: (xb(s) // bps, 0, 0)),
            pl.BlockSpec((1, d), lambda s: (0, 0)),
            pl.BlockSpec((d, n_cols), lambda s: (0, 0)),
            pl.BlockSpec((CONV_WIDTH, 3 * gw), lambda s: (0, 0)),
            pl.BlockSpec((1, LANES), lambda s: (0, 0)),
            pl.BlockSpec((1, LANES), lambda s: (0, 0)),
            pl.BlockSpec((1, GDN_HEAD_DIM), lambda s: (0, 0)),
        ],
        out_specs=pl.BlockSpec((2 * rows, gw), lambda s: (jnp.maximum(s - 1, 0), 0)),
        out_shape=jax.ShapeDtypeStruct((m, gw), BF16),
        scratch_shapes=[
            pltpu.VMEM((rows + SUBLANES, n_cols), F32),
            pltpu.VMEM((rows + SUBLANES, n_cols), F32),
            pltpu.VMEM((n_heads, GDN_HEAD_DIM, GDN_HEAD_DIM), F32),
            pltpu.VMEM((rows, gw), BF16),
            pltpu.VMEM((rows, gw), BF16),
            pltpu.VMEM((rows, gw), BF16),
            pltpu.VMEM((rows, gw), F32),
            pltpu.VMEM((rows, gw), F32),
            pltpu.VMEM((rows, gw), BF16),
            pltpu.VMEM((rows, gw), F32),
            pltpu.VMEM((n_prob, CHUNK, GDN_PACK * CHUNK), F32),
            pltpu.VMEM((n_prob, CHUNK, GDN_PACK * CHUNK), F32),
            pltpu.VMEM((n_prob, CHUNK, GDN_PACK * CHUNK), F32),
            pltpu.VMEM((n_chunks, n_heads // 2, GDN_HEAD_DIM + CHUNK, 2 * GDN_HEAD_DIM), BF16),
            pltpu.VMEM((n_chunks, n_heads, GDN_HEAD_DIM, GDN_HEAD_DIM), F32),
            pltpu.VMEM((rows, gw), F32),
        ],
        compiler_params=_cparams("arbitrary"),
        name="gdn_in",
    )(x2, mod_s, x2, mod_s, g_pre.reshape(1, d), w_g, conv_w, alog_row, dtb_row,
      o_norm.reshape(1, GDN_HEAD_DIM))


def _rotary(x, cos, sin):
    half = x.shape[-1] // 2
    x1, x2 = x[:, :half], x[:, half:]
    return jnp.concatenate([x1 * cos - x2 * sin, x2 * cos + x1 * sin], axis=-1)


RET_PIECES = 8


def _ret_block(p_ref, seq_start, between, cos, sin, norm_ref, o_ref, row0, s_scr, *, n_heads, rows,
               n_pieces):
    dk = RET_HEAD_DIM
    rw = n_heads * dk
    ri = lax.broadcasted_iota(jnp.int32, (rows, rows), 0)
    ci = lax.broadcasted_iota(jnp.int32, (rows, rows), 1)
    delta = (ri - ci).astype(F32)
    causal = ri >= ci
    pos = lax.broadcasted_iota(jnp.int32, (rows, 1), 0).astype(F32)
    done = 0

    for h in range(n_heads):
        while done < min(n_pieces, (h + 1) * -(-n_pieces // n_heads)):
            between(done)
            done += 1
        lo, hi = h * dk, (h + 1) * dk
        log_gamma = math.log1p(-(2.0 ** (-5.0 - h)))
        q = _rotary(p_ref[:, lo:hi], cos, sin)
        k = _rotary(p_ref[:, rw + lo:rw + hi], cos, sin) * dk ** -0.5
        v_b = p_ref[:, 2 * rw + lo:2 * rw + hi].astype(BF16)
        intra = jnp.exp(jnp.where(causal, delta * log_gamma, -jnp.inf))
        scores = _dot_nt(q.astype(BF16), k.astype(BF16)) * intra
        q_in = q * jnp.exp((pos + 1.0) * log_gamma)
        k_st = k * jnp.exp((rows - 1.0 - pos) * log_gamma)
        state = jnp.where(seq_start, 0.0, s_scr[h])
        o = _dot(scores.astype(BF16), v_b) + _dot(q_in.astype(BF16), state.astype(BF16))
        s_scr[h] = state * math.exp(rows * log_gamma) + _dot_tn(k_st.astype(BF16), v_b)
        out = _rms(o, norm_ref[:, lo:hi]) * _silu(p_ref[:, 3 * rw + lo:3 * rw + hi])
        o_ref[row0:row0 + rows, lo:hi] = out.astype(o_ref.dtype)

    while done < n_pieces:
        between(done)
        done += 1


def _ret_in_kernel(xa_ref, moda_ref, xb_ref, modb_ref, gpre_ref, win_ref, cos_ref, sin_ref, norm_ref,
                   o_ref, p0_scr, p1_scr, s_scr, *, n_heads, rows, blocks_per_seq):
    s = pl.program_id(0)
    piece = win_ref.shape[1] // RET_PIECES

    def in_proj_piece(h, p_ref, i):
        p_ref[:, i * piece:(i + 1) * piece] = _dot(h, win_ref[:, i * piece:(i + 1) * piece])

    def normed(x_ref, mod_ref):
        return _norm_mod(x_ref[...], gpre_ref[...], mod_ref).astype(BF16)

    @pl.when(s == 0)
    def _():
        p0_scr[...] = jnp.zeros_like(p0_scr)
        p1_scr[...] = jnp.zeros_like(p1_scr)
        s_scr[...] = jnp.zeros_like(s_scr)

    seq_start = (2 * s - 2) % blocks_per_seq == 0
    ha = normed(xa_ref, moda_ref)
    _ret_block(p0_scr, seq_start, lambda i: in_proj_piece(ha, p1_scr, i), cos_ref[0:rows, :],
               sin_ref[0:rows, :], norm_ref, o_ref, 0, s_scr, n_heads=n_heads, rows=rows,
               n_pieces=RET_PIECES)
    hb = normed(xb_ref, modb_ref)
    _ret_block(p1_scr, False, lambda i: in_proj_piece(hb, p0_scr, i), cos_ref[rows:2 * rows, :],
               sin_ref[rows:2 * rows, :], norm_ref, o_ref, rows, s_scr, n_heads=n_heads, rows=rows,
               n_pieces=RET_PIECES)


def _ret_in(x2, mod_s, g_pre, w_r, ret_norm, seq):
    m, d = x2.shape
    rw = ret_norm.shape[0]
    n_heads = rw // RET_HEAD_DIM
    n_cols = w_r.shape[1]
    assert n_cols == 4 * rw and n_cols % (RET_PIECES * LANES) == 0
    rows = _pick(seq, 256)
    bps = seq // rows
    nb = m // rows
    assert bps % 2 == 0 and nb % 2 == 0
    last = nb - 1
    half = RET_HEAD_DIM // 2
    inv_freq = ROPE_BASE ** (-jnp.arange(half, dtype=F32) / half)
    ang = jnp.arange(seq, dtype=F32)[:, None] * inv_freq[None, :]
    cos, sin = jnp.cos(ang), jnp.sin(ang)
    xa = lambda s: jnp.maximum(2 * s - 1, 0)
    xb = lambda s: jnp.minimum(2 * s, last)
    pos_blk = lambda s: (jnp.maximum(2 * s - 2, 0) % bps) // 2
    return pl.pallas_call(
        functools.partial(_ret_in_kernel, n_heads=n_heads, rows=rows, blocks_per_seq=bps),
        grid=(nb // 2 + 1,),
        in_specs=[
            pl.BlockSpec((rows, d), lambda s: (xa(s), 0)),
            pl.BlockSpec((1, 3, d), lambda s: (xa(s) // bps, 0, 0)),
            pl.BlockSpec((rows, d), lambda s: (xb(s), 0)),
            pl.BlockSpec((1, 3, d), lambda s: (xb(s) // bps, 0, 0)),
            pl.BlockSpec((1, d), lambda s: (0, 0)),
            pl.BlockSpec((d, n_cols), lambda s: (0, 0)),
            pl.BlockSpec((2 * rows, half), lambda s: (pos_blk(s), 0)),
            pl.BlockSpec((2 * rows, half), lambda s: (pos_blk(s), 0)),
            pl.BlockSpec((1, rw), lambda s: (0, 0)),
        ],
        out_specs=pl.BlockSpec((2 * rows, rw), lambda s: (jnp.maximum(s - 1, 0), 0)),
        out_shape=jax.ShapeDtypeStruct((m, rw), BF16),
        scratch_shapes=[
            pltpu.VMEM((rows, n_cols), F32),
            pltpu.VMEM((rows, n_cols), F32),
            pltpu.VMEM((n_heads, RET_HEAD_DIM, RET_HEAD_DIM), F32),
        ],
        compiler_params=_cparams("arbitrary"),
        name="retention_in",
    )(x2, mod_s, x2, mod_s, g_pre.reshape(1, d), w_r, cos, sin, ret_norm.reshape(1, rw))


LRU_ROWS = 256


def _lru_block(p_ref, p_prev_ref, seq_start, between, perm_ref, cw_ref, cb_ref, wa_ref, ba_ref, wx_ref,
               bx_ref, lam_ref, o_ref, row0, h_scr, act_scr, *, n_blocks, rows, bw):
    width = n_blocks * bw
    nj = rows // SUBLANES
    sub = lax.broadcasted_iota(jnp.int32, (SUBLANES, bw), 0)
    tails = [jnp.where(seq_start, 0.0, p_prev_ref[rows - SUBLANES * (k - 1) - 1:rows - SUBLANES * (k - 1),
                                                  width:2 * width]) for k in range(1, CONV_WIDTH)]

    for n in range(n_blocks):
        between(2 * n)
        lo, hi = n * bw, (n + 1) * bw
        xp = p_ref[:, width + lo:width + hi]

        def edge(k):
            prev = pltpu.roll(xp[rows - k * SUBLANES:rows - (k - 1) * SUBLANES], 1, 0)
            return jnp.where(sub == 0, tails[k - 1][:, lo:hi], prev)

        edges = [edge(k) for k in range(1, CONV_WIDTH)]
        xc = xp * cw_ref[CONV_WIDTH - 1:CONV_WIDTH, lo:hi]
        for back in range(1, CONV_WIDTH):
            tap = CONV_WIDTH - 1 - back
            shifted = jnp.concatenate(edges[:back][::-1] + [xp[:rows - back * SUBLANES]], axis=0)
            xc = xc + shifted * cw_ref[tap:tap + 1, lo:hi]
        xc = xc + cb_ref[:, lo:hi]
        xc_b = xc.astype(BF16)
        r = _sigmoid(_dot(xc_b, wa_ref[n]) + ba_ref[:, lo:hi])
        i = _sigmoid(_dot(xc_b, wx_ref[n]) + bx_ref[:, lo:hi])
        log_a = (-LRU_C * r) * _softplus(-lam_ref[:, lo:hi])
        a = jnp.exp(log_a)
        mult = jnp.sqrt(-jnp.tanh(log_a) * (a * a + 1.0))
        bv = mult * (i * xc)

        h = bv[0:SUBLANES]
        ac = a[0:SUBLANES]
        hl, al = [h], [ac]
        for j in range(1, nj):
            aj = a[j * SUBLANES:(j + 1) * SUBLANES]
            h = aj * h + bv[j * SUBLANES:(j + 1) * SUBLANES]
            ac = aj * ac
            hl.append(h)
            al.append(ac)
        between(2 * n + 1)
        c = jnp.where(seq_start, 0.0, h_scr[0:1, lo:hi])
        cs = []
        for s in range(SUBLANES):
            cs.append(c)
            c = h[s:s + 1] + ac[s:s + 1] * c
        h_scr[0:1, lo:hi] = c
        c_in = jnp.concatenate(cs, axis=0)
        hs = jnp.concatenate([hl[j] + al[j] * c_in for j in range(nj)], axis=0)
        act_scr[:, lo:hi] = (hs * _gelu_tanh(p_ref[:, lo:hi])).astype(BF16)

    o_ref[row0:row0 + rows, :] = _dot_tn(perm_ref[...], act_scr[...]).astype(o_ref.dtype)


def _lru_in_kernel(x0_ref, mod0_ref, xa_ref, moda_ref, xb_ref, modb_ref, gpre_ref, win_ref, perm_ref,
                   cw_ref, cb_ref, wa_ref, ba_ref, wx_ref, bx_ref, lam_ref, o_ref,
                   p0_scr, p1_scr, h_scr, act_scr, *, n_blocks, rows, bw, blocks_per_seq):
    s = pl.program_id(0)

    def in_proj_cols(h, p_ref, n):
        cw = win_ref.shape[1] // (2 * n_blocks)
        p_ref[:, n * cw:(n + 1) * cw] = _dot(h, win_ref[:, n * cw:(n + 1) * cw])

    def normed(x_ref, mod_ref):
        h = _norm_mod(x_ref[...], gpre_ref[...], mod_ref).astype(BF16)
        return _dot(perm_ref[...], h).astype(BF16)

    @pl.when(s == 0)
    def _():
        h0 = normed(x0_ref, mod0_ref)
        for n in range(2 * n_blocks):
            in_proj_cols(h0, p0_scr, n)
        p1_scr[...] = jnp.zeros_like(p1_scr)
        h_scr[...] = jnp.zeros_like(h_scr)

    lru = functools.partial(_lru_block, perm_ref=perm_ref, cw_ref=cw_ref, cb_ref=cb_ref, wa_ref=wa_ref,
                            ba_ref=ba_ref, wx_ref=wx_ref, bx_ref=bx_ref, lam_ref=lam_ref, o_ref=o_ref,
                            h_scr=h_scr, act_scr=act_scr, n_blocks=n_blocks, rows=rows, bw=bw)
    seq_start = (2 * s) % blocks_per_seq == 0
    ha = normed(xa_ref, moda_ref)
    lru(p0_scr, p1_scr, seq_start, lambda n: in_proj_cols(ha, p1_scr, n), row0=0)
    hb = normed(xb_ref, modb_ref)
    lru(p1_scr, p0_scr, False, lambda n: in_proj_cols(hb, p0_scr, n), row0=rows)


def _lru_in(x2, mod_s, g_pre, w_in, conv_w, conv_b, gate_a_w, gate_a_b, gate_x_w, gate_x_b, lam, seq):
    m, d = x2.shape
    width = w_in.shape[1] // 2
    n_blocks, bw, _ = gate_a_w.shape
    rows = _pick(seq, LRU_ROWS)
    bps = seq // rows
    nb = m // rows
    assert bps % 2 == 0 and nb % 2 == 0
    last = nb - 1
    r_idx = jnp.arange(rows)
    t_of_r = (r_idx % SUBLANES) * (rows // SUBLANES) + r_idx // SUBLANES
    perm = (t_of_r[:, None] == r_idx[None, :]).astype(BF16)
    xa = lambda s: 2 * s + 1
    xb = lambda s: jnp.minimum(2 * s + 2, last)
    vec = lambda: pl.BlockSpec((1, width), lambda s: (0, 0))
    gate = lambda: pl.BlockSpec((n_blocks, bw, bw), lambda s: (0, 0, 0))
    return pl.pallas_call(
        functools.partial(_lru_in_kernel, n_blocks=n_blocks, rows=rows, bw=bw, blocks_per_seq=bps),
        grid=(nb // 2,),
        in_specs=[
            pl.BlockSpec((rows, d), lambda s: (0, 0)),
            pl.BlockSpec((1, 3, d), lambda s: (0, 0, 0)),
            pl.BlockSpec((rows, d), lambda s: (xa(s), 0)),
            pl.BlockSpec((1, 3, d), lambda s: (xa(s) // bps, 0, 0)),
            pl.BlockSpec((rows, d), lambda s: (xb(s), 0)),
            pl.BlockSpec((1, 3, d), lambda s: (xb(s) // bps, 0, 0)),
            pl.BlockSpec((1, d), lambda s: (0, 0)),
            pl.BlockSpec((d, 2 * width), lambda s: (0, 0)),
            pl.BlockSpec((rows, rows), lambda s: (0, 0)),
            pl.BlockSpec((CONV_WIDTH, width), lambda s: (0, 0)),
            vec(), gate(), vec(), gate(), vec(), vec(),
        ],
        out_specs=pl.BlockSpec((2 * rows, width), lambda s: (s, 0)),
        out_shape=jax.ShapeDtypeStruct((m, width), BF16),
        scratch_shapes=[
            pltpu.VMEM((rows, 2 * width), F32),
            pltpu.VMEM((rows, 2 * width), F32),
            pltpu.VMEM((SUBLANES, width), F32),
            pltpu.VMEM((rows, width), BF16),
        ],
        compiler_params=_cparams("arbitrary"),
        name="rglru_in",
    )(x2, mod_s, x2, mod_s, x2, mod_s, g_pre.reshape(1, d), w_in, perm, conv_w, conv_b.reshape(1, width),
      gate_a_w.astype(BF16), gate_a_b.reshape(1, width), gate_x_w.astype(BF16),
      gate_x_b.reshape(1, width), lam.reshape(1, width))


def kernel(x, c, ada_w, ada_b, norm_pre, norm_post, ffn_w13, ffn_w2, ev_w_in, ev_conv_w, ev_a_log,
           ev_dt_bias, ev_o_norm, ev_ret_norm, ev_w_out, od_w_in, od_conv_w, od_conv_b,
           od_gate_a_w, od_gate_a_b, od_gate_x_w, od_gate_x_b, od_lambda, od_w_out):
    b, seq, d = x.shape
    depth = ada_w.shape[0]
    m = b * seq
    mod = _ada(c, ada_w, ada_b).reshape(depth, b, N_SUB, 3, d)
    x2 = x.reshape(m, d)
    w13_b = ffn_w13.astype(BF16)
    w2_b = ffn_w2.astype(BF16)

    for layer in range(depth):
        mod_l = mod[layer]
        x2 = _ffn(x2, mod_l[:, 0], norm_pre[layer, 0], norm_post[layer, 0], w13_b, w2_b, layer, 0, seq, 0.5)
        if layer % 2 == 0:
            e = layer // 2
            n_heads = ev_a_log.shape[1]
            gw = n_heads * GDN_HEAD_DIM
            w_in = ev_w_in[e].astype(BF16)
            w_small = jnp.pad(w_in[:, 3 * gw:3 * gw + 2 * n_heads], ((0, 0), (0, LANES - 2 * n_heads)))
            w_g = jnp.concatenate([w_in[:, :3 * gw], w_in[:, 3 * gw + 2 * n_heads:4 * gw + 2 * n_heads],
                                   w_small], axis=1)
            w_r = w_in[:, 4 * gw + 2 * n_heads:]
            o_a = _gdn_in(x2, mod_l[:, 1], norm_pre[layer, 1], w_g, ev_conv_w[e], ev_a_log[e],
                          ev_dt_bias[e], ev_o_norm[e], seq)
            o_b = _ret_in(x2, mod_l[:, 1], norm_pre[layer, 1], w_r, ev_ret_norm[e], seq)
            acts = [o_a, o_b]
            w_out = ev_w_out[e]
        else:
            o = layer // 2
            hs = _lru_in(x2, mod_l[:, 1], norm_pre[layer, 1], od_w_in[o].astype(BF16), od_conv_w[o],
                         od_conv_b[o], od_gate_a_w[o], od_gate_a_b[o], od_gate_x_w[o], od_gate_x_b[o],
                         od_lambda[o], seq)
            acts = [hs.reshape(m, -1)]
            w_out = od_w_out[o]
        x2 = _outproj(acts, w_out.astype(BF16), x2, mod_l[:, 1], norm_post[layer, 1], seq, 1.0)
        x2 = _ffn(x2, mod_l[:, 2], norm_pre[layer, 2], norm_post[layer, 2], w13_b, w2_b, layer, 1, seq, 0.5)
    return x2.reshape(b, seq, d)
```
